```python
import math
import jax, jax.numpy as jnp
from jax import lax
import numpy as np

D_MODEL = 1024
BATCH = 4
SEQ = 4096
DEPTH = 1
DEC_BATCH = 128
DEC_SEQ = 1
PAST_LEN = 8192
PAGE_SIZE = 128

D_MIX = D_MODEL
M_WIDTH = D_MIX // 2
M_HEADS = 4
M_HD = M_WIDTH // M_HEADS
M_CHUNK = 128
A_WIDTH = D_MIX - M_WIDTH
A_HEADS = 8
A_HD = A_WIDTH // A_HEADS
A_KV = 2
A_GROUP = A_HEADS // A_KV
A_KVW = A_KV * A_HD
CMP_LEN = 32
CMP_STRIDE = 16
SLC_BLOCK = 64
N_SEL = 16
SLC_Q_BLK = 64
WINDOW = 512
Q_BLK = 128
N_BUCKETS = 32
MAX_DIST = 128
LN_EPS = 1e-5
ATT_SCALE = A_HD ** -0.5
DEEPNORM_ALPHA = (2.0 * DEPTH) ** 0.25
DEEPNORM_BETA = (8.0 * DEPTH) ** -0.25
IN_SPLITS = (M_WIDTH,) * 5 + (M_HEADS, M_HEADS) + (A_WIDTH,) + (A_KVW,) * 6 + (3 * A_HEADS, A_WIDTH)
D_IN = sum(IN_SPLITS)
F_GATE_OFF = 5 * M_WIDTH + M_HEADS

kernel_name = 'hymba_mlstm_nsa_deepnorm_step'


def layer_norm(x, g=None, b=None):
    xf = x.astype(jnp.float32)
    mu = xf.mean(-1, keepdims=True)
    var = jnp.square(xf - mu).mean(-1, keepdims=True)
    y = (xf - mu) * lax.rsqrt(var + LN_EPS)
    if g is not None:
        y = y * g + b
    return y


def masked_softmax(s, mask):
    s = jnp.where(mask, s.astype(jnp.float32), -jnp.inf)
    m = jnp.max(s, axis=-1, keepdims=True)
    m = jnp.where(jnp.isfinite(m), m, 0.0)
    e = jnp.exp(s - m)
    tot = e.sum(-1, keepdims=True)
    return e / jnp.where(tot > 0, tot, 1.0)


def t5_bucket(dist):
    n = jnp.maximum(dist, 0)
    max_exact = N_BUCKETS // 2
    nf = jnp.maximum(n, 1).astype(jnp.float32)
    large = max_exact + (jnp.log(nf / max_exact) / math.log(MAX_DIST / max_exact)
                         * (N_BUCKETS - max_exact)).astype(jnp.int32)
    large = jnp.minimum(large, N_BUCKETS - 1)
    return jnp.where(n < max_exact, n, large)


def adaln_project(x, c, w_ada, b_ada, w_in, b_in):
    mod = jax.nn.silu(c.astype(jnp.float32)) @ w_ada + b_ada
    shift, scale, gate = jnp.split(mod[:, None, :], 3, axis=-1)
    h = layer_norm(x) * (1.0 + scale) + shift
    split_points = np.cumsum(IN_SPLITS)[:-1].tolist()
    parts = jnp.split(h @ w_in + b_in, split_points, axis=-1)
    return parts, gate


def residual_out(x, mix, gate, w_out, b_out, ln_g, ln_b):
    y = mix @ w_out + b_out
    return layer_norm(DEEPNORM_ALPHA * x.astype(jnp.float32) + gate * y, ln_g, ln_b)


def mlstm_chunk(carry, inp):
    C, n, m = carry
    q, k, v, ig, lf = inp
    L = q.shape[2]
    b = jnp.cumsum(lf, axis=-1)
    inter = b + m[..., None]
    D = b[..., :, None] - b[..., None, :] + ig[..., None, :]
    D = jnp.where(jnp.tril(jnp.ones((L, L), dtype=bool)), D, -jnp.inf)
    m_t = jnp.maximum(inter, D.max(-1))
    w_inter = jnp.exp(inter - m_t)
    qk = jnp.einsum('bhtd,bhsd->bhts', q, k) * jnp.exp(D - m_t[..., None])
    num = w_inter[..., None] * jnp.einsum('bhtd,bhde->bhte', q, C) + jnp.einsum('bhts,bhse->bhte', qk, v)
    den = w_inter * jnp.einsum('bhtd,bhd->bht', q, n) + qk.sum(-1)
    h = num / jnp.maximum(jnp.abs(den), jnp.exp(-m_t))[..., None]
    m_new = m_t[..., -1]
    w_c = jnp.exp(inter[..., -1] - m_new)
    w_s = jnp.exp(b[..., -1:] - b + ig - m_new[..., None])
    C_new = w_c[..., None, None] * C + jnp.einsum('bhs,bhsd,bhse->bhde', w_s, k, v)
    n_new = w_c[..., None] * n + jnp.einsum('bhs,bhsd->bhd', w_s, k)
    return (C_new, n_new, m_new), h


def mlstm_mixer(q, k, v, o_pre, i_pre, f_pre, norm_g, carry, chunk):
    B, T, _ = q.shape
    nc = T // chunk

    def heads(a):
        return a.astype(jnp.float32).reshape(B, nc, chunk, M_HEADS, M_HD).transpose(1, 0, 3, 2, 4)

    def gates(a):
        return a.astype(jnp.float32).reshape(B, nc, chunk, M_HEADS).transpose(1, 0, 3, 2)

    inp = (heads(q), heads(k) * (M_HD ** -0.5), heads(v), gates(i_pre), jax.nn.log_sigmoid(gates(f_pre)))
    carry, h = lax.scan(mlstm_chunk, carry, inp)
    h = h.transpose(1, 0, 3, 2, 4).reshape(B, T, M_HEADS, M_HD)
    h = layer_norm(h) * norm_g.reshape(M_HEADS, M_HD)
    h = h.reshape(B, T, M_WIDTH)
    return jax.nn.sigmoid(o_pre.astype(jnp.float32)) * h, carry


def kv_rows(k, v):
    B, T, _ = k.shape
    return jnp.stack([k.reshape(B, T, A_KV, A_HD), v.reshape(B, T, A_KV, A_HD)], axis=3)


def compress(rows, pe, w1, b1, w2):
    B, L = rows.shape[:2]
    n_half = L // CMP_STRIDE
    r = CMP_LEN // CMP_STRIDE
    n_cmp = n_half - r + 1
    halves = rows[:, :n_half * CMP_STRIDE].astype(jnp.float32).reshape(B, n_half, CMP_STRIDE, A_KV, 2, A_HD)
    pre = b1
    for j in range(r):
        wj = w1[:, j * CMP_STRIDE:(j + 1) * CMP_STRIDE]
        pj = pe[:, j * CMP_STRIDE:(j + 1) * CMP_STRIDE]
        proj = jnp.einsum('bnpkcd,cpdh->bnkch', halves, wj) + jnp.einsum('cpd,cpdh->ch', pj, wj)
        pre = pre + proj[:, j:j + n_cmp]
    return jnp.einsum('bnkch,che->bnkce', jax.nn.gelu(pre), w2)


def cmp_attend(q, kc, t_pos, rel_bias):
    T, N = q.shape[1], kc.shape[1]
    ends = jnp.arange(N, dtype=jnp.int32) * CMP_STRIDE + (CMP_LEN - 1)
    dist = t_pos[:, None] - ends[None, :]
    bias = rel_bias[t5_bucket(dist)].reshape(T, N, A_KV, A_GROUP).transpose(2, 3, 0, 1)
    s = jnp.einsum('btkgd,bnkd->bkgtn', q, kc[..., 0, :]) * ATT_SCALE + bias
    p = masked_softmax(s, dist >= 0)
    o = jnp.einsum('bkgtn,bnkd->btkgd', p, kc[..., 1, :])
    return o, p.sum(axis=2)


def select_blocks(imp, t_pos, L):
    n_cmp = imp.shape[-1]
    n_slc = -(-L // SLC_BLOCK)
    c_start = jnp.arange(n_cmp) * CMP_STRIDE
    c_end = c_start + CMP_LEN - 1
    s_start = jnp.arange(n_slc) * SLC_BLOCK
    s_end = s_start + SLC_BLOCK - 1
    cover = ((c_start[:, None] <= s_end[None]) & (c_end[:, None] >= s_start[None])).astype(jnp.float32)
    score = imp @ cover
    blk = jnp.arange(n_slc)[None, :]
    cur = (t_pos // SLC_BLOCK)[:, None]
    valid = s_start[None, :] <= t_pos[:, None]
    forced = (blk == 0) | (blk == cur) | (blk == cur - 1)
    score = jnp.where(forced, jnp.inf, jnp.where(valid, score, -jnp.inf))
    _, idx = lax.top_k(score, min(N_SEL, n_slc))
    return idx


def slc_attend(q, rows, pos, t_pos, rel_bias):
    dist = t_pos[None, None, :, None] - pos
    rb = rel_bias.reshape(N_BUCKETS, A_KV, A_GROUP)
    kv_i = jnp.arange(A_KV)[None, :, None, None]
    bias = jnp.moveaxis(rb[t5_bucket(dist), kv_i], -1, 2)
    rows = rows.astype(jnp.float32)
    s = jnp.einsum('btkgd,bktmd->bkgtm', q, rows[..., 0, :]) * ATT_SCALE + bias
    p = masked_softmax(s, (dist >= 0)[:, :, None])
    return jnp.einsum('bkgtm,bktmd->btkgd', p, rows[..., 1, :])


def window_prompt(q, win_rows, rel_bias):
    B, T = q.shape[:2]
    nb = T // Q_BLK
    span = WINDOW + Q_BLK
    pad = jnp.pad(win_rows.astype(jnp.float32), ((0, 0), (WINDOW, 0), (0, 0), (0, 0), (0, 0)))
    rel = jnp.arange(Q_BLK)[:, None] - jnp.arange(span)[None, :] + WINDOW
    band_mask = (rel >= 0) & (rel <= WINDOW)
    bias = rel_bias[t5_bucket(rel)].reshape(Q_BLK, span, A_KV, A_GROUP).transpose(2, 3, 0, 1)
    q_blk = jnp.moveaxis(q.reshape(B, nb, Q_BLK, A_KV, A_GROUP, A_HD), 1, 0)

    def one(args):
        qb, b = args
        band = lax.dynamic_slice_in_dim(pad, b * Q_BLK, span, axis=1)
        s_pos = b * Q_BLK - WINDOW + jnp.arange(span)
        mask = band_mask & (s_pos >= 0)[None, :]
        s = jnp.einsum('bqkgd,bmkd->bkgqm', qb, band[..., 0, :]) * ATT_SCALE + bias
        p = masked_softmax(s, mask)
        return jnp.einsum('bkgqm,bmkd->bqkgd', p, band[..., 1, :])

    o = lax.map(one, (q_blk, jnp.arange(nb)))
    return jnp.moveaxis(o, 0, 1).reshape(B, T, A_KV, A_GROUP, A_HD)


def window_sample(q, keys, t_pos, s0, rel_bias):
    T, M = q.shape[1], keys.shape[1]
    keys = keys.astype(jnp.float32)
    rel = t_pos[:, None] - (s0 + jnp.arange(M))[None, :]
    mask = (rel >= 0) & (rel <= WINDOW)
    bias = rel_bias[t5_bucket(rel)].reshape(T, M, A_KV, A_GROUP).transpose(2, 3, 0, 1)
    s = jnp.einsum('btkgd,bmkd->bkgtm', q, keys[..., 0, :]) * ATT_SCALE + bias
    p = masked_softmax(s, mask)
    return jnp.einsum('bkgtm,bmkd->btkgd', p, keys[..., 1, :])


def gate_merge(ga, o_c, o_s, o_w):
    B, T, _ = ga.shape
    g = jax.nn.sigmoid(ga.astype(jnp.float32)).reshape(B, T, 3, A_KV, A_GROUP, 1)
    o = g[:, :, 0] * o_c + g[:, :, 1] * o_s + g[:, :, 2] * o_w
    return o.reshape(B, T, A_WIDTH)


def nsa_prompt(qa, cmp_rows, slc_rows, win_rows, ga, cmp_params, rel_bias):
    B, T, _ = qa.shape
    q = qa.astype(jnp.float32).reshape(B, T, A_KV, A_GROUP, A_HD)
    t_pos = jnp.arange(T, dtype=jnp.int32)
    kc = compress(cmp_rows, *cmp_params)
    o_c, imp = cmp_attend(q, kc, t_pos, rel_bias)
    idx = select_blocks(imp, t_pos, T)
    n_slc = T // SLC_BLOCK
    blocks = slc_rows.astype(jnp.float32).transpose(0, 2, 1, 3, 4).reshape(B, A_KV, n_slc, SLC_BLOCK, 2, A_HD)
    b_i = jnp.arange(B)[:, None, None, None]
    kv_i = jnp.arange(A_KV)[None, :, None, None]
    nqb = T // SLC_Q_BLK

    def one(args):
        qb, ib, tb = args
        kk = ib.shape[-1]
        rows = blocks[b_i, kv_i, ib].reshape(B, A_KV, SLC_Q_BLK, kk * SLC_BLOCK, 2, A_HD)
        pos = (ib[..., None] * SLC_BLOCK + jnp.arange(SLC_BLOCK)).reshape(B, A_KV, SLC_Q_BLK, kk * SLC_BLOCK)
        return slc_attend(qb, rows, pos, tb, rel_bias)

    q_blk = jnp.moveaxis(q.reshape(B, nqb, SLC_Q_BLK, A_KV, A_GROUP, A_HD), 1, 0)
    i_blk = jnp.moveaxis(idx.reshape(B, A_KV, nqb, SLC_Q_BLK, idx.shape[-1]), 2, 0)
    t_blk = t_pos.reshape(nqb, SLC_Q_BLK)
    o_s = lax.map(one, (q_blk, i_blk, t_blk))
    o_s = jnp.moveaxis(o_s, 0, 1).reshape(B, T, A_KV, A_GROUP, A_HD)
    o_w = window_prompt(q, win_rows, rel_bias)
    return gate_merge(ga, o_c, o_s, o_w)


def nsa_sample(qa, cmp_new, slc_new, win_new, ga, cache_cmp_kv, cache_slc_kv, win_cache,
               page_table, layer, cmp_params, rel_bias):
    DB, T, _ = qa.shape
    P = PAST_LEN
    L = P + T
    q = qa.astype(jnp.float32).reshape(DB, T, A_KV, A_GROUP, A_HD)
    t_pos = P + jnp.arange(T, dtype=jnp.int32)
    past_cmp = cache_cmp_kv[layer, page_table].reshape(DB, P, A_KV, 2, A_HD)
    kc = compress(jnp.concatenate([past_cmp, cmp_new], axis=1), *cmp_params)
    o_c, imp = cmp_attend(q, kc, t_pos, rel_bias)
    idx = select_blocks(imp, t_pos, L)
    kk = idx.shape[-1]
    pos = (idx[..., None] * SLC_BLOCK + jnp.arange(SLC_BLOCK)).reshape(DB, A_KV, T, kk * SLC_BLOCK)
    b_i = jnp.arange(DB)[:, None, None, None]
    kv_i = jnp.arange(A_KV)[None, :, None, None]
    past_pos = jnp.minimum(pos, P - 1)
    phys = page_table[b_i, past_pos // PAGE_SIZE]
    rows_past = cache_slc_kv[layer, phys, past_pos % PAGE_SIZE, kv_i]
    rows_new = slc_new[b_i, jnp.clip(pos - P, 0, T - 1), kv_i]
    rows = jnp.where((pos < P)[..., None, None], rows_past, rows_new)
    o_s = slc_attend(q, rows, pos, t_pos, rel_bias)
    keys = jnp.concatenate([win_cache, win_new], axis=1)
    o_w = window_sample(q, keys, t_pos, P - win_cache.shape[1], rel_bias)
    return gate_merge(ga, o_c, o_s, o_w), keys[:, T:]


def setup_inputs(seed: int = 0) -> dict:
    key = jax.random.key(seed)
    ks = jax.random.split(key, 26)
    f32 = jnp.float32
    n_pages = PAST_LEN // PAGE_SIZE
    n_phys = (DEC_BATCH * n_pages * 5) // 4
    win_buf = min(WINDOW, PAST_LEN)

    def nrm(k, shape, s):
        return s * jax.random.normal(k, shape, f32)

    b_in = nrm(ks[13], (DEPTH, D_IN), 0.01)
    b_in = b_in.at[:, F_GATE_OFF:F_GATE_OFF + M_HEADS].add(jnp.linspace(3.0, 6.0, M_HEADS))
    page_table = jax.random.permutation(ks[8], n_phys)[:DEC_BATCH * n_pages].reshape(DEC_BATCH, n_pages).astype(jnp.int32)
    return {
        'x_prompt': nrm(ks[0], (BATCH, SEQ, D_MODEL), 1.0),
        'x_sample': nrm(ks[1], (DEC_BATCH, DEC_SEQ, D_MODEL), 1.0),
        'cache_cmp_kv': nrm(ks[2], (DEPTH, n_phys, PAGE_SIZE, A_KV, 2, A_HD), 1.0),
        'cache_slc_kv': nrm(ks[3], (DEPTH, n_phys, PAGE_SIZE, A_KV, 2, A_HD), 1.0),
        'cache_win_kv': nrm(ks[4], (DEPTH, DEC_BATCH, win_buf, A_KV, 2, A_HD), 1.0),
        'state_mlstm_C': nrm(ks[5], (DEPTH, DEC_BATCH, M_HEADS, M_HD, M_HD), 0.5),
        'state_mlstm_n': nrm(ks[6], (DEPTH, DEC_BATCH, M_HEADS, M_HD), 0.5),
        'state_mlstm_m': nrm(ks[7], (DEPTH, DEC_BATCH, M_HEADS), 1.0),
        'page_table': page_table,
        'c_prompt': nrm(ks[9], (BATCH, D_MODEL), 1.0),
        'c_sample': nrm(ks[10], (DEC_BATCH, D_MODEL), 1.0),
        'rel_bias': nrm(ks[11], (N_BUCKETS, A_HEADS), 0.5),
        'w_ada': nrm(ks[12], (DEPTH, D_MODEL, 3 * D_MODEL), 0.5 * D_MODEL ** -0.5),
        'b_ada': nrm(ks[14], (DEPTH, 3 * D_MODEL), 0.01),
        'w_in': nrm(ks[15], (DEPTH, D_MODEL, D_IN), D_MODEL ** -0.5),
        'b_in': b_in,
        'm_norm_g': 1.0 + nrm(ks[16], (DEPTH, M_WIDTH), 0.01),
        'cmp_pe': nrm(ks[17], (DEPTH, 2, CMP_LEN, A_HD), 0.02),
        'cmp_w1': nrm(ks[18], (DEPTH, 2, CMP_LEN, A_HD, A_HD), (CMP_LEN * A_HD) ** -0.5),
        'cmp_b1': nrm(ks[19], (DEPTH, 2, A_HD), 0.01),
        'cmp_w2': nrm(ks[20], (DEPTH, 2, A_HD, A_HD), A_HD ** -0.5),
        'w_out': nrm(ks[21], (DEPTH, D_MIX, D_MODEL), DEEPNORM_BETA * D_MIX ** -0.5),
        'b_out': nrm(ks[22], (DEPTH, D_MODEL), 0.01),
        'ln_g': 1.0 + nrm(ks[23], (DEPTH, D_MODEL), 0.01),
        'ln_b': nrm(ks[24], (DEPTH, D_MODEL), 0.01),
    }


def reference(x_prompt, x_sample, cache_cmp_kv, cache_slc_kv, cache_win_kv,
              state_mlstm_C, state_mlstm_n, state_mlstm_m, page_table,
              c_prompt, c_sample, rel_bias, w_ada, b_ada, w_in, b_in, m_norm_g,
              cmp_pe, cmp_w1, cmp_b1, cmp_w2, w_out, b_out, ln_g, ln_b):
    f32 = jnp.float32
    win_buf = min(WINDOW, PAST_LEN)
    x_p, x_s = x_prompt, x_sample
    outs = [[] for _ in range(12)]
    for l in range(DEPTH):
        cmp_params = (cmp_pe[l], cmp_w1[l], cmp_b1[l], cmp_w2[l])
        parts, gate = adaln_project(x_p, c_prompt, w_ada[l], b_ada[l], w_in[l], b_in[l])
        qm, km, vm, om, im, fm, zm = parts[0], parts[1], parts[2], parts[3], parts[5], parts[6], parts[4]
        qa, ck, cv, sk, sv, wk, wv, ga, za = parts[7:]
        B, T = x_p.shape[:2]
        carry0 = (jnp.zeros((B, M_HEADS, M_HD, M_HD), f32), jnp.zeros((B, M_HEADS, M_HD), f32),
                  jnp.zeros((B, M_HEADS), f32))
        hm, (C_p, n_p, m_p) = mlstm_mixer(qm, km, vm, om, im, fm, m_norm_g[l], carry0, M_CHUNK)
        cmp_p, slc_p, win_p = kv_rows(ck, cv), kv_rows(sk, sv), kv_rows(wk, wv)
        ha = nsa_prompt(qa, cmp_p, slc_p, win_p, ga, cmp_params, rel_bias)
        mix = jnp.concatenate([hm * jax.nn.silu(zm), ha * jax.nn.silu(za)], axis=-1)
        x_p = residual_out(x_p, mix, gate, w_out[l], b_out[l], ln_g[l], ln_b[l])
        wbuf_p = jnp.pad(win_p, ((0, 0), (max(win_buf - T, 0), 0), (0, 0), (0, 0), (0, 0)))[:, -win_buf:]
        parts, gate = adaln_project(x_s, c_sample, w_ada[l], b_ada[l], w_in[l], b_in[l])
        qm, km, vm, om, im, fm, zm = parts[0], parts[1], parts[2], parts[3], parts[5], parts[6], parts[4]
        qa, ck, cv, sk, sv, wk, wv, ga, za = parts[7:]
        carry = (state_mlstm_C[l].astype(f32), state_mlstm_n[l].astype(f32), state_mlstm_m[l].astype(f32))
        hm, (C_s, n_s, m_s) = mlstm_mixer(qm, km, vm, om, im, fm, m_norm_g[l], carry, x_s.shape[1])
        cmp_s, slc_s, win_s = kv_rows(ck, cv), kv_rows(sk, sv), kv_rows(wk, wv)
        ha, wbuf_s = nsa_sample(qa, cmp_s, slc_s, win_s, ga, cache_cmp_kv, cache_slc_kv, cache_win_kv[l],
                                page_table, l, cmp_params, rel_bias)
        mix = jnp.concatenate([hm * jax.nn.silu(zm), ha * jax.nn.silu(za)], axis=-1)
        x_s = residual_out(x_s, mix, gate, w_out[l], b_out[l], ln_g[l], ln_b[l])
        for o, a in zip(outs, (cmp_p, cmp_s, slc_p, slc_s, wbuf_p, wbuf_s, C_p, C_s, n_p, n_s, m_p, m_s)):
            o.append(a)
    new_state = [jnp.stack(o) for o in outs]
    return (x_p, x_s, *new_state)
```

```python
import functools
import math

import numpy as np
import jax
import jax.numpy as jnp
from jax import lax
from jax.experimental import pallas as pl
from jax.experimental.pallas import tpu as pltpu

F32 = jnp.float32
BF16 = jnp.bfloat16
HIGHEST = lax.Precision.HIGHEST

D_MODEL = 1024
M_HEADS = 4
M_HD = 128
M_WIDTH = M_HEADS * M_HD
M_CHUNK = 128
A_HEADS = 8
A_HD = 64
A_KV = 2
A_GROUP = A_HEADS // A_KV
A_WIDTH = A_HEADS * A_HD
A_KVW = A_KV * A_HD
ROW_W = 2 * A_KVW
CMP_LEN = 32
CMP_STRIDE = 16
SLC_BLOCK = 64
N_SEL = 16
WINDOW = 512
N_BUCKETS = 32
MAX_EXACT = N_BUCKETS // 2
MAX_DIST = 128
FAR_BUCKET = N_BUCKETS - 1
LN_EPS = 1e-5
ATT_SCALE = A_HD ** -0.5
DEPTH = 1
DEEPNORM_ALPHA = (2.0 * DEPTH) ** 0.25
IN_SPLITS = (M_WIDTH,) * 5 + (M_HEADS, M_HEADS) + (A_WIDTH,) + (A_KVW,) * 6 + (3 * A_HEADS, A_WIDTH)

LANE = 128
SUBLANE = 8
TQ = 128
NEG = -1e30
MASKED_ID = N_BUCKETS
VMEM_LIMIT = 56 * 1024 * 1024

PM_W = 5 * M_WIDTH + LANE
PA_W = 2 * A_WIDTH + LANE
PW_TOTAL = PM_W + PA_W + 3 * ROW_W


def _cparams(*sem):
    return pltpu.CompilerParams(dimension_semantics=sem, vmem_limit_bytes=VMEM_LIMIT)


def _nt(a, b):
    return lax.dot_general(a, b, (((1,), (1,)), ((), ())), preferred_element_type=F32)


def _dot(a, b, precision=None):
    return jnp.dot(a, b, preferred_element_type=F32, precision=precision)


def _log_sigmoid(x):
    return jnp.minimum(x, 0.0) - jnp.log(1.0 + jnp.exp(-jnp.abs(x)))


def _silu(x):
    return x * jax.nn.sigmoid(x)


def _gelu_tanh(x):
    return 0.5 * x * (1.0 + jnp.tanh(math.sqrt(2.0 / math.pi) * (x + 0.044715 * (x * x * x))))


def _ln_rows(x):
    mu = jnp.mean(x, axis=-1, keepdims=True)
    xc = x - mu
    var = jnp.mean(xc * xc, axis=-1, keepdims=True)
    return xc * lax.rsqrt(var + LN_EPS)


def _bucket_np(dist):
    dist = np.asarray(dist, np.int64)
    n = np.maximum(dist, 0)
    nf = np.maximum(n, 1).astype(np.float32)
    large = MAX_EXACT + (np.log(nf / np.float32(MAX_EXACT)) / np.float32(math.log(MAX_DIST / MAX_EXACT))
                         * np.float32(N_BUCKETS - MAX_EXACT)).astype(np.int32)
    large = np.minimum(large, N_BUCKETS - 1)
    b = np.where(n < MAX_EXACT, n, large)
    return np.where(dist < 0, MASKED_ID, b).astype(np.int32)


def _bucket_dyn(dist):
    n = jnp.maximum(dist, 0)
    nf = jnp.maximum(n, 1).astype(F32)
    large = MAX_EXACT + (jnp.log(nf / MAX_EXACT) / math.log(MAX_DIST / MAX_EXACT)
                         * (N_BUCKETS - MAX_EXACT)).astype(jnp.int32)
    large = jnp.minimum(large, N_BUCKETS - 1)
    return jnp.where(n < MAX_EXACT, n, large)


def _mod_kernel(c_ref, w_ref, b_ref, o_ref):
    a = _silu(c_ref[...])
    o_ref[...] = _dot(a, w_ref[...]) + b_ref[...]


def _adaln_mod(c, w_ada, b_ada):
    rows = c.shape[0]
    n3 = w_ada.shape[1]
    tn = D_MODEL
    return pl.pallas_call(
        _mod_kernel,
        grid=(n3 // tn,),
        in_specs=[pl.BlockSpec((rows, D_MODEL), lambda j: (0, 0)),
                  pl.BlockSpec((D_MODEL, tn), lambda j: (0, j)),
                  pl.BlockSpec((1, tn), lambda j: (0, j))],
        out_specs=pl.BlockSpec((rows, tn), lambda j: (0, j)),
        out_shape=jax.ShapeDtypeStruct((rows, n3), F32),
        compiler_params=_cparams("arbitrary"),
        name="adaln_mod",
    )(c, w_ada, b_ada.reshape(1, n3))


def _bias_kernel(rb_ref, ids_ref, o_ref, *, n_groups):
    def body(i, carry):
        r0 = pl.multiple_of(i * SUBLANE, SUBLANE)
        ids = ids_ref[pl.ds(r0, SUBLANE), :]
        for h in range(A_HEADS):
            acc = jnp.full((SUBLANE, LANE), NEG, F32)
            for b in range(N_BUCKETS):
                acc = jnp.where(ids == b, rb_ref[b, h], acc)
            o_ref[h, pl.ds(r0, SUBLANE), :] = acc
        return carry

    lax.fori_loop(0, n_groups, body, 0)


def _bias_tables(rel_bias, ids):
    rows = ids.shape[0]
    return pl.pallas_call(
        functools.partial(_bias_kernel, n_groups=rows // SUBLANE),
        in_specs=[pl.BlockSpec(memory_space=pltpu.SMEM),
                  pl.BlockSpec((rows, LANE), lambda: (0, 0))],
        out_specs=pl.BlockSpec((A_HEADS, rows, LANE), lambda: (0, 0, 0)),
        out_shape=jax.ShapeDtypeStruct((A_HEADS, rows, LANE), F32),
        name="bias_tables",
    )(rel_bias, jnp.asarray(ids))


def _pack_in_proj(w_in, b_in):
    offs = np.cumsum((0,) + IN_SPLITS)
    names = ("mq", "mk", "mv", "mo", "mz", "mi", "mf", "aq", "ck", "cv", "sk", "sv", "wk", "wv", "ga", "za")
    sl = {n: (int(offs[i]), int(offs[i + 1])) for i, n in enumerate(names)}

    def cols(a, name, lo=None, hi=None):
        s, e = sl[name]
        if lo is not None:
            s, e = s + lo, s + hi
        return a[..., s:e]

    def rows_of(a, kn, vn):
        return [cols(a, kn, 0, A_HD), cols(a, vn, 0, A_HD), cols(a, kn, A_HD, 2 * A_HD), cols(a, vn, A_HD, 2 * A_HD)]

    def pack(a):
        def zeros(n):
            return jnp.zeros(a.shape[:-1] + (n,), a.dtype)
        parts = [cols(a, n) for n in ("mq", "mk", "mv", "mo", "mz")]
        parts += [cols(a, "mi"), cols(a, "mf"), zeros(LANE - 2 * M_HEADS)]
        parts += [cols(a, "aq"), cols(a, "za"), cols(a, "ga"), zeros(LANE - 3 * A_HEADS)]
        parts += rows_of(a, "ck", "cv") + rows_of(a, "sk", "sv") + rows_of(a, "wk", "wv")
        return jnp.concatenate(parts, axis=-1)

    w = pack(w_in)
    b = pack(b_in.reshape(1, -1))
    wt = jnp.concatenate([cols(w_in, "sv"), cols(w_in, "wv")], axis=-1).T
    bt = jnp.concatenate([cols(b_in, "sv"), cols(b_in, "wv")], axis=-1).reshape(-1, 1)
    return w.astype(BF16), b, wt.astype(BF16), bt


def _proj_kernel(x_ref, sh_ref, sc_ref, w_ref, b_ref, *rest, with_t):
    if with_t:
        wt_ref, bt_ref, om_ref, oa_ref, oc_ref, os_ref, ow_ref, ost_ref, owt_ref = rest
    else:
        om_ref, oa_ref, oc_ref, os_ref, ow_ref = rest
    h = _ln_rows(x_ref[...]) * (1.0 + sc_ref[...]) + sh_ref[...]
    hb = h.astype(BF16)
    lo = 0
    for o_ref in (om_ref, oa_ref, oc_ref, os_ref, ow_ref):
        n = o_ref.shape[-1]
        o_ref[...] = _dot(hb, w_ref[:, lo:lo + n]) + b_ref[:, lo:lo + n]
        lo += n
    if with_t:
        t = _nt(wt_ref[...], hb) + bt_ref[...]
        ost_ref[...] = t[:A_KVW]
        owt_ref[...] = t[A_KVW:]


def _project(x, shift, scale, packed, rows_per_mod, tm, with_t):
    w, b, wt, bt = packed
    rows = x.shape[0]
    grid = (rows // tm,)
    if rows_per_mod:
        per = rows_per_mod // tm
        mod_spec = pl.BlockSpec((None, 1, D_MODEL), lambda i: (i // per, 0, 0))
    else:
        mod_spec = pl.BlockSpec((tm, D_MODEL), lambda i: (i, 0))
    in_specs = [pl.BlockSpec((tm, D_MODEL), lambda i: (i, 0)), mod_spec, mod_spec,
                pl.BlockSpec((D_MODEL, PW_TOTAL), lambda i: (0, 0)),
                pl.BlockSpec((1, PW_TOTAL), lambda i: (0, 0))]
    args = [x, shift, scale, w, b]
    widths = (PM_W, PA_W, ROW_W, ROW_W, ROW_W)
    out_specs = [pl.BlockSpec((tm, n), lambda i: (i, 0)) for n in widths]
    out_shape = [jax.ShapeDtypeStruct((rows, n), F32) for n in widths]
    if with_t:
        in_specs += [pl.BlockSpec((2 * A_KVW, D_MODEL), lambda i: (0, 0)),
                     pl.BlockSpec((2 * A_KVW, 1), lambda i: (0, 0))]
        args += [wt, bt]
        out_specs += [pl.BlockSpec((A_KVW, tm), lambda i: (0, i))] * 2
        out_shape += [jax.ShapeDtypeStruct((A_KVW, rows), F32)] * 2
    return pl.pallas_call(
        functools.partial(_proj_kernel, with_t=with_t),
        grid=grid, in_specs=in_specs, out_specs=out_specs, out_shape=out_shape,
        compiler_params=_cparams("arbitrary"),
        name="in_proj_t" if with_t else "in_proj",
    )(*args)


def _mlstm_head_out(h, o_pre, z_pre, g_row):
    return jax.nn.sigmoid(o_pre) * (_ln_rows(h) * g_row) * _silu(z_pre)


def _mlstm_prompt_kernel(q_ref, k_ref, v_ref, o_ref, z_ref, g_ref, ng_ref, mix_ref, c_ref, n_ref, m_ref):
    L = M_CHUNK

    @pl.when(pl.program_id(1) == 0)
    def _():
        c_ref[...] = jnp.zeros_like(c_ref)
        n_ref[...] = jnp.zeros_like(n_ref)
        m_ref[...] = jnp.zeros_like(m_ref)

    gates = g_ref[...]
    gates_t = gates.T
    row = lax.broadcasted_iota(jnp.int32, (L, L), 0)
    col = lax.broadcasted_iota(jnp.int32, (L, L), 1)
    tril = col <= row
    cum = _dot(tril.astype(F32), _log_sigmoid(gates), HIGHEST)
    cum_t = _dot(_log_sigmoid(gates_t), (row <= col).astype(F32), HIGHEST)
    for h in range(M_HEADS):
        hs = slice(h * M_HD, (h + 1) * M_HD)
        b_col = cum[:, M_HEADS + h:M_HEADS + h + 1]
        b_row = cum_t[M_HEADS + h:M_HEADS + h + 1, :]
        ig_col = gates[:, h:h + 1]
        ig_row = gates_t[h:h + 1, :]
        m_prev = m_ref[0, h:h + 1, 0:1]
        c_prev = c_ref[0, h]
        n_prev = n_ref[0, h:h + 1, :]
        d = jnp.where(tril, b_col - b_row + ig_row, NEG)
        inter = b_col + m_prev
        m_t = jnp.maximum(inter, jnp.max(d, axis=1, keepdims=True))
        w_inter = jnp.exp(inter - m_t)
        q = q_ref[:, hs]
        ks = k_ref[:, hs] * (M_HD ** -0.5)
        v = v_ref[:, hs]
        qb, kb, vb = q.astype(BF16), ks.astype(BF16), v.astype(BF16)
        qk = _nt(qb, kb) * jnp.exp(d - m_t)
        num = w_inter * _dot(qb, c_prev.astype(BF16)) + _dot(qk.astype(BF16), vb)
        den = w_inter * jnp.sum(q * n_prev, axis=1, keepdims=True) + jnp.sum(qk, axis=1, keepdims=True)
        hh = num / jnp.maximum(jnp.abs(den), jnp.exp(-m_t))
        m_new = m_t[L - 1:L, :]
        b_last = b_col[L - 1:L, :]
        w_c = jnp.exp(b_last + m_prev - m_new)
        w_s = jnp.exp(b_last - b_col + ig_col - m_new)
        kw = ks * w_s
        c_ref[0, h] = w_c * c_prev + _dot(kw.T.astype(BF16), vb)
        n_ref[0, h:h + 1, :] = w_c * n_prev + jnp.sum(kw, axis=0, keepdims=True)
        m_ref[0, h:h + 1, :] = jnp.broadcast_to(m_new, (1, M_HD))
        mix_ref[:, hs] = _mlstm_head_out(hh, o_ref[:, hs], z_ref[:, hs], ng_ref[:, hs])


def _mlstm_prompt(pm, norm_g, batch, seq):
    nc = seq // M_CHUNK
    rows = batch * seq

    def col_spec(j, width=M_WIDTH):
        return pl.BlockSpec((M_CHUNK, width), lambda b, c: (b * nc + c, j))

    return pl.pallas_call(
        _mlstm_prompt_kernel,
        grid=(batch, nc),
        in_specs=[col_spec(0), col_spec(1), col_spec(2), col_spec(3), col_spec(4),
                  pl.BlockSpec((M_CHUNK, LANE), lambda b, c: (b * nc + c, 5 * M_WIDTH // LANE)),
                  pl.BlockSpec((1, M_WIDTH), lambda b, c: (0, 0))],
        out_specs=[pl.BlockSpec((M_CHUNK, M_WIDTH), lambda b, c: (b * nc + c, 0)),
                   pl.BlockSpec((1, M_HEADS, M_HD, M_HD), lambda b, c: (b, 0, 0, 0)),
                   pl.BlockSpec((1, M_HEADS, M_HD), lambda b, c: (b, 0, 0)),
                   pl.BlockSpec((1, M_HEADS, M_HD), lambda b, c: (b, 0, 0))],
        out_shape=[jax.ShapeDtypeStruct((rows, M_WIDTH), F32),
                   jax.ShapeDtypeStruct((batch, M_HEADS, M_HD, M_HD), F32),
                   jax.ShapeDtypeStruct((batch, M_HEADS, M_HD), F32),
                   jax.ShapeDtypeStruct((batch, M_HEADS, M_HD), F32)],
        compiler_params=_cparams("arbitrary", "arbitrary"),
        name="mlstm_prompt",
    )(pm, pm, pm, pm, pm, pm, norm_g.reshape(1, M_WIDTH))


MS_BB = 16
MS_ROWS = PM_W // LANE


def _mlstm_sample_kernel(x_ref, c_ref, n_ref, m_ref, ng_ref, mix_ref, co_ref, no_ref, mo_ref):
    qk_cols = x_ref[:, 0:2 * M_HEADS, :].reshape(MS_BB * 2 * M_HEADS, M_HD).T
    for bi in range(MS_BB):
        xb = x_ref[bi]
        for h in range(M_HEADS):
            cb = bi * 2 * M_HEADS + h
            q_col = qk_cols[:, cb:cb + 1]
            k_col = qk_cols[:, cb + M_HEADS:cb + M_HEADS + 1] * (M_HD ** -0.5)
            q_row = xb[h:h + 1, :]
            k_row = xb[M_HEADS + h:M_HEADS + h + 1, :] * (M_HD ** -0.5)
            v_row = xb[2 * M_HEADS + h:2 * M_HEADS + h + 1, :]
            o_row = xb[3 * M_HEADS + h:3 * M_HEADS + h + 1, :]
            z_row = xb[4 * M_HEADS + h:4 * M_HEADS + h + 1, :]
            ig = xb[5 * M_HEADS:5 * M_HEADS + 1, h:h + 1]
            lf = _log_sigmoid(xb[5 * M_HEADS:5 * M_HEADS + 1, M_HEADS + h:M_HEADS + h + 1])
            c_prev = c_ref[bi, h]
            n_prev = n_ref[bi, h:h + 1, :]
            m_prev = m_ref[bi, h:h + 1, 0:1]
            inter = lf + m_prev
            m_t = jnp.maximum(inter, ig)
            w_inter = jnp.exp(inter - m_t)
            w_s = jnp.exp(ig - m_t)
            qk = jnp.sum(q_row * k_row, axis=1, keepdims=True) * w_s
            q_c = jnp.sum(q_col * c_prev, axis=0, keepdims=True)
            num = w_inter * q_c + qk * v_row
            den = w_inter * jnp.sum(q_row * n_prev, axis=1, keepdims=True) + qk
            hh = num / jnp.maximum(jnp.abs(den), jnp.exp(-m_t))
            co_ref[bi, h] = w_inter * c_prev + (w_s * k_col) * v_row
            no_ref[bi, h:h + 1, :] = w_inter * n_prev + w_s * k_row
            mo_ref[bi, h:h + 1, :] = jnp.broadcast_to(m_t, (1, M_HD))
            hs = slice(h * M_HD, (h + 1) * M_HD)
            mix_ref[bi, h:h + 1, :] = _mlstm_head_out(hh, o_row, z_row, ng_ref[:, hs])


def _mlstm_sample(pm, norm_g, c0, n0, m0):
    nb = pm.shape[0]
    x3 = pm.reshape(nb, MS_ROWS, LANE)
    m_b = jnp.broadcast_to(m0[:, :, None], (nb, M_HEADS, M_HD))
    bb = MS_BB
    state_specs = [pl.BlockSpec((bb, M_HEADS, M_HD, M_HD), lambda i: (i, 0, 0, 0)),
                   pl.BlockSpec((bb, M_HEADS, M_HD), lambda i: (i, 0, 0)),
                   pl.BlockSpec((bb, M_HEADS, M_HD), lambda i: (i, 0, 0))]
    return pl.pallas_call(
        _mlstm_sample_kernel,
        grid=(nb // bb,),
        in_specs=[pl.BlockSpec((bb, MS_ROWS, LANE), lambda i: (i, 0, 0))] + state_specs
                 + [pl.BlockSpec((1, M_WIDTH), lambda i: (0, 0))],
        out_specs=[pl.BlockSpec((bb, M_HEADS, M_HD), lambda i: (i, 0, 0))] + state_specs,
        out_shape=[jax.ShapeDtypeStruct((nb, M_HEADS, M_HD), F32),
                   jax.ShapeDtypeStruct((nb, M_HEADS, M_HD, M_HD), F32),
                   jax.ShapeDtypeStruct((nb, M_HEADS, M_HD), F32),
                   jax.ShapeDtypeStruct((nb, M_HEADS, M_HD), F32)],
        compiler_params=_cparams("arbitrary"),
        name="mlstm_sample",
    )(x3, c0, n0, m_b, norm_g.reshape(1, M_WIDTH))


KVROW_W = 2 * A_HD
HALF_W = 2 * KVROW_W


def _pack_compress(w1, w2):
    def block_diag(k, v):
        z = jnp.zeros_like(k)
        return jnp.concatenate([jnp.concatenate([k, z], axis=-1), jnp.concatenate([z, v], axis=-1)], axis=-2)

    wbd = block_diag(w1[0], w1[1])
    wcat = jnp.concatenate([wbd[:CMP_STRIDE], wbd[CMP_STRIDE:]], axis=-1)
    return wcat.astype(BF16), block_diag(w2[0], w2[1]).astype(BF16)


def _cmp_const_kernel(pe_ref, w_ref, b_ref, o_ref):
    for c in range(2):
        o_ref[c] = _dot(pe_ref[c], w_ref[c], HIGHEST) + b_ref[c]


def _compress_const(pe, w1, b1):
    k = CMP_LEN * A_HD
    pe8 = jnp.broadcast_to(pe.reshape(2, 1, k), (2, SUBLANE, k))
    out = pl.pallas_call(
        _cmp_const_kernel,
        out_shape=jax.ShapeDtypeStruct((2, SUBLANE, A_HD), F32),
        name="compress_const",
    )(pe8, w1.reshape(2, k, A_HD), b1.reshape(2, 1, A_HD))
    return jnp.concatenate([out[0, 0:1], out[1, 0:1]], axis=-1)


def _compress_halves(load_rows, wcat_ref, const_ref, w2_ref, acc_ref, n_half):
    for p in range(CMP_STRIDE):
        xp = jnp.concatenate([load_rows(p, kv) for kv in range(A_KV)], axis=0)
        d = _dot(xp.astype(BF16), wcat_ref[p])
        if p == 0:
            acc_ref[...] = d
        else:
            acc_ref[...] += d
    pre = acc_ref[:, :KVROW_W] + pltpu.roll(acc_ref[:, KVROW_W:], A_KV * n_half - 1, 0) + const_ref[...]
    return _dot(_gelu_tanh(pre).astype(BF16), w2_ref[...])


def _compress_prompt_kernel(x0_ref, x1_ref, wcat_ref, const_ref, w2_ref, kk_ref, kvt_ref, acc_ref, *, n_half):
    x_refs = (x0_ref, x1_ref)
    kc = _compress_halves(lambda p, kv: x_refs[kv][pl.ds(p, n_half, stride=CMP_STRIDE), :],
                          wcat_ref, const_ref, w2_ref, acc_ref, n_half)
    kct = kc.T
    for kv in range(A_KV):
        kk_ref[0, kv] = kc[kv * n_half:(kv + 1) * n_half, 0:A_HD]
        kvt_ref[0, kv] = kct[A_HD:, kv * n_half:(kv + 1) * n_half]


def _compress_prompt(rows, wcat, const_row, w2bd, batch, seq):
    n_half = seq // CMP_STRIDE
    return pl.pallas_call(
        functools.partial(_compress_prompt_kernel, n_half=n_half),
        grid=(batch,),
        in_specs=[pl.BlockSpec((seq, KVROW_W), lambda b: (b, 0)),
                  pl.BlockSpec((seq, KVROW_W), lambda b: (b, 1)),
                  pl.BlockSpec((CMP_STRIDE, KVROW_W, HALF_W), lambda b: (0, 0, 0)),
                  pl.BlockSpec((1, KVROW_W), lambda b: (0, 0)),
                  pl.BlockSpec((KVROW_W, KVROW_W), lambda b: (0, 0))],
        out_specs=[pl.BlockSpec((1, A_KV, n_half, A_HD), lambda b: (b, 0, 0, 0)),
                   pl.BlockSpec((1, A_KV, A_HD, n_half), lambda b: (b, 0, 0, 0))],
        out_shape=[jax.ShapeDtypeStruct((batch, A_KV, n_half, A_HD), F32),
                   jax.ShapeDtypeStruct((batch, A_KV, A_HD, n_half), F32)],
        scratch_shapes=[pltpu.VMEM((A_KV * n_half, HALF_W), F32)],
        compiler_params=_cparams("arbitrary"),
        name="compress_prompt",
    )(rows, rows, wcat, const_row, w2bd)


CMP_PAT = 16


def _static_ids(p_len):
    i = np.arange(TQ)[None, :]
    c = np.arange(CMP_PAT)[:, None]
    cmp_a = _bucket_np(i + (TQ - (CMP_LEN - 1)) - CMP_STRIDE * c)
    cmp_b = _bucket_np(i - CMP_STRIDE * c - (CMP_LEN - 1))
    r = np.arange(TQ)[:, None]
    slc_diag = _bucket_np(i - r)
    slc_sub = _bucket_np(TQ + i - r)
    slc_far = np.full((TQ, TQ), FAR_BUCKET, np.int32)
    rw = np.arange(WINDOW + TQ)[:, None]
    dw = WINDOW + i - rw
    win = np.where(dw > WINDOW, MASKED_ID, _bucket_np(dw))
    n_half = p_len // CMP_STRIDE
    n = np.arange(n_half)
    cs = _bucket_np(p_len - (CMP_STRIDE * n + CMP_LEN - 1))
    cs[n_half - 1] = MASKED_ID
    cs_rows = -(-n_half // LANE)
    cs_pad = np.full((cs_rows * LANE,), MASKED_ID, np.int32)
    cs_pad[:n_half] = cs
    ws = _bucket_np(WINDOW - np.arange(WINDOW))
    parts = [cmp_a, cmp_b, slc_diag, slc_sub, slc_far, win, cs_pad.reshape(cs_rows, LANE),
             ws.reshape(WINDOW // LANE, LANE)]
    offs = np.cumsum([0] + [p.shape[0] for p in parts])
    total = -(-int(offs[-1]) // SUBLANE) * SUBLANE
    ids = np.full((total, LANE), MASKED_ID, np.int32)
    ids[:offs[-1]] = np.concatenate(parts, axis=0)
    return ids, [int(o) for o in offs]


def _cover_np(n_cmp_rows, n_cmp, n_slc_rows, n_slc):
    cs = np.arange(n_cmp_rows)[:, None] * CMP_STRIDE
    ss = np.arange(n_slc_rows)[None, :] * SLC_BLOCK
    cov = (cs <= ss + SLC_BLOCK - 1) & (cs + CMP_LEN - 1 >= ss)
    cov &= (np.arange(n_cmp_rows)[:, None] < n_cmp) & (np.arange(n_slc_rows)[None, :] < n_slc)
    return cov.astype(np.float32)


def _softmax_keys_on_rows(s):
    m = jnp.max(s, axis=0, keepdims=True)
    m = jnp.where(m > 0.5 * NEG, m, 0.0)
    e = jnp.exp(s - m)
    tot = jnp.sum(e, axis=0, keepdims=True)
    return e / jnp.where(tot > 0.0, tot, 1.0)


def _softmax_keys_on_lanes(s, s_new=None):
    m = jnp.max(s, axis=1, keepdims=True)
    if s_new is not None:
        m = jnp.maximum(m, s_new)
    m = jnp.where(m > 0.5 * NEG, m, 0.0)
    e = jnp.exp(s - m)
    tot = jnp.sum(e, axis=1, keepdims=True)
    e_new = None
    if s_new is not None:
        e_new = jnp.exp(s_new - m)
        tot = tot + e_new
    return e, e_new, 1.0 / jnp.where(tot > 0.0, tot, 1.0)


def _cmp_attend_kernel(rb_ref, q_ref, kk_ref, kvt_ref, pt_ref, cov_ref, o_ref, sel_ref, bscr, *, nc, ns):
    k = pl.program_id(1)
    start = pl.multiple_of(jnp.maximum(SUBLANE * k - SUBLANE, 0), SUBLANE)
    variant = jnp.where(k == 0, 1, 0)
    row = lax.broadcasted_iota(jnp.int32, (nc, TQ), 0)
    t = k * TQ + lax.broadcasted_iota(jnp.int32, (ns, TQ), 1)
    blk = lax.broadcasted_iota(jnp.int32, (ns, TQ), 0)
    cur = t // SLC_BLOCK
    valid = blk * SLC_BLOCK <= t
    forced = (blk == 0) | (blk == cur) | (blk == cur - 1)
    for kv in range(A_KV):
        kk = kk_ref[0, kv].astype(BF16)
        kvt = kvt_ref[0, kv].astype(BF16)
        imp = jnp.zeros((nc, TQ), F32)
        for g in range(A_GROUP):
            h = kv * A_GROUP + g
            hs = slice(h * A_HD, (h + 1) * A_HD)
            bscr[...] = jnp.where(row < start, rb_ref[FAR_BUCKET, h], NEG)
            bscr[pl.ds(start, CMP_PAT), :] = pt_ref[variant, h]
            s = _nt(kk, q_ref[:, hs].astype(BF16)) * ATT_SCALE + bscr[...]
            p = _softmax_keys_on_rows(s)
            o_ref[0, hs, :] = _dot(kvt, p.astype(BF16))
            imp = imp + p
        score = _dot(cov_ref[...], imp, HIGHEST)
        sc = jnp.where(forced, jnp.inf, jnp.where(valid, score, -jnp.inf))
        cnt = jnp.zeros((ns, TQ), jnp.int32)
        for j in range(ns):
            r = sc[j:j + 1, :]
            before = (r > sc) | ((r == sc) & (blk > j))
            cnt = cnt + before.astype(jnp.int32)
        sel_ref[0, kv] = (cnt < N_SEL).astype(F32)


def _cmp_attend(rel_bias, pa, kk, kvt, pat, cov_t, batch, seq):
    nq = seq // TQ
    nc = seq // CMP_STRIDE
    ns = seq // SLC_BLOCK
    return pl.pallas_call(
        functools.partial(_cmp_attend_kernel, nc=nc, ns=ns),
        grid=(batch, nq),
        in_specs=[pl.BlockSpec(memory_space=pltpu.SMEM),
                  pl.BlockSpec((TQ, A_WIDTH), lambda b, k: (b * nq + k, 0)),
                  pl.BlockSpec((1, A_KV, nc, A_HD), lambda b, k: (b, 0, 0, 0)),
                  pl.BlockSpec((1, A_KV, A_HD, nc), lambda b, k: (b, 0, 0, 0)),
                  pl.BlockSpec((2, A_HEADS, CMP_PAT, TQ), lambda b, k: (0, 0, 0, 0)),
                  pl.BlockSpec((ns, nc), lambda b, k: (0, 0))],
        out_specs=[pl.BlockSpec((1, A_WIDTH, TQ), lambda b, k: (b, 0, k)),
                   pl.BlockSpec((1, A_KV, ns, TQ), lambda b, k: (b, 0, 0, k))],
        out_shape=[jax.ShapeDtypeStruct((batch, A_WIDTH, seq), F32),
                   jax.ShapeDtypeStruct((batch, A_KV, ns, seq), F32)],
        scratch_shapes=[pltpu.VMEM((nc, TQ), F32)],
        compiler_params=_cparams("arbitrary", "arbitrary"),
        name="cmp_attend",
    )(rel_bias, pa, kk, kvt, pat, cov_t)


def _slc_attend_kernel(q_ref, k_ref, vt_ref, sel_ref, tab_ref, o_ref):
    kv = pl.program_id(1)
    k = pl.program_id(2)
    qs = [q_ref[:, g * A_HD:(g + 1) * A_HD].astype(BF16) for g in range(A_GROUP)]
    upper = lax.broadcasted_iota(jnp.int32, (TQ, TQ), 0) < SLC_BLOCK

    def body(j, carry):
        j0 = pl.multiple_of(j * TQ, TQ)
        kj = k_ref[pl.ds(j0, TQ), 0:A_HD].astype(BF16)
        vt = vt_ref[:, pl.ds(j0, TQ)].astype(BF16)
        sel0 = sel_ref[0, 0, pl.ds(2 * j, 1), :]
        sel1 = sel_ref[0, 0, pl.ds(2 * j + 1, 1), :]
        keep = jnp.where(upper, sel0, sel1) > 0.5
        tile = jnp.minimum(k - j, 2)
        out = []
        for g in range(A_GROUP):
            m_run, l_run, acc = carry[g]
            s = _nt(kj, qs[g]) * ATT_SCALE + tab_ref[tile, kv * A_GROUP + g]
            s = jnp.where(keep, s, NEG)
            m_new = jnp.maximum(m_run, jnp.max(s, axis=0, keepdims=True))
            alpha = jnp.exp(m_run - m_new)
            p = jnp.exp(s - m_new)
            l_new = alpha * l_run + jnp.sum(p, axis=0, keepdims=True)
            acc_new = alpha * acc + _dot(vt, p.astype(BF16))
            out.append((m_new, l_new, acc_new))
        return tuple(out)

    init = tuple((jnp.full((1, TQ), NEG, F32), jnp.zeros((1, TQ), F32), jnp.zeros((A_HD, TQ), F32))
                 for _ in range(A_GROUP))
    res = lax.fori_loop(0, k + 1, body, init)
    for g in range(A_GROUP):
        _, l_run, acc = res[g]
        o_ref[0, g * A_HD:(g + 1) * A_HD, :] = acc / l_run


def _slc_attend(pa, rows, vt, sel, tab, batch, seq):
    nq = seq // TQ
    ns = seq // SLC_BLOCK
    gw = A_GROUP * A_HD
    return pl.pallas_call(
        _slc_attend_kernel,
        grid=(batch, A_KV, nq),
        in_specs=[pl.BlockSpec((TQ, gw), lambda b, kv, k: (b * nq + k, kv)),
                  pl.BlockSpec((seq, LANE), lambda b, kv, k: (b, kv)),
                  pl.BlockSpec((A_HD, seq), lambda b, kv, k: (kv, b)),
                  pl.BlockSpec((1, 1, ns, TQ), lambda b, kv, k: (b, kv, 0, k)),
                  pl.BlockSpec((3, A_HEADS, TQ, TQ), lambda b, kv, k: (0, 0, 0, 0))],
        out_specs=pl.BlockSpec((1, gw, TQ), lambda b, kv, k: (b, kv, k)),
        out_shape=jax.ShapeDtypeStruct((batch, A_WIDTH, seq), F32),
        compiler_params=_cparams("arbitrary", "arbitrary", "arbitrary"),
        name="slc_attend",
    )(pa, rows, vt, sel, tab)


WIN_SPAN = WINDOW + TQ


def _win_attend_kernel(q_ref, k_ref, vt_ref, bias_ref, o_ref):
    k = pl.program_id(1)
    r0 = pl.multiple_of(k * TQ, TQ)
    exists = lax.broadcasted_iota(jnp.int32, (WIN_SPAN, TQ), 0) + k * TQ >= WINDOW
    for kv in range(A_KV):
        kw = k_ref[0, pl.ds(r0, WIN_SPAN), kv * LANE:kv * LANE + A_HD].astype(BF16)
        vt = vt_ref[0, kv * A_HD:(kv + 1) * A_HD, pl.ds(r0, WIN_SPAN)].astype(BF16)
        for g in range(A_GROUP):
            h = kv * A_GROUP + g
            hs = slice(h * A_HD, (h + 1) * A_HD)
            s = _nt(kw, q_ref[:, hs].astype(BF16)) * ATT_SCALE + bias_ref[h]
            p = _softmax_keys_on_rows(jnp.where(exists, s, NEG))
            o_ref[0, hs, :] = _dot(vt, p.astype(BF16))


def _win_attend(pa, rows_pad, vt_pad, bias_w, batch, seq):
    nq = seq // TQ
    return pl.pallas_call(
        _win_attend_kernel,
        grid=(batch, nq),
        in_specs=[pl.BlockSpec((TQ, A_WIDTH), lambda b, k: (b * nq + k, 0)),
                  pl.BlockSpec((1, seq + WINDOW, ROW_W), lambda b, k: (b, 0, 0)),
                  pl.BlockSpec((1, A_KVW, seq + WINDOW), lambda b, k: (b, 0, 0)),
                  pl.BlockSpec((A_HEADS, WIN_SPAN, TQ), lambda b, k: (0, 0, 0))],
        out_specs=pl.BlockSpec((1, A_WIDTH, TQ), lambda b, k: (b, 0, k)),
        out_shape=jax.ShapeDtypeStruct((batch, A_WIDTH, seq), F32),
        compiler_params=_cparams("arbitrary", "arbitrary"),
        name="win_attend",
    )(pa, rows_pad, vt_pad, bias_w)


def _out_tail(x, mix_m, mix_a, gate, w_ref, b_ref, g_ref, beta_ref):
    y = (_dot(mix_m.astype(BF16), w_ref[:M_WIDTH]) + _dot(mix_a.astype(BF16), w_ref[M_WIDTH:]) + b_ref[...])
    return _ln_rows(DEEPNORM_ALPHA * x + gate * y) * g_ref[...] + beta_ref[...]


def _out_prompt_kernel(x_ref, mm_ref, oc_ref, os_ref, ow_ref, ga_ref, za_ref, gate_ref,
                       w_ref, b_ref, g_ref, beta_ref, y_ref):
    sig = jax.nn.sigmoid(ga_ref[...].T)
    parts = []
    for h in range(A_HEADS):
        hs = slice(h * A_HD, (h + 1) * A_HD)
        parts.append(sig[h:h + 1] * oc_ref[0, hs, :] + sig[A_HEADS + h:A_HEADS + h + 1] * os_ref[0, hs, :]
                     + sig[2 * A_HEADS + h:2 * A_HEADS + h + 1] * ow_ref[0, hs, :])
    ha = jnp.concatenate(parts, axis=0).T
    mix_a = ha * _silu(za_ref[...])
    y_ref[...] = _out_tail(x_ref[...], mm_ref[...], mix_a, gate_ref[...], w_ref, b_ref, g_ref, beta_ref)


def _out_prompt(x, mix_m, o_c, o_s, o_w, pa, gate, w_out, b_out, ln_g, ln_b, batch, seq):
    nq = seq // TQ
    rows = batch * seq
    branch = pl.BlockSpec((1, A_WIDTH, TQ), lambda i: (i // nq, 0, i % nq))
    vec = pl.BlockSpec((1, D_MODEL), lambda i: (0, 0))
    return pl.pallas_call(
        _out_prompt_kernel,
        grid=(rows // TQ,),
        in_specs=[pl.BlockSpec((TQ, D_MODEL), lambda i: (i, 0)),
                  pl.BlockSpec((TQ, M_WIDTH), lambda i: (i, 0)),
                  branch, branch, branch,
                  pl.BlockSpec((TQ, LANE), lambda i: (i, 2 * A_WIDTH // LANE)),
                  pl.BlockSpec((TQ, A_WIDTH), lambda i: (i, 1)),
                  pl.BlockSpec((None, 1, D_MODEL), lambda i: (i // nq, 0, 0)),
                  pl.BlockSpec((D_MODEL, D_MODEL), lambda i: (0, 0)),
                  vec, vec, vec],
        out_specs=pl.BlockSpec((TQ, D_MODEL), lambda i: (i, 0)),
        out_shape=jax.ShapeDtypeStruct((rows, D_MODEL), F32),
        compiler_params=_cparams("arbitrary"),
        name="out_prompt",
    )(x, mix_m, o_c, o_s, o_w, pa, pa, gate, w_out, b_out, ln_g, ln_b)


def _out_sample_kernel(x_ref, mm_ref, ha_ref, za_ref, gate_ref, w_ref, b_ref, g_ref, beta_ref, y_ref):
    mix_a = ha_ref[...] * _silu(za_ref[...])
    y_ref[...] = _out_tail(x_ref[...], mm_ref[...], mix_a, gate_ref[...], w_ref, b_ref, g_ref, beta_ref)


def _out_sample(x, mix_m, ha, pa, gate, w_out, b_out, ln_g, ln_b):
    rows = x.shape[0]
    vec = pl.BlockSpec((1, D_MODEL), lambda i: (0, 0))
    return pl.pallas_call(
        _out_sample_kernel,
        grid=(1,),
        in_specs=[pl.BlockSpec((rows, D_MODEL), lambda i: (0, 0)),
                  pl.BlockSpec((rows, M_WIDTH), lambda i: (0, 0)),
                  pl.BlockSpec((rows, A_WIDTH), lambda i: (0, 0)),
                  pl.BlockSpec((rows, A_WIDTH), lambda i: (0, 1)),
                  pl.BlockSpec((rows, D_MODEL), lambda i: (0, 0)),
                  pl.BlockSpec((D_MODEL, D_MODEL), lambda i: (0, 0)),
                  vec, vec, vec],
        out_specs=pl.BlockSpec((rows, D_MODEL), lambda i: (0, 0)),
        out_shape=jax.ShapeDtypeStruct((rows, D_MODEL), F32),
        compiler_params=_cparams("arbitrary"),
        name="out_sample",
    )(x, mix_m, ha, pa, gate, w_out, b_out, ln_g, ln_b)


def _nsa_prompt(rel_bias, pa, rc, rs, rw, svt, wvt, bias, offs, cmp_w, const_row, batch, seq):
    wcat, w2bd = cmp_w
    nc = seq // CMP_STRIDE
    ns = seq // SLC_BLOCK
    kk, kvt = _compress_prompt(rc, wcat, const_row, w2bd, batch, seq)
    pat = bias[:, offs[0]:offs[2]].reshape(A_HEADS, 2, CMP_PAT, TQ).transpose(1, 0, 2, 3)
    cov_t = jnp.asarray(_cover_np(nc, nc - 1, ns, ns).T)
    o_c, sel = _cmp_attend(rel_bias, pa, kk, kvt, pat, cov_t, batch, seq)
    tab = bias[:, offs[2]:offs[5]].reshape(A_HEADS, 3, TQ, TQ).transpose(1, 0, 2, 3)
    o_s = _slc_attend(pa, rs, svt, sel, tab, batch, seq)
    rows_pad = jnp.pad(rw.reshape(batch, seq, ROW_W), ((0, 0), (WINDOW, 0), (0, 0)))
    vt_pad = jnp.pad(wvt.reshape(A_KVW, batch, seq).transpose(1, 0, 2), ((0, 0), (0, 0), (WINDOW, 0)))
    o_w = _win_attend(pa, rows_pad, vt_pad, bias[:, offs[5]:offs[6]], batch, seq)
    return o_c, o_s, o_w, sel


HALVES_PER_PAGE = 8
SEL_LANES = 256


def _cmp_sample_kernel(pt_ref, *refs, n_pages, p_len):
    del pt_ref
    pages = refs[:n_pages]
    q_ref, wcat_ref, const_ref, w2_ref, bias_ref, cov_ref, o_ref, idx_ref, acc_ref = refs[n_pages:]
    n_half = n_pages * HALVES_PER_PAGE

    def load_rows(p, kv):
        rows = pl.ds(A_KV * p + kv, HALVES_PER_PAGE, stride=A_KV * CMP_STRIDE)
        return jnp.concatenate([pg[0, rows, :] for pg in pages], axis=0)

    kc = _compress_halves(load_rows, wcat_ref, const_ref, w2_ref, acc_ref, n_half).astype(BF16)
    kc0, kc1 = kc[:n_half], kc[n_half:]
    q8 = q_ref[0].astype(BF16)
    first = lax.broadcasted_iota(jnp.int32, (A_HEADS, n_half), 0) < A_GROUP
    s = jnp.where(first, _nt(q8, kc0[:, :A_HD]), _nt(q8, kc1[:, :A_HD])) * ATT_SCALE + bias_ref[...]
    e, _, inv = _softmax_keys_on_lanes(s)
    p = e * inv
    pb = p.astype(BF16)
    first_o = lax.broadcasted_iota(jnp.int32, (A_HEADS, A_HD), 0) < A_GROUP
    o_ref[0] = jnp.where(first_o, _dot(pb, kc0[:, A_HD:]), _dot(pb, kc1[:, A_HD:]))
    hrow = lax.broadcasted_iota(jnp.int32, (A_HEADS, n_half), 0)
    imp0 = jnp.sum(jnp.where(first, p, 0.0), axis=0, keepdims=True)
    imp1 = jnp.sum(jnp.where(first, 0.0, p), axis=0, keepdims=True)
    imp = jnp.where(hrow == 0, imp0, jnp.where(hrow == 1, imp1, 0.0))
    score = _dot(imp, cov_ref[...], HIGHEST)
    n_slc = p_len // SLC_BLOCK + 1
    cur = p_len // SLC_BLOCK
    lane = lax.broadcasted_iota(jnp.int32, (A_HEADS, SEL_LANES), 1)
    forced = (lane == 0) | (lane == cur) | (lane == cur - 1)
    valid = lane * SLC_BLOCK <= p_len
    sc = jnp.where(forced, jnp.inf, jnp.where(valid, score, -jnp.inf))
    lane_f = lane.astype(F32)
    avail = lane < n_slc
    out_lane = lax.broadcasted_iota(jnp.int32, (A_HEADS, LANE), 1)
    picks = jnp.zeros((A_HEADS, LANE), F32)
    for j in range(min(N_SEL, n_slc)):
        best = jnp.max(jnp.where(avail, sc, -jnp.inf), axis=1, keepdims=True)
        pick = jnp.min(jnp.where(avail & (sc == best), lane_f, float(SEL_LANES)), axis=1, keepdims=True)
        picks = jnp.where(out_lane == j, pick, picks)
        avail = avail & (lane_f != pick)
    idx_ref[0] = picks.astype(jnp.int32)


def _cmp_sample(page_table, cache_pages, q3, wcat, const_row, w2bd, bias_cs, cov, p_len):
    nb, n_pages = page_table.shape
    n_half = n_pages * HALVES_PER_PAGE

    def page_spec(j):
        return pl.BlockSpec((1, A_KV * PAGE_ROWS, KVROW_W), lambda b, pt: (pt[b * n_pages + j], 0, 0))

    const2 = lambda b, pt: (0, 0)
    grid_spec = pltpu.PrefetchScalarGridSpec(
        num_scalar_prefetch=1,
        grid=(nb,),
        in_specs=[page_spec(j) for j in range(n_pages)]
                 + [pl.BlockSpec((1, A_HEADS, A_HD), lambda b, pt: (b, 0, 0)),
                    pl.BlockSpec((CMP_STRIDE, KVROW_W, HALF_W), lambda b, pt: (0, 0, 0)),
                    pl.BlockSpec((1, KVROW_W), const2),
                    pl.BlockSpec((KVROW_W, KVROW_W), const2),
                    pl.BlockSpec((A_HEADS, n_half), const2),
                    pl.BlockSpec((n_half, SEL_LANES), const2)],
        out_specs=[pl.BlockSpec((1, A_HEADS, A_HD), lambda b, pt: (b, 0, 0)),
                   pl.BlockSpec((1, A_HEADS, LANE), lambda b, pt: (b, 0, 0))],
        scratch_shapes=[pltpu.VMEM((A_KV * n_half, HALF_W), F32)],
    )
    return pl.pallas_call(
        functools.partial(_cmp_sample_kernel, n_pages=n_pages, p_len=p_len),
        grid_spec=grid_spec,
        out_shape=[jax.ShapeDtypeStruct((nb, A_HEADS, A_HD), F32),
                   jax.ShapeDtypeStruct((nb, A_HEADS, LANE), jnp.int32)],
        compiler_params=_cparams("arbitrary"),
        name="cmp_sample",
    )(page_table.reshape(-1), *([cache_pages] * n_pages), q3, wcat, const_row, w2bd, bias_cs, cov)


PAGE_ROWS = 128
BLOCKS_PER_PAGE = PAGE_ROWS // SLC_BLOCK


def _slc_sample_kernel(idx_ref, pt_ref, *refs, p_len):
    del pt_ref
    n_blk = A_KV * N_SEL
    blocks = refs[:n_blk]
    q_ref, snew_ref, win_ref, wnew_ref, oc_ref, g_ref, rbt_ref, bw_ref, ha_ref, wbuf_ref = refs[n_blk:]
    b = pl.program_id(0)
    past_blocks = p_len // SLC_BLOCK
    n_keys = N_SEL * SLC_BLOCK
    qf = q_ref[0]
    q8 = qf.astype(BF16)
    first_o = lax.broadcasted_iota(jnp.int32, (A_HEADS, A_HD), 0) < A_GROUP
    lane = lax.broadcasted_iota(jnp.int32, (1, n_keys), 1)
    slot = lane // SLC_BLOCK
    within = lane % SLC_BLOCK
    bucket_row = lax.broadcasted_iota(jnp.int32, (N_BUCKETS, n_keys), 0)
    bias_new = rbt_ref[:, 0:1]

    def new_key_logit(row_ref, kv):
        k_new = row_ref[0, :, kv * LANE:kv * LANE + A_HD]
        v_new = row_ref[0, :, kv * LANE + A_HD:(kv + 1) * LANE]
        return jnp.sum(qf * k_new, axis=1, keepdims=True) * ATT_SCALE + bias_new, v_new

    o_s, o_w = [], []
    for kv in range(A_KV):
        base = jnp.zeros((1, n_keys), jnp.int32)
        has_new = False
        for j in range(N_SEL):
            blk = idx_ref[(b * A_KV + kv) * N_SEL + j]
            base = jnp.where(slot == j, blk * SLC_BLOCK, base)
            has_new = jnp.logical_or(has_new, blk == past_blocks)
        pos = base + within
        valid = pos < p_len
        onehot = (bucket_row == _bucket_dyn(p_len - pos)).astype(F32)
        bias = _dot(rbt_ref[...], onehot, HIGHEST)
        kcat = jnp.concatenate([blocks[kv * N_SEL + j][0, :, 0:A_HD] for j in range(N_SEL)], axis=0).astype(BF16)
        vcat = jnp.concatenate([blocks[kv * N_SEL + j][0, :, A_HD:LANE] for j in range(N_SEL)], axis=0).astype(BF16)
        s = jnp.where(valid, _nt(q8, kcat) * ATT_SCALE + bias, NEG)
        s_new, v_new = new_key_logit(snew_ref, kv)
        s_new = jnp.where(has_new, s_new, NEG)
        e, e_new, inv = _softmax_keys_on_lanes(s, s_new)
        o_s.append((_dot(e.astype(BF16), vcat) + e_new * v_new) * inv)
        kw = win_ref[0, :, kv * LANE:kv * LANE + A_HD].astype(BF16)
        vw = win_ref[0, :, kv * LANE + A_HD:(kv + 1) * LANE].astype(BF16)
        sw = _nt(q8, kw) * ATT_SCALE + bw_ref[...]
        sw_new, vw_new = new_key_logit(wnew_ref, kv)
        e, e_new, inv = _softmax_keys_on_lanes(sw, sw_new)
        o_w.append((_dot(e.astype(BF16), vw) + e_new * vw_new) * inv)
    g = jax.nn.sigmoid(g_ref[0])
    ha_ref[0] = (g[0] * oc_ref[0] + g[1] * jnp.where(first_o, o_s[0], o_s[1])
                 + g[2] * jnp.where(first_o, o_w[0], o_w[1]))
    n_buf = win_ref.shape[1]
    wbuf_ref[0, 0:n_buf - 1, :] = win_ref[0, 1:n_buf, :]
    wbuf_ref[0, n_buf - 1:n_buf, :] = wnew_ref[0]


def _slc_sample(idx, page_table, cache_blocks, q3, slc_new, win_cache, win_new, o_c, gates, rb_t, bias_ws, p_len):
    nb, n_pages = page_table.shape
    past_blocks = p_len // SLC_BLOCK
    n_buf = win_cache.shape[1]

    def block_spec(kv, j):
        def index_map(b, idx_ref, pt_ref):
            blk = jnp.minimum(idx_ref[(b * A_KV + kv) * N_SEL + j], past_blocks - 1)
            page = pt_ref[b * n_pages + blk // BLOCKS_PER_PAGE]
            return (page * BLOCKS_PER_PAGE + blk % BLOCKS_PER_PAGE, 0, kv)
        return pl.BlockSpec((1, SLC_BLOCK, LANE), index_map)

    per_seq3 = lambda b, i, p: (b, 0, 0)
    grid_spec = pltpu.PrefetchScalarGridSpec(
        num_scalar_prefetch=2,
        grid=(nb,),
        in_specs=[block_spec(kv, j) for kv in range(A_KV) for j in range(N_SEL)]
                 + [pl.BlockSpec((1, A_HEADS, A_HD), per_seq3),
                    pl.BlockSpec((1, 1, ROW_W), per_seq3),
                    pl.BlockSpec((1, n_buf, ROW_W), per_seq3),
                    pl.BlockSpec((1, 1, ROW_W), per_seq3),
                    pl.BlockSpec((1, A_HEADS, A_HD), per_seq3),
                    pl.BlockSpec((1, 3, A_HEADS, 1), lambda b, i, p: (b, 0, 0, 0)),
                    pl.BlockSpec((A_HEADS, N_BUCKETS), lambda b, i, p: (0, 0)),
                    pl.BlockSpec((A_HEADS, n_buf), lambda b, i, p: (0, 0))],
        out_specs=[pl.BlockSpec((1, A_HEADS, A_HD), per_seq3),
                   pl.BlockSpec((1, n_buf, ROW_W), per_seq3)],
    )
    return pl.pallas_call(
        functools.partial(_slc_sample_kernel, p_len=p_len),
        grid_spec=grid_spec,
        out_shape=[jax.ShapeDtypeStruct((nb, A_HEADS, A_HD), F32),
                   jax.ShapeDtypeStruct((nb, n_buf, ROW_W), F32)],
        compiler_params=_cparams("arbitrary"),
        name="slc_win_sample",
    )(idx.reshape(-1), page_table.reshape(-1), *([cache_blocks] * (A_KV * N_SEL)),
      q3, slc_new, win_cache, win_new, o_c, gates, rb_t, bias_ws)


def _nsa_sample(rel_bias, pa, rc_new, rs_new, rw_new, cache_cmp, cache_slc, win_cache, page_table,
                bias, offs, cmp_w, const_row, p_len):
    wcat, w2bd = cmp_w
    nb, n_pages = page_table.shape
    n_phys = cache_cmp.shape[0]
    n_half = p_len // CMP_STRIDE
    n_slc = p_len // SLC_BLOCK + 1
    n_buf = win_cache.shape[1]
    q3 = pa[:, :A_WIDTH].reshape(nb, A_HEADS, A_HD)
    gates = pa[:, 2 * A_WIDTH:2 * A_WIDTH + 3 * A_HEADS].reshape(nb, 3, A_HEADS, 1)
    bias_cs = bias[:, offs[6]:offs[7]].reshape(A_HEADS, -1)[:, :n_half]
    bias_ws = bias[:, offs[7]:offs[8]].reshape(A_HEADS, -1)[:, :n_buf]
    cov = jnp.asarray(_cover_np(n_half, n_half - 1, SEL_LANES, n_slc))
    o_c, picks = _cmp_sample(page_table, cache_cmp.reshape(n_phys, A_KV * PAGE_ROWS, KVROW_W), q3,
                             wcat, const_row, w2bd, bias_cs, cov, p_len)
    idx = picks[:, :A_KV, :N_SEL]
    ha, wbuf = _slc_sample(idx, page_table, cache_slc.reshape(n_phys * BLOCKS_PER_PAGE, SLC_BLOCK, ROW_W), q3,
                           rs_new.reshape(nb, 1, ROW_W), win_cache.reshape(nb, n_buf, ROW_W),
                           rw_new.reshape(nb, 1, ROW_W), o_c, gates, rel_bias.T, bias_ws, p_len)
    return ha.reshape(nb, A_WIDTH), idx, wbuf


def kernel(x_prompt, x_sample, cache_cmp_kv, cache_slc_kv, cache_win_kv, state_mlstm_C, state_mlstm_n, state_mlstm_m, page_table, c_prompt, c_sample, rel_bias, w_ada, b_ada, w_in, b_in, m_norm_g, cmp_pe, cmp_w1, cmp_b1, cmp_w2, w_out, b_out, ln_g, ln_b):
    B, T, _ = x_prompt.shape
    NB = x_sample.shape[0]
    n_pages = page_table.shape[1]
    p_len = n_pages * PAGE_ROWS
    depth = w_in.shape[0]
    assert depth == 1 and x_sample.shape[1] == 1 and cache_win_kv.shape[2] == WINDOW
    ids, offs = _static_ids(p_len)
    bias = _bias_tables(rel_bias, ids)
    x_p = x_prompt.reshape(B * T, D_MODEL)
    x_s = x_sample.reshape(NB, D_MODEL)
    l = 0
    n_mod = -(-(B + NB) // SUBLANE) * SUBLANE
    c_all = jnp.concatenate([c_prompt, c_sample, jnp.zeros((n_mod - B - NB, D_MODEL), F32)])
    shift, scale, gate = jnp.split(_adaln_mod(c_all, w_ada[l], b_ada[l]), 3, axis=-1)
    packed = _pack_in_proj(w_in[l], b_in[l])
    cmp_w = _pack_compress(cmp_w1[l], cmp_w2[l])
    const_row = _compress_const(cmp_pe[l], cmp_w1[l], cmp_b1[l])
    w_out_b = w_out[l].astype(BF16)
    vecs = (b_out[l].reshape(1, -1), ln_g[l].reshape(1, -1), ln_b[l].reshape(1, -1))
    pm, pa, rc, rs, rw, svt, wvt = _project(x_p, shift[:B, None], scale[:B, None], packed, T, 256, True)
    mix_m, c_p, n_p, m_p = _mlstm_prompt(pm, m_norm_g[l], B, T)
    o_c, o_s, o_w, _ = _nsa_prompt(rel_bias, pa, rc, rs, rw, svt, wvt, bias, offs, cmp_w, const_row, B, T)
    y_p = _out_prompt(x_p, mix_m, o_c, o_s, o_w, pa, gate[:B, None], w_out_b, *vecs, B, T)
    pm_s, pa_s, rc_s, rs_s, rw_s = _project(x_s, shift[B:B + NB], scale[B:B + NB], packed, 0, NB, False)
    mix_s, c_s, n_s, m_s = _mlstm_sample(pm_s, m_norm_g[l], state_mlstm_C[l], state_mlstm_n[l], state_mlstm_m[l])
    ha_s, _, wbuf_s = _nsa_sample(rel_bias, pa_s, rc_s, rs_s, rw_s, cache_cmp_kv[l], cache_slc_kv[l],
                                  cache_win_kv[l], page_table, bias, offs, cmp_w, const_row, p_len)
    y_s = _out_sample(x_s, mix_s.reshape(NB, M_WIDTH), ha_s, pa_s, gate[B:B + NB], w_out_b, *vecs)

    def rows6(a, n):
        return a.reshape(1, n, -1, A_KV, 2, A_HD)

    win_p = rw.reshape(B, T, ROW_W)[:, T - WINDOW:]
    return (y_p.reshape(B, T, D_MODEL), y_s.reshape(NB, 1, D_MODEL),
            rows6(rc, B), rows6(rc_s, NB), rows6(rs, B), rows6(rs_s, NB),
            rows6(win_p, B), rows6(wbuf_s, NB),
            c_p[None], c_s[None], n_p[None], n_s[None], m_p[None, :, :, 0], m_s[None, :, :, 0])
```

```python
import functools
import math

import numpy as np
import jax
import jax.numpy as jnp
from jax import lax
from jax.experimental import pallas as pl
from jax.experimental.pallas import tpu as pltpu

F32 = jnp.float32
BF16 = jnp.bfloat16
HIGHEST = lax.Precision.HIGHEST

D_MODEL = 1024
M_HEADS = 4
M_HD = 128
M_WIDTH = M_HEADS * M_HD
M_CHUNK = 128
A_HEADS = 8
A_HD = 64
A_KV = 2
A_GROUP = A_HEADS // A_KV
A_WIDTH = A_HEADS * A_HD
A_KVW = A_KV * A_HD
ROW_W = 2 * A_KVW
CMP_LEN = 32
CMP_STRIDE = 16
SLC_BLOCK = 64
N_SEL = 16
WINDOW = 512
N_BUCKETS = 32
MAX_EXACT = N_BUCKETS // 2
MAX_DIST = 128
FAR_BUCKET = N_BUCKETS - 1
LN_EPS = 1e-5
ATT_SCALE = A_HD ** -0.5
DEPTH = 1
DEEPNORM_ALPHA = (2.0 * DEPTH) ** 0.25
IN_SPLITS = (M_WIDTH,) * 5 + (M_HEADS, M_HEADS) + (A_WIDTH,) + (A_KVW,) * 6 + (3 * A_HEADS, A_WIDTH)

LANE = 128
SUBLANE = 8
TQ = 128
NEG = -1e30
MASKED_ID = N_BUCKETS
VMEM_LIMIT = 56 * 1024 * 1024

PM_W = 5 * M_WIDTH + LANE
PA_W = 2 * A_WIDTH + LANE
PW_TOTAL = PM_W + PA_W + 3 * ROW_W


def _cparams(*sem):
    return pltpu.CompilerParams(dimension_semantics=sem, vmem_limit_bytes=VMEM_LIMIT)


def _nt(a, b):
    return lax.dot_general(a, b, (((1,), (1,)), ((), ())), preferred_element_type=F32)


def _dot(a, b, precision=None):
    return jnp.dot(a, b, preferred_element_type=F32, precision=precision)


def _log_sigmoid(x):
    return jnp.minimum(x, 0.0) - jnp.log(1.0 + jnp.exp(-jnp.abs(x)))


def _silu(x):
    return x * jax.nn.sigmoid(x)


def _gelu_tanh(x):
    return 0.5 * x * (1.0 + jnp.tanh(math.sqrt(2.0 / math.pi) * (x + 0.044715 * (x * x * x))))


def _ln_rows(x):
    mu = jnp.mean(x, axis=-1, keepdims=True)
    xc = x - mu
    var = jnp.mean(xc * xc, axis=-1, keepdims=True)
    return xc * lax.rsqrt(var + LN_EPS)


def _bucket_np(dist):
    dist = np.asarray(dist, np.int64)
    n = np.maximum(dist, 0)
    nf = np.maximum(n, 1).astype(np.float32)
    large = MAX_EXACT + (np.log(nf / np.float32(MAX_EXACT)) / np.float32(math.log(MAX_DIST / MAX_EXACT))
                         * np.float32(N_BUCKETS - MAX_EXACT)).astype(np.int32)
    large = np.minimum(large, N_BUCKETS - 1)
    b = np.where(n < MAX_EXACT, n, large)
    return np.where(dist < 0, MASKED_ID, b).astype(np.int32)


def _bucket_dyn(dist):
    n = jnp.maximum(dist, 0)
    nf = jnp.maximum(n, 1).astype(F32)
    large = MAX_EXACT + (jnp.log(nf / MAX_EXACT) / math.log(MAX_DIST / MAX_EXACT)
                         * (N_BUCKETS - MAX_EXACT)).astype(jnp.int32)
    large = jnp.minimum(large, N_BUCKETS - 1)
    return jnp.where(n < MAX_EXACT, n, large)


def _mod_kernel(c_ref, w_ref, b_ref, o_ref):
    a = _silu(c_ref[...])
    o_ref[...] = _dot(a, w_ref[...]) + b_ref[...]


def _adaln_mod(c, w_ada, b_ada):
    rows = c.shape[0]
    n3 = w_ada.shape[1]
    tn = D_MODEL
    return pl.pallas_call(
        _mod_kernel,
        grid=(n3 // tn,),
        in_specs=[pl.BlockSpec((rows, D_MODEL), lambda j: (0, 0)),
                  pl.BlockSpec((D_MODEL, tn), lambda j: (0, j)),
                  pl.BlockSpec((1, tn), lambda j: (0, j))],
        out_specs=pl.BlockSpec((rows, tn), lambda j: (0, j)),
        out_shape=jax.ShapeDtypeStruct((rows, n3), F32),
        compiler_params=_cparams("arbitrary"),
        name="adaln_mod",
    )(c, w_ada, b_ada.reshape(1, n3))


def _bias_kernel(rb_ref, ids_ref, o_ref, *, n_groups):
    def body(i, carry):
        r0 = pl.multiple_of(i * SUBLANE, SUBLANE)
        ids = ids_ref[pl.ds(r0, SUBLANE), :]
        for h in range(A_HEADS):
            acc = jnp.full((SUBLANE, LANE), NEG, F32)
            for b in range(N_BUCKETS):
                acc = jnp.where(ids == b, rb_ref[b, h], acc)
            o_ref[h, pl.ds(r0, SUBLANE), :] = acc
        return carry

    lax.fori_loop(0, n_groups, body, 0)


def _bias_tables(rel_bias, ids):
    rows = ids.shape[0]
    return pl.pallas_call(
        functools.partial(_bias_kernel, n_groups=rows // SUBLANE),
        in_specs=[pl.BlockSpec(memory_space=pltpu.SMEM),
                  pl.BlockSpec((rows, LANE), lambda: (0, 0))],
        out_specs=pl.BlockSpec((A_HEADS, rows, LANE), lambda: (0, 0, 0)),
        out_shape=jax.ShapeDtypeStruct((A_HEADS, rows, LANE), F32),
        name="bias_tables",
    )(rel_bias, jnp.asarray(ids))


def _pack_in_proj(w_in, b_in):
    offs = np.cumsum((0,) + IN_SPLITS)
    names = ("mq", "mk", "mv", "mo", "mz", "mi", "mf", "aq", "ck", "cv", "sk", "sv", "wk", "wv", "ga", "za")
    sl = {n: (int(offs[i]), int(offs[i + 1])) for i, n in enumerate(names)}

    def cols(a, name, lo=None, hi=None):
        s, e = sl[name]
        if lo is not None:
            s, e = s + lo, s + hi
        return a[..., s:e]

    def rows_of(a, kn, vn):
        return [cols(a, kn, 0, A_HD), cols(a, vn, 0, A_HD), cols(a, kn, A_HD, 2 * A_HD), cols(a, vn, A_HD, 2 * A_HD)]

    def pack(a):
        def zeros(n):
            return jnp.zeros(a.shape[:-1] + (n,), a.dtype)
        parts = [cols(a, n) for n in ("mq", "mk", "mv", "mo", "mz")]
        parts += [cols(a, "mi"), cols(a, "mf"), zeros(LANE - 2 * M_HEADS)]
        parts += [cols(a, "aq"), cols(a, "za"), cols(a, "ga"), zeros(LANE - 3 * A_HEADS)]
        parts += rows_of(a, "ck", "cv") + rows_of(a, "sk", "sv") + rows_of(a, "wk", "wv")
        return jnp.concatenate(parts, axis=-1)

    w = pack(w_in)
    b = pack(b_in.reshape(1, -1))
    wt = w[:, PM_W + PA_W:].T
    bt = b[:, PM_W + PA_W:].reshape(-1, 1)
    return w.astype(BF16), b, wt.astype(BF16), bt


def _proj_kernel(x_ref, sh_ref, sc_ref, w_ref, b_ref, wt_ref, bt_ref,
                 om_ref, oa_ref, oc_ref, os_ref, ow_ref, oct_ref, ost_ref, owt_ref):
    h = _ln_rows(x_ref[...]) * (1.0 + sc_ref[...]) + sh_ref[...]
    hb = h.astype(BF16)
    lo = 0
    for o_ref in (om_ref, oa_ref, oc_ref, os_ref, ow_ref):
        n = o_ref.shape[-1]
        o_ref[...] = _dot(hb, w_ref[:, lo:lo + n]) + b_ref[:, lo:lo + n]
        lo += n
    t = _nt(wt_ref[...], hb) + bt_ref[...]
    for i, o_ref in enumerate((oct_ref, ost_ref, owt_ref)):
        o_ref[0] = t[i * ROW_W:(i + 1) * ROW_W]


def _project(x, shift, scale, packed, groups, tm):
    w, b, wt, bt = packed
    rows = x.shape[0]
    per = rows // groups // tm
    if shift.ndim == 3:
        mod_spec = pl.BlockSpec((None, 1, D_MODEL), lambda i: (i // per, 0, 0))
    else:
        mod_spec = pl.BlockSpec((tm, D_MODEL), lambda i: (i, 0))
    widths = (PM_W, PA_W, ROW_W, ROW_W, ROW_W)
    return pl.pallas_call(
        _proj_kernel,
        grid=(rows // tm,),
        in_specs=[pl.BlockSpec((tm, D_MODEL), lambda i: (i, 0)), mod_spec, mod_spec,
                  pl.BlockSpec((D_MODEL, PW_TOTAL), lambda i: (0, 0)),
                  pl.BlockSpec((1, PW_TOTAL), lambda i: (0, 0)),
                  pl.BlockSpec((3 * ROW_W, D_MODEL), lambda i: (0, 0)),
                  pl.BlockSpec((3 * ROW_W, 1), lambda i: (0, 0))],
        out_specs=[pl.BlockSpec((tm, n), lambda i: (i, 0)) for n in widths]
                  + [pl.BlockSpec((1, ROW_W, tm), lambda i: (i // per, 0, i % per))] * 3,
        out_shape=[jax.ShapeDtypeStruct((rows, n), F32) for n in widths]
                  + [jax.ShapeDtypeStruct((groups, ROW_W, rows // groups), F32)] * 3,
        compiler_params=_cparams("arbitrary"),
        name="in_proj",
    )(x, shift, scale, w, b, wt, bt)


def _mlstm_head_out(h, o_pre, z_pre, g_row):
    return jax.nn.sigmoid(o_pre) * (_ln_rows(h) * g_row) * _silu(z_pre)


def _mlstm_prompt_kernel(q_ref, k_ref, v_ref, o_ref, z_ref, g_ref, ng_ref, mix_ref, c_ref, n_ref, m_ref):
    L = M_CHUNK

    @pl.when(pl.program_id(1) == 0)
    def _():
        c_ref[...] = jnp.zeros_like(c_ref)
        n_ref[...] = jnp.zeros_like(n_ref)
        m_ref[...] = jnp.zeros_like(m_ref)

    gates = g_ref[...]
    gates_t = gates.T
    row = lax.broadcasted_iota(jnp.int32, (L, L), 0)
    col = lax.broadcasted_iota(jnp.int32, (L, L), 1)
    tril = col <= row
    cum = _dot(tril.astype(F32), _log_sigmoid(gates), HIGHEST)
    cum_t = _dot(_log_sigmoid(gates_t), (row <= col).astype(F32), HIGHEST)
    for h in range(M_HEADS):
        hs = slice(h * M_HD, (h + 1) * M_HD)
        b_col = cum[:, M_HEADS + h:M_HEADS + h + 1]
        b_row = cum_t[M_HEADS + h:M_HEADS + h + 1, :]
        ig_col = gates[:, h:h + 1]
        ig_row = gates_t[h:h + 1, :]
        m_prev = m_ref[0, h:h + 1, 0:1]
        c_prev = c_ref[0, h]
        n_prev = n_ref[0, h:h + 1, :]
        d = jnp.where(tril, b_col - b_row + ig_row, NEG)
        inter = b_col + m_prev
        m_t = jnp.maximum(inter, jnp.max(d, axis=1, keepdims=True))
        w_inter = jnp.exp(inter - m_t)
        q = q_ref[:, hs]
        ks = k_ref[:, hs] * (M_HD ** -0.5)
        v = v_ref[:, hs]
        qb, kb, vb = q.astype(BF16), ks.astype(BF16), v.astype(BF16)
        qk = _nt(qb, kb) * jnp.exp(d - m_t)
        num = w_inter * _dot(qb, c_prev.astype(BF16)) + _dot(qk.astype(BF16), vb)
        den = w_inter * jnp.sum(q * n_prev, axis=1, keepdims=True) + jnp.sum(qk, axis=1, keepdims=True)
        hh = num / jnp.maximum(jnp.abs(den), jnp.exp(-m_t))
        m_new = m_t[L - 1:L, :]
        b_last = b_col[L - 1:L, :]
        w_c = jnp.exp(b_last + m_prev - m_new)
        w_s = jnp.exp(b_last - b_col + ig_col - m_new)
        kw = ks * w_s
        c_ref[0, h] = w_c * c_prev + _dot(kw.T.astype(BF16), vb)
        n_ref[0, h:h + 1, :] = w_c * n_prev + jnp.sum(kw, axis=0, keepdims=True)
        m_ref[0, h:h + 1, :] = jnp.broadcast_to(m_new, (1, M_HD))
        mix_ref[:, hs] = _mlstm_head_out(hh, o_ref[:, hs], z_ref[:, hs], ng_ref[:, hs])


def _mlstm_prompt(pm, norm_g, batch, seq):
    nc = seq // M_CHUNK
    rows = batch * seq

    def col_spec(j, width=M_WIDTH):
        return pl.BlockSpec((M_CHUNK, width), lambda b, c: (b * nc + c, j))

    return pl.pallas_call(
        _mlstm_prompt_kernel,
        grid=(batch, nc),
        in_specs=[col_spec(0), col_spec(1), col_spec(2), col_spec(3), col_spec(4),
                  pl.BlockSpec((M_CHUNK, LANE), lambda b, c: (b * nc + c, 5 * M_WIDTH // LANE)),
                  pl.BlockSpec((1, M_WIDTH), lambda b, c: (0, 0))],
        out_specs=[pl.BlockSpec((M_CHUNK, M_WIDTH), lambda b, c: (b * nc + c, 0)),
                   pl.BlockSpec((1, M_HEADS, M_HD, M_HD), lambda b, c: (b, 0, 0, 0)),
                   pl.BlockSpec((1, M_HEADS, M_HD), lambda b, c: (b, 0, 0)),
                   pl.BlockSpec((1, M_HEADS, M_HD), lambda b, c: (b, 0, 0))],
        out_shape=[jax.ShapeDtypeStruct((rows, M_WIDTH), F32),
                   jax.ShapeDtypeStruct((batch, M_HEADS, M_HD, M_HD), F32),
                   jax.ShapeDtypeStruct((batch, M_HEADS, M_HD), F32),
                   jax.ShapeDtypeStruct((batch, M_HEADS, M_HD), F32)],
        compiler_params=_cparams("arbitrary", "arbitrary"),
        name="mlstm_prompt",
    )(pm, pm, pm, pm, pm, pm, norm_g.reshape(1, M_WIDTH))


MS_BB = 16
MS_ROWS = PM_W // LANE


def _mlstm_sample_kernel(x_ref, c_ref, n_ref, m_ref, ng_ref, mix_ref, co_ref, no_ref, mo_ref):
    qk_cols = x_ref[:, 0:2 * M_HEADS, :].reshape(MS_BB * 2 * M_HEADS, M_HD).T
    for bi in range(MS_BB):
        xb = x_ref[bi]
        for h in range(M_HEADS):
            cb = bi * 2 * M_HEADS + h
            q_col = qk_cols[:, cb:cb + 1]
            k_col = qk_cols[:, cb + M_HEADS:cb + M_HEADS + 1] * (M_HD ** -0.5)
            q_row = xb[h:h + 1, :]
            k_row = xb[M_HEADS + h:M_HEADS + h + 1, :] * (M_HD ** -0.5)
            v_row = xb[2 * M_HEADS + h:2 * M_HEADS + h + 1, :]
            o_row = xb[3 * M_HEADS + h:3 * M_HEADS + h + 1, :]
            z_row = xb[4 * M_HEADS + h:4 * M_HEADS + h + 1, :]
            ig = xb[5 * M_HEADS:5 * M_HEADS + 1, h:h + 1]
            lf = _log_sigmoid(xb[5 * M_HEADS:5 * M_HEADS + 1, M_HEADS + h:M_HEADS + h + 1])
            c_prev = c_ref[bi, h]
            n_prev = n_ref[bi, h:h + 1, :]
            m_prev = m_ref[bi, h:h + 1, 0:1]
            inter = lf + m_prev
            m_t = jnp.maximum(inter, ig)
            w_inter = jnp.exp(inter - m_t)
            w_s = jnp.exp(ig - m_t)
            qk = jnp.sum(q_row * k_row, axis=1, keepdims=True) * w_s
            q_c = jnp.sum(q_col * c_prev, axis=0, keepdims=True)
            num = w_inter * q_c + qk * v_row
            den = w_inter * jnp.sum(q_row * n_prev, axis=1, keepdims=True) + qk
            hh = num / jnp.maximum(jnp.abs(den), jnp.exp(-m_t))
            co_ref[bi, h] = w_inter * c_prev + (w_s * k_col) * v_row
            no_ref[bi, h:h + 1, :] = w_inter * n_prev + w_s * k_row
            mo_ref[bi, h:h + 1, :] = jnp.broadcast_to(m_t, (1, M_HD))
            hs = slice(h * M_HD, (h + 1) * M_HD)
            mix_ref[bi, h:h + 1, :] = _mlstm_head_out(hh, o_row, z_row, ng_ref[:, hs])


def _mlstm_sample(pm, norm_g, c0, n0, m0):
    nb = pm.shape[0]
    x3 = pm.reshape(nb, MS_ROWS, LANE)
    m_b = jnp.broadcast_to(m0[:, :, None], (nb, M_HEADS, M_HD))
    bb = MS_BB
    state_specs = [pl.BlockSpec((bb, M_HEADS, M_HD, M_HD), lambda i: (i, 0, 0, 0)),
                   pl.BlockSpec((bb, M_HEADS, M_HD), lambda i: (i, 0, 0)),
                   pl.BlockSpec((bb, M_HEADS, M_HD), lambda i: (i, 0, 0))]
    return pl.pallas_call(
        _mlstm_sample_kernel,
        grid=(nb // bb,),
        in_specs=[pl.BlockSpec((bb, MS_ROWS, LANE), lambda i: (i, 0, 0))] + state_specs
                 + [pl.BlockSpec((1, M_WIDTH), lambda i: (0, 0))],
        out_specs=[pl.BlockSpec((bb, M_HEADS, M_HD), lambda i: (i, 0, 0))] + state_specs,
        out_shape=[jax.ShapeDtypeStruct((nb, M_HEADS, M_HD), F32),
                   jax.ShapeDtypeStruct((nb, M_HEADS, M_HD, M_HD), F32),
                   jax.ShapeDtypeStruct((nb, M_HEADS, M_HD), F32),
                   jax.ShapeDtypeStruct((nb, M_HEADS, M_HD), F32)],
        compiler_params=_cparams("arbitrary"),
        name="mlstm_sample",
    )(x3, c0, n0, m_b, norm_g.reshape(1, M_WIDTH))


KVROW_W = 2 * A_HD
HALF_W = 2 * KVROW_W


def _pack_compress(w1, w2):
    def block_diag(k, v):
        z = jnp.zeros_like(k)
        return jnp.concatenate([jnp.concatenate([k, z], axis=-1), jnp.concatenate([z, v], axis=-1)], axis=-2)

    wbd = block_diag(w1[0], w1[1])
    wcat = jnp.concatenate([wbd[:CMP_STRIDE], wbd[CMP_STRIDE:]], axis=-1)
    return wcat.reshape(CMP_STRIDE * KVROW_W, HALF_W).astype(BF16), block_diag(w2[0], w2[1]).astype(BF16)


def _cmp_const_kernel(pe_ref, w_ref, b_ref, o_ref):
    for c in range(2):
        o_ref[c] = _dot(pe_ref[c], w_ref[c], HIGHEST) + b_ref[c]


def _compress_const(pe, w1, b1):
    k = CMP_LEN * A_HD
    pe8 = jnp.broadcast_to(pe.reshape(2, 1, k), (2, SUBLANE, k))
    out = pl.pallas_call(
        _cmp_const_kernel,
        out_shape=jax.ShapeDtypeStruct((2, SUBLANE, A_HD), F32),
        name="compress_const",
    )(pe8, w1.reshape(2, k, A_HD), b1.reshape(2, 1, A_HD))
    return jnp.concatenate([out[0, 0:1], out[1, 0:1]], axis=-1)


def _compress_halves(load_rows, wcat_ref, const_ref, w2_ref, n_half):
    halves = jnp.concatenate(
        [jnp.concatenate([load_rows(p, kv) for kv in range(A_KV)], axis=0).astype(BF16) for p in range(CMP_STRIDE)],
        axis=1)
    acc = _dot(halves, wcat_ref[...])
    pre = acc[:, :KVROW_W] + pltpu.roll(acc[:, KVROW_W:], A_KV * n_half - 1, 0) + const_ref[...]
    return _dot(_gelu_tanh(pre).astype(BF16), w2_ref[...])


def _compress_prompt_kernel(x0_ref, x1_ref, wcat_ref, const_ref, w2_ref, kk_ref, kvt_ref, *, n_half):
    x_refs = (x0_ref, x1_ref)
    kc = _compress_halves(lambda p, kv: x_refs[kv][pl.ds(p, n_half, stride=CMP_STRIDE), :],
                          wcat_ref, const_ref, w2_ref, n_half)
    kct = kc.T
    for kv in range(A_KV):
        kk_ref[0, kv] = kc[kv * n_half:(kv + 1) * n_half, 0:A_HD]
        kvt_ref[0, kv] = kct[A_HD:, kv * n_half:(kv + 1) * n_half]


def _compress_prompt(rows, wcat, const_row, w2bd, batch, seq):
    n_half = seq // CMP_STRIDE
    return pl.pallas_call(
        functools.partial(_compress_prompt_kernel, n_half=n_half),
        grid=(batch,),
        in_specs=[pl.BlockSpec((seq, KVROW_W), lambda b: (b, 0)),
                  pl.BlockSpec((seq, KVROW_W), lambda b: (b, 1)),
                  pl.BlockSpec((CMP_STRIDE * KVROW_W, HALF_W), lambda b: (0, 0)),
                  pl.BlockSpec((1, KVROW_W), lambda b: (0, 0)),
                  pl.BlockSpec((KVROW_W, KVROW_W), lambda b: (0, 0))],
        out_specs=[pl.BlockSpec((1, A_KV, n_half, A_HD), lambda b: (b, 0, 0, 0)),
                   pl.BlockSpec((1, A_KV, A_HD, n_half), lambda b: (b, 0, 0, 0))],
        out_shape=[jax.ShapeDtypeStruct((batch, A_KV, n_half, A_HD), F32),
                   jax.ShapeDtypeStruct((batch, A_KV, A_HD, n_half), F32)],
        compiler_params=_cparams("arbitrary"),
        name="compress_prompt",
    )(rows, rows, wcat, const_row, w2bd)


CMP_PAT = 16


def _static_ids(p_len):
    i = np.arange(TQ)[None, :]
    c = np.arange(CMP_PAT)[:, None]
    cmp_a = _bucket_np(i + (TQ - (CMP_LEN - 1)) - CMP_STRIDE * c)
    cmp_b = _bucket_np(i - CMP_STRIDE * c - (CMP_LEN - 1))
    r = np.arange(TQ)[:, None]
    slc_diag = _bucket_np(i - r)
    slc_sub = _bucket_np(TQ + i - r)
    slc_far = np.full((TQ, TQ), FAR_BUCKET, np.int32)
    rw = np.arange(WINDOW + TQ)[:, None]
    dw = WINDOW + i - rw
    win = np.where(dw > WINDOW, MASKED_ID, _bucket_np(dw))
    n_half = p_len // CMP_STRIDE
    n = np.arange(n_half)
    cs = _bucket_np(p_len - (CMP_STRIDE * n + CMP_LEN - 1))
    cs[n_half - 1] = MASKED_ID
    cs_rows = -(-n_half // LANE)
    cs_pad = np.full((cs_rows * LANE,), MASKED_ID, np.int32)
    cs_pad[:n_half] = cs
    ws = _bucket_np(WINDOW - np.arange(WINDOW))
    parts = [cmp_a, cmp_b, slc_diag, slc_sub, slc_far, win, cs_pad.reshape(cs_rows, LANE),
             ws.reshape(WINDOW // LANE, LANE)]
    offs = np.cumsum([0] + [p.shape[0] for p in parts])
    total = -(-int(offs[-1]) // SUBLANE) * SUBLANE
    ids = np.full((total, LANE), MASKED_ID, np.int32)
    ids[:offs[-1]] = np.concatenate(parts, axis=0)
    return ids, [int(o) for o in offs]


def _cover_np(n_cmp_rows, n_cmp, n_slc_rows, n_slc):
    cs = np.arange(n_cmp_rows)[:, None] * CMP_STRIDE
    ss = np.arange(n_slc_rows)[None, :] * SLC_BLOCK
    cov = (cs <= ss + SLC_BLOCK - 1) & (cs + CMP_LEN - 1 >= ss)
    cov &= (np.arange(n_cmp_rows)[:, None] < n_cmp) & (np.arange(n_slc_rows)[None, :] < n_slc)
    return cov.astype(np.float32)


def _softmax_keys_on_rows(s):
    m = jnp.max(s, axis=0, keepdims=True)
    m = jnp.where(m > 0.5 * NEG, m, 0.0)
    e = jnp.exp(s - m)
    tot = jnp.sum(e, axis=0, keepdims=True)
    return e / jnp.where(tot > 0.0, tot, 1.0)


def _softmax_keys_on_lanes(s, s_new=None):
    m = jnp.max(s, axis=1, keepdims=True)
    if s_new is not None:
        m = jnp.maximum(m, s_new)
    m = jnp.where(m > 0.5 * NEG, m, 0.0)
    e = jnp.exp(s - m)
    tot = jnp.sum(e, axis=1, keepdims=True)
    e_new = None
    if s_new is not None:
        e_new = jnp.exp(s_new - m)
        tot = tot + e_new
    return e, e_new, 1.0 / jnp.where(tot > 0.0, tot, 1.0)


def _cmp_attend_kernel(rb_ref, q_ref, kk_ref, kvt_ref, pt_ref, cov_ref, o_ref, sel_ref, bscr, *, nc, ns):
    k = pl.program_id(1)
    start = pl.multiple_of(jnp.maximum(SUBLANE * k - SUBLANE, 0), SUBLANE)
    variant = jnp.where(k == 0, 1, 0)
    row = lax.broadcasted_iota(jnp.int32, (nc, TQ), 0)
    t = k * TQ + lax.broadcasted_iota(jnp.int32, (ns, TQ), 1)
    blk = lax.broadcasted_iota(jnp.int32, (ns, TQ), 0)
    cur = t // SLC_BLOCK
    valid = blk * SLC_BLOCK <= t
    forced = (blk == 0) | (blk == cur) | (blk == cur - 1)
    for kv in range(A_KV):
        kk = kk_ref[0, kv].astype(BF16)
        kvt = kvt_ref[0, kv].astype(BF16)
        imp = jnp.zeros((nc, TQ), F32)
        for g in range(A_GROUP):
            h = kv * A_GROUP + g
            hs = slice(h * A_HD, (h + 1) * A_HD)
            bscr[...] = jnp.where(row < start, rb_ref[FAR_BUCKET, h], NEG)
            bscr[pl.ds(start, CMP_PAT), :] = pt_ref[variant, h]
            s = _nt(kk, q_ref[:, hs].astype(BF16)) * ATT_SCALE + bscr[...]
            p = _softmax_keys_on_rows(s)
            o_ref[0, hs, :] = _dot(kvt, p.astype(BF16))
            imp = imp + p
        score = _dot(cov_ref[...], imp, HIGHEST)
        sc = jnp.where(forced, jnp.inf, jnp.where(valid, score, -jnp.inf))
        cnt = jnp.zeros((ns, TQ), jnp.int32)
        for j in range(ns):
            r = sc[j:j + 1, :]
            before = (r > sc) | ((r == sc) & (blk > j))
            cnt = cnt + before.astype(jnp.int32)
        sel_ref[0, kv] = (cnt < N_SEL).astype(F32)


def _cmp_attend(rel_bias, pa, kk, kvt, pat, cov_t, batch, seq):
    nq = seq // TQ
    nc = seq // CMP_STRIDE
    ns = seq // SLC_BLOCK
    return pl.pallas_call(
        functools.partial(_cmp_attend_kernel, nc=nc, ns=ns),
        grid=(batch, nq),
        in_specs=[pl.BlockSpec(memory_space=pltpu.SMEM),
                  pl.BlockSpec((TQ, A_WIDTH), lambda b, k: (b * nq + k, 0)),
                  pl.BlockSpec((1, A_KV, nc, A_HD), lambda b, k: (b, 0, 0, 0)),
                  pl.BlockSpec((1, A_KV, A_HD, nc), lambda b, k: (b, 0, 0, 0)),
                  pl.BlockSpec((2, A_HEADS, CMP_PAT, TQ), lambda b, k: (0, 0, 0, 0)),
                  pl.BlockSpec((ns, nc), lambda b, k: (0, 0))],
        out_specs=[pl.BlockSpec((1, A_WIDTH, TQ), lambda b, k: (b, 0, k)),
                   pl.BlockSpec((1, A_KV, ns, TQ), lambda b, k: (b, 0, 0, k))],
        out_shape=[jax.ShapeDtypeStruct((batch, A_WIDTH, seq), F32),
                   jax.ShapeDtypeStruct((batch, A_KV, ns, seq), F32)],
        scratch_shapes=[pltpu.VMEM((nc, TQ), F32)],
        compiler_params=_cparams("arbitrary", "arbitrary"),
        name="cmp_attend",
    )(rel_bias, pa, kk, kvt, pat, cov_t)


SLC_CK = 2 * TQ
GROUP_LANES = A_GROUP * TQ
SLC_CLASSES = 4


def _stacked_queries(q_ref, kv):
    heads = [q_ref[:, (kv * A_GROUP + g) * A_HD:(kv * A_GROUP + g + 1) * A_HD] for g in range(A_GROUP)]
    return (jnp.concatenate(heads, axis=0) * ATT_SCALE).astype(BF16)


def _slc_attend_kernel(q_ref, k_ref, vt_ref, sel_ref, tab_ref, o_ref):
    k = pl.program_id(1)
    sub = SLC_CK // TQ
    q4 = [_stacked_queries(q_ref, kv) for kv in range(A_KV)]
    upper = lax.broadcasted_iota(jnp.int32, (TQ, GROUP_LANES), 0) < SLC_BLOCK

    def body(j, carry):
        j0 = pl.multiple_of(j * SLC_CK, SLC_CK)
        out = []
        for kv in range(A_KV):
            m_run, l_run, acc = carry[kv]
            kj = k_ref[pl.ds(j0, SLC_CK), kv * LANE:kv * LANE + A_HD].astype(BF16)
            vt = vt_ref[0, kv * LANE + A_HD:(kv + 1) * LANE, pl.ds(j0, SLC_CK)].astype(BF16)
            s_all = _nt(kj, q4[kv])
            parts = []
            for u in range(sub):
                jj = j * sub + u
                cls = jnp.where(jj > k, SLC_CLASSES - 1, jnp.minimum(k - jj, 2))
                sel0 = jnp.concatenate([sel_ref[0, kv, pl.ds(2 * jj, 1), :]] * A_GROUP, axis=1)
                sel1 = jnp.concatenate([sel_ref[0, kv, pl.ds(2 * jj + 1, 1), :]] * A_GROUP, axis=1)
                keep = jnp.where(upper, sel0, sel1) > 0.5
                parts.append(jnp.where(keep, s_all[u * TQ:(u + 1) * TQ] + tab_ref[cls, kv], NEG))
            s = jnp.concatenate(parts, axis=0)
            m_new = jnp.maximum(m_run, jnp.max(s, axis=0, keepdims=True))
            alpha = jnp.exp(m_run - m_new)
            p = jnp.exp(s - m_new)
            l_new = alpha * l_run + jnp.sum(p, axis=0, keepdims=True)
            out.append((m_new, l_new, alpha * acc + _dot(vt, p.astype(BF16))))
        return tuple(out)

    init = tuple((jnp.full((1, GROUP_LANES), NEG, F32), jnp.zeros((1, GROUP_LANES), F32),
                  jnp.zeros((A_HD, GROUP_LANES), F32)) for _ in range(A_KV))
    res = lax.fori_loop(0, (k + sub) // sub, body, init)
    for kv in range(A_KV):
        _, l_run, acc = res[kv]
        o = acc / l_run
        for g in range(A_GROUP):
            h = kv * A_GROUP + g
            o_ref[0, h * A_HD:(h + 1) * A_HD, :] = o[:, g * TQ:(g + 1) * TQ]


def _slc_attend(pa, rows, rows_t, sel, tab, batch, seq):
    nq = seq // TQ
    ns = seq // SLC_BLOCK
    return pl.pallas_call(
        _slc_attend_kernel,
        grid=(batch, nq),
        in_specs=[pl.BlockSpec((TQ, A_WIDTH), lambda b, k: (b * nq + k, 0)),
                  pl.BlockSpec((seq, ROW_W), lambda b, k: (b, 0)),
                  pl.BlockSpec((1, ROW_W, seq), lambda b, k: (b, 0, 0)),
                  pl.BlockSpec((1, A_KV, ns, TQ), lambda b, k: (b, 0, 0, k)),
                  pl.BlockSpec((SLC_CLASSES, A_KV, TQ, GROUP_LANES), lambda b, k: (0, 0, 0, 0))],
        out_specs=pl.BlockSpec((1, A_WIDTH, TQ), lambda b, k: (b, 0, k)),
        out_shape=jax.ShapeDtypeStruct((batch, A_WIDTH, seq), F32),
        compiler_params=_cparams("arbitrary", "arbitrary"),
        name="slc_attend",
    )(pa, rows, rows_t, sel, tab)


WIN_SPAN = WINDOW + TQ


def _win_attend_kernel(q_ref, k_ref, vt_ref, bias_ref, o_ref):
    k = pl.program_id(1)
    r0 = pl.multiple_of(k * TQ, TQ)
    exists = lax.broadcasted_iota(jnp.int32, (WIN_SPAN, GROUP_LANES), 0) + k * TQ >= WINDOW
    for kv in range(A_KV):
        kw = k_ref[0, pl.ds(r0, WIN_SPAN), kv * LANE:kv * LANE + A_HD].astype(BF16)
        vt = vt_ref[0, kv * LANE + A_HD:(kv + 1) * LANE, pl.ds(r0, WIN_SPAN)].astype(BF16)
        s = _nt(kw, _stacked_queries(q_ref, kv)) + bias_ref[kv]
        p = _softmax_keys_on_rows(jnp.where(exists, s, NEG))
        o = _dot(vt, p.astype(BF16))
        for g in range(A_GROUP):
            h = kv * A_GROUP + g
            o_ref[0, h * A_HD:(h + 1) * A_HD, :] = o[:, g * TQ:(g + 1) * TQ]


def _win_attend(pa, rows_pad, rows_t_pad, bias_w, batch, seq):
    nq = seq // TQ
    return pl.pallas_call(
        _win_attend_kernel,
        grid=(batch, nq),
        in_specs=[pl.BlockSpec((TQ, A_WIDTH), lambda b, k: (b * nq + k, 0)),
                  pl.BlockSpec((1, seq + WINDOW, ROW_W), lambda b, k: (b, 0, 0)),
                  pl.BlockSpec((1, ROW_W, seq + WINDOW), lambda b, k: (b, 0, 0)),
                  pl.BlockSpec((A_KV, WIN_SPAN, GROUP_LANES), lambda b, k: (0, 0, 0))],
        out_specs=pl.BlockSpec((1, A_WIDTH, TQ), lambda b, k: (b, 0, k)),
        out_shape=jax.ShapeDtypeStruct((batch, A_WIDTH, seq), F32),
        compiler_params=_cparams("arbitrary", "arbitrary"),
        name="win_attend",
    )(pa, rows_pad, rows_t_pad, bias_w)


def _out_tail(x, mix_m, mix_a, gate, w_ref, b_ref, g_ref, beta_ref):
    y = (_dot(mix_m.astype(BF16), w_ref[:M_WIDTH]) + _dot(mix_a.astype(BF16), w_ref[M_WIDTH:]) + b_ref[...])
    return _ln_rows(DEEPNORM_ALPHA * x + gate * y) * g_ref[...] + beta_ref[...]


def _out_prompt_kernel(x_ref, mm_ref, oc_ref, os_ref, ow_ref, ga_ref, za_ref, gate_ref,
                       w_ref, b_ref, g_ref, beta_ref, y_ref):
    sig = jax.nn.sigmoid(ga_ref[...].T)
    parts = []
    for h in range(A_HEADS):
        hs = slice(h * A_HD, (h + 1) * A_HD)
        parts.append(sig[h:h + 1] * oc_ref[0, hs, :] + sig[A_HEADS + h:A_HEADS + h + 1] * os_ref[0, hs, :]
                     + sig[2 * A_HEADS + h:2 * A_HEADS + h + 1] * ow_ref[0, hs, :])
    ha = jnp.concatenate(parts, axis=0).T
    mix_a = ha * _silu(za_ref[...])
    y_ref[...] = _out_tail(x_ref[...], mm_ref[...], mix_a, gate_ref[...], w_ref, b_ref, g_ref, beta_ref)


def _out_prompt(x, mix_m, o_c, o_s, o_w, pa, gate, w_out, b_out, ln_g, ln_b, batch, seq):
    nq = seq // TQ
    rows = batch * seq
    branch = pl.BlockSpec((1, A_WIDTH, TQ), lambda i: (i // nq, 0, i % nq))
    vec = pl.BlockSpec((1, D_MODEL), lambda i: (0, 0))
    return pl.pallas_call(
        _out_prompt_kernel,
        grid=(rows // TQ,),
        in_specs=[pl.BlockSpec((TQ, D_MODEL), lambda i: (i, 0)),
                  pl.BlockSpec((TQ, M_WIDTH), lambda i: (i, 0)),
                  branch, branch, branch,
                  pl.BlockSpec((TQ, LANE), lambda i: (i, 2 * A_WIDTH // LANE)),
                  pl.BlockSpec((TQ, A_WIDTH), lambda i: (i, 1)),
                  pl.BlockSpec((None, 1, D_MODEL), lambda i: (i // nq, 0, 0)),
                  pl.BlockSpec((D_MODEL, D_MODEL), lambda i: (0, 0)),
                  vec, vec, vec],
        out_specs=pl.BlockSpec((TQ, D_MODEL), lambda i: (i, 0)),
        out_shape=jax.ShapeDtypeStruct((rows, D_MODEL), F32),
        compiler_params=_cparams("arbitrary"),
        name="out_prompt",
    )(x, mix_m, o_c, o_s, o_w, pa, pa, gate, w_out, b_out, ln_g, ln_b)


def _out_sample_kernel(x_ref, mm_ref, ha_ref, za_ref, gate_ref, w_ref, b_ref, g_ref, beta_ref, y_ref):
    mix_a = ha_ref[...] * _silu(za_ref[...])
    y_ref[...] = _out_tail(x_ref[...], mm_ref[...], mix_a, gate_ref[...], w_ref, b_ref, g_ref, beta_ref)


def _out_sample(x, mix_m, ha, pa, gate, w_out, b_out, ln_g, ln_b):
    rows = x.shape[0]
    vec = pl.BlockSpec((1, D_MODEL), lambda i: (0, 0))
    return pl.pallas_call(
        _out_sample_kernel,
        grid=(1,),
        in_specs=[pl.BlockSpec((rows, D_MODEL), lambda i: (0, 0)),
                  pl.BlockSpec((rows, M_WIDTH), lambda i: (0, 0)),
                  pl.BlockSpec((rows, A_WIDTH), lambda i: (0, 0)),
                  pl.BlockSpec((rows, A_WIDTH), lambda i: (0, 1)),
                  pl.BlockSpec((rows, D_MODEL), lambda i: (0, 0)),
                  pl.BlockSpec((D_MODEL, D_MODEL), lambda i: (0, 0)),
                  vec, vec, vec],
        out_specs=pl.BlockSpec((rows, D_MODEL), lambda i: (0, 0)),
        out_shape=jax.ShapeDtypeStruct((rows, D_MODEL), F32),
        compiler_params=_cparams("arbitrary"),
        name="out_sample",
    )(x, mix_m, ha, pa, gate, w_out, b_out, ln_g, ln_b)


def _nsa_prompt(rel_bias, pa, rc, rs, rw, st, wt, bias, offs, cmp_w, const_row, batch, seq):
    wcat, w2bd = cmp_w
    nc = seq // CMP_STRIDE
    ns = seq // SLC_BLOCK
    kk, kvt = _compress_prompt(rc, wcat, const_row, w2bd, batch, seq)
    pat = bias[:, offs[0]:offs[2]].reshape(A_HEADS, 2, CMP_PAT, TQ).transpose(1, 0, 2, 3)
    cov_t = jnp.asarray(_cover_np(nc, nc - 1, ns, ns).T)
    o_c, sel = _cmp_attend(rel_bias, pa, kk, kvt, pat, cov_t, batch, seq)

    def group_lanes(tiles):
        rows = tiles.shape[1]
        return tiles.reshape(A_KV, A_GROUP, rows, TQ).transpose(0, 2, 1, 3).reshape(A_KV, rows, GROUP_LANES)

    tab = group_lanes(bias[:, offs[2]:offs[5]]).reshape(A_KV, 3, TQ, GROUP_LANES).transpose(1, 0, 2, 3)
    tab = jnp.concatenate([tab, jnp.full((1,) + tab.shape[1:], NEG, F32)], axis=0)
    o_s = _slc_attend(pa, rs, st, sel, tab, batch, seq)
    rows_pad = jnp.pad(rw.reshape(batch, seq, ROW_W), ((0, 0), (WINDOW, 0), (0, 0)))
    rows_t_pad = jnp.pad(wt, ((0, 0), (0, 0), (WINDOW, 0)))
    o_w = _win_attend(pa, rows_pad, rows_t_pad, group_lanes(bias[:, offs[5]:offs[6]]), batch, seq)
    return o_c, o_s, o_w, sel


HALVES_PER_PAGE = 8
SEL_LANES = 256


def _cmp_sample_kernel(pt_ref, *refs, n_pages, p_len):
    del pt_ref
    pages = refs[:n_pages]
    q_ref, wcat_ref, const_ref, w2_ref, bias_ref, cov_ref, o_ref, idx_ref, rows_ref = refs[n_pages:]
    n_half = n_pages * HALVES_PER_PAGE
    for j, pg in enumerate(pages):
        for kv in range(A_KV):
            rows_ref[kv, j * PAGE_ROWS:(j + 1) * PAGE_ROWS, :] = pg[0, kv].reshape(KVROW_W, PAGE_ROWS).T

    kc = _compress_halves(lambda p, kv: rows_ref[kv, pl.ds(p, n_half, stride=CMP_STRIDE), :],
                          wcat_ref, const_ref, w2_ref, n_half).astype(BF16)
    kc0, kc1 = kc[:n_half], kc[n_half:]
    q8 = q_ref[0].astype(BF16)
    first = lax.broadcasted_iota(jnp.int32, (A_HEADS, n_half), 0) < A_GROUP
    s = jnp.where(first, _nt(q8, kc0[:, :A_HD]), _nt(q8, kc1[:, :A_HD])) * ATT_SCALE + bias_ref[...]
    e, _, inv = _softmax_keys_on_lanes(s)
    p = e * inv
    pb = p.astype(BF16)
    first_o = lax.broadcasted_iota(jnp.int32, (A_HEADS, A_HD), 0) < A_GROUP
    o_ref[0] = jnp.where(first_o, _dot(pb, kc0[:, A_HD:]), _dot(pb, kc1[:, A_HD:]))
    hrow = lax.broadcasted_iota(jnp.int32, (A_HEADS, n_half), 0)
    imp0 = jnp.sum(jnp.where(first, p, 0.0), axis=0, keepdims=True)
    imp1 = jnp.sum(jnp.where(first, 0.0, p), axis=0, keepdims=True)
    imp = jnp.where(hrow == 0, imp0, jnp.where(hrow == 1, imp1, 0.0))
    score = _dot(imp, cov_ref[...], HIGHEST)
    n_slc = p_len // SLC_BLOCK + 1
    cur = p_len // SLC_BLOCK
    lane = lax.broadcasted_iota(jnp.int32, (A_HEADS, SEL_LANES), 1)
    forced = (lane == 0) | (lane == cur) | (lane == cur - 1)
    valid = lane * SLC_BLOCK <= p_len
    sc = jnp.where(forced, jnp.inf, jnp.where(valid, score, -jnp.inf))
    lane_f = lane.astype(F32)
    avail = lane < n_slc
    out_lane = lax.broadcasted_iota(jnp.int32, (A_HEADS, LANE), 1)
    picks = jnp.zeros((A_HEADS, LANE), F32)
    for j in range(min(N_SEL, n_slc)):
        best = jnp.max(jnp.where(avail, sc, -jnp.inf), axis=1, keepdims=True)
        pick = jnp.min(jnp.where(avail & (sc == best), lane_f, float(SEL_LANES)), axis=1, keepdims=True)
        picks = jnp.where(out_lane == j, pick, picks)
        avail = avail & (lane_f != pick)
    idx_ref[0] = picks.astype(jnp.int32)


def _cmp_sample(page_table, cache_pages, q3, wcat, const_row, w2bd, bias_cs, cov, p_len):
    nb, n_pages = page_table.shape
    n_half = n_pages * HALVES_PER_PAGE

    def page_spec(j):
        return pl.BlockSpec((1, A_KV, 2, A_HD, PAGE_ROWS), lambda b, pt: (pt[b * n_pages + j], 0, 0, 0, 0))

    const2 = lambda b, pt: (0, 0)
    grid_spec = pltpu.PrefetchScalarGridSpec(
        num_scalar_prefetch=1,
        grid=(nb,),
        in_specs=[page_spec(j) for j in range(n_pages)]
                 + [pl.BlockSpec((1, A_HEADS, A_HD), lambda b, pt: (b, 0, 0)),
                    pl.BlockSpec((CMP_STRIDE * KVROW_W, HALF_W), const2),
                    pl.BlockSpec((1, KVROW_W), const2),
                    pl.BlockSpec((KVROW_W, KVROW_W), const2),
                    pl.BlockSpec((A_HEADS, n_half), const2),
                    pl.BlockSpec((n_half, SEL_LANES), const2)],
        out_specs=[pl.BlockSpec((1, A_HEADS, A_HD), lambda b, pt: (b, 0, 0)),
                   pl.BlockSpec((1, A_HEADS, LANE), lambda b, pt: (b, 0, 0))],
        scratch_shapes=[pltpu.VMEM((A_KV, n_pages * PAGE_ROWS, KVROW_W), F32)],
    )
    return pl.pallas_call(
        functools.partial(_cmp_sample_kernel, n_pages=n_pages, p_len=p_len),
        grid_spec=grid_spec,
        out_shape=[jax.ShapeDtypeStruct((nb, A_HEADS, A_HD), F32),
                   jax.ShapeDtypeStruct((nb, A_HEADS, LANE), jnp.int32)],
        compiler_params=_cparams("arbitrary"),
        name="cmp_sample",
    )(page_table.reshape(-1), *([cache_pages] * n_pages), q3, wcat, const_row, w2bd, bias_cs, cov)


PAGE_ROWS = 128
BLOCKS_PER_PAGE = PAGE_ROWS // SLC_BLOCK


def _slc_sample_kernel(idx_ref, pt_ref, *refs, p_len):
    del pt_ref
    n_blk = A_KV * N_SEL
    blocks = refs[:n_blk]
    q_ref, snew_ref, win_ref, wnew_ref, wcol_ref, oc_ref, g_ref, rbt_ref, bw_ref, ha_ref, wbuf_ref = refs[n_blk:]
    b = pl.program_id(0)
    past_blocks = p_len // SLC_BLOCK
    n_keys = N_SEL * PAGE_ROWS
    n_buf = win_ref.shape[-1]
    qf = q_ref[0]
    q8 = qf.astype(BF16)
    first_o = lax.broadcasted_iota(jnp.int32, (A_HEADS, A_HD), 0) < A_GROUP
    lane = lax.broadcasted_iota(jnp.int32, (1, n_keys), 1)
    slot = lane // PAGE_ROWS
    in_page = lane % PAGE_ROWS
    bucket_row = lax.broadcasted_iota(jnp.int32, (N_BUCKETS, n_keys), 0)
    bias_new = rbt_ref[:, 0:1]

    def new_key_logit(row_ref, kv):
        k_new = row_ref[0, :, kv * LANE:kv * LANE + A_HD]
        v_new = row_ref[0, :, kv * LANE + A_HD:(kv + 1) * LANE]
        return jnp.sum(qf * k_new, axis=1, keepdims=True) * ATT_SCALE + bias_new, v_new

    o_s, o_w = [], []
    for kv in range(A_KV):
        blk_of = jnp.zeros((1, n_keys), jnp.int32)
        has_new = False
        for j in range(N_SEL):
            blk = idx_ref[(b * A_KV + kv) * N_SEL + j]
            blk_of = jnp.where(slot == j, blk, blk_of)
            has_new = jnp.logical_or(has_new, blk == past_blocks)
        pos = jnp.minimum(blk_of, past_blocks - 1) // BLOCKS_PER_PAGE * PAGE_ROWS + in_page
        valid = (pos // SLC_BLOCK == blk_of) & (pos < p_len)
        onehot = (bucket_row == _bucket_dyn(p_len - pos)).astype(F32)
        bias = _dot(rbt_ref[...], onehot, HIGHEST)
        kt = jnp.concatenate([blocks[kv * N_SEL + j][0, 0, 0] for j in range(N_SEL)], axis=1).astype(BF16)
        vt = jnp.concatenate([blocks[kv * N_SEL + j][0, 0, 1] for j in range(N_SEL)], axis=1).astype(BF16)
        s = jnp.where(valid, _dot(q8, kt) * ATT_SCALE + bias, NEG)
        s_new, v_new = new_key_logit(snew_ref, kv)
        s_new = jnp.where(has_new, s_new, NEG)
        e, e_new, inv = _softmax_keys_on_lanes(s, s_new)
        o_s.append((_nt(e.astype(BF16), vt) + e_new * v_new) * inv)
        sw = _dot(q8, win_ref[0, kv, 0].astype(BF16)) * ATT_SCALE + bw_ref[...]
        sw_new, vw_new = new_key_logit(wnew_ref, kv)
        e, e_new, inv = _softmax_keys_on_lanes(sw, sw_new)
        o_w.append((_nt(e.astype(BF16), win_ref[0, kv, 1].astype(BF16)) + e_new * vw_new) * inv)
    g = jax.nn.sigmoid(g_ref[0])
    ha_ref[0] = (g[0] * oc_ref[0] + g[1] * jnp.where(first_o, o_s[0], o_s[1])
                 + g[2] * jnp.where(first_o, o_w[0], o_w[1]))
    last = lax.broadcasted_iota(jnp.int32, (A_HD, n_buf), 1) == n_buf - 1
    for kv in range(A_KV):
        for c in range(2):
            r0 = (kv * 2 + c) * A_HD
            wbuf_ref[0, kv, c] = jnp.where(last, wcol_ref[0, r0:r0 + A_HD, :],
                                           pltpu.roll(win_ref[0, kv, c], n_buf - 1, 1))


def _slc_sample(idx, page_table, cache_t, q3, slc_new, win_t, win_new, win_new_col, o_c, gates, rb_t, bias_ws, p_len):
    nb, n_pages = page_table.shape
    past_blocks = p_len // SLC_BLOCK
    n_buf = win_t.shape[-1]

    def block_spec(kv, j):
        def index_map(b, idx_ref, pt_ref):
            blk = jnp.minimum(idx_ref[(b * A_KV + kv) * N_SEL + j], past_blocks - 1)
            return (pt_ref[b * n_pages + blk // BLOCKS_PER_PAGE], kv, 0, 0, 0)
        return pl.BlockSpec((1, 1, 2, A_HD, PAGE_ROWS), index_map)

    per_seq3 = lambda b, i, p: (b, 0, 0)
    win_spec = pl.BlockSpec((1, A_KV, 2, A_HD, n_buf), lambda b, i, p: (b, 0, 0, 0, 0))
    grid_spec = pltpu.PrefetchScalarGridSpec(
        num_scalar_prefetch=2,
        grid=(nb,),
        in_specs=[block_spec(kv, j) for kv in range(A_KV) for j in range(N_SEL)]
                 + [pl.BlockSpec((1, A_HEADS, A_HD), per_seq3),
                    pl.BlockSpec((1, 1, ROW_W), per_seq3),
                    win_spec,
                    pl.BlockSpec((1, 1, ROW_W), per_seq3),
                    pl.BlockSpec((1, ROW_W, 1), per_seq3),
                    pl.BlockSpec((1, A_HEADS, A_HD), per_seq3),
                    pl.BlockSpec((1, 3, A_HEADS, 1), lambda b, i, p: (b, 0, 0, 0)),
                    pl.BlockSpec((A_HEADS, N_BUCKETS), lambda b, i, p: (0, 0)),
                    pl.BlockSpec((A_HEADS, n_buf), lambda b, i, p: (0, 0))],
        out_specs=[pl.BlockSpec((1, A_HEADS, A_HD), per_seq3), win_spec],
    )
    return pl.pallas_call(
        functools.partial(_slc_sample_kernel, p_len=p_len),
        grid_spec=grid_spec,
        out_shape=[jax.ShapeDtypeStruct((nb, A_HEADS, A_HD), F32),
                   jax.ShapeDtypeStruct(win_t.shape, F32)],
        compiler_params=_cparams("arbitrary"),
        name="slc_win_sample",
    )(idx.reshape(-1), page_table.reshape(-1), *([cache_t] * (A_KV * N_SEL)),
      q3, slc_new, win_t, win_new, win_new_col, o_c, gates, rb_t, bias_ws)


def _rows_last(a):
    n = a.ndim
    return a.transpose(*range(n - 4), n - 3, n - 2, n - 1, n - 4)


def _rows_first(a):
    n = a.ndim
    return a.transpose(*range(n - 4), n - 1, n - 4, n - 3, n - 2)


def _nsa_sample(rel_bias, pa, rs_new, rw_new, cache_cmp, cache_slc, win_cache, page_table,
                bias, offs, cmp_w, const_row, p_len):
    wcat, w2bd = cmp_w
    nb, n_pages = page_table.shape
    n_half = p_len // CMP_STRIDE
    n_slc = p_len // SLC_BLOCK + 1
    n_buf = win_cache.shape[1]
    q3 = pa[:, :A_WIDTH].reshape(nb, A_HEADS, A_HD)
    gates = pa[:, 2 * A_WIDTH:2 * A_WIDTH + 3 * A_HEADS].reshape(nb, 3, A_HEADS, 1)
    bias_cs = bias[:, offs[6]:offs[7]].reshape(A_HEADS, -1)[:, :n_half]
    bias_ws = bias[:, offs[7]:offs[8]].reshape(A_HEADS, -1)[:, :n_buf]
    cov = jnp.asarray(_cover_np(n_half, n_half - 1, SEL_LANES, n_slc))
    o_c, picks = _cmp_sample(page_table, _rows_last(cache_cmp), q3, wcat, const_row, w2bd, bias_cs, cov, p_len)
    idx = picks[:, :A_KV, :N_SEL]
    ha, wbuf = _slc_sample(idx, page_table, _rows_last(cache_slc), q3, rs_new.reshape(nb, 1, ROW_W),
                           _rows_last(win_cache), rw_new.reshape(nb, 1, ROW_W), rw_new.reshape(nb, ROW_W, 1),
                           o_c, gates, rel_bias.T, bias_ws, p_len)
    return ha.reshape(nb, A_WIDTH), idx, wbuf


def kernel(x_prompt, x_sample, cache_cmp_kv, cache_slc_kv, cache_win_kv, state_mlstm_C, state_mlstm_n, state_mlstm_m, page_table, c_prompt, c_sample, rel_bias, w_ada, b_ada, w_in, b_in, m_norm_g, cmp_pe, cmp_w1, cmp_b1, cmp_w2, w_out, b_out, ln_g, ln_b):
    B, T, _ = x_prompt.shape
    NB = x_sample.shape[0]
    n_pages = page_table.shape[1]
    p_len = n_pages * PAGE_ROWS
    depth = w_in.shape[0]
    assert depth == 1 and x_sample.shape[1] == 1 and cache_win_kv.shape[2] == WINDOW
    ids, offs = _static_ids(p_len)
    bias = _bias_tables(rel_bias, ids)
    x_p = x_prompt.reshape(B * T, D_MODEL)
    x_s = x_sample.reshape(NB, D_MODEL)
    l = 0
    n_mod = -(-(B + NB) // SUBLANE) * SUBLANE
    c_all = jnp.concatenate([c_prompt, c_sample, jnp.zeros((n_mod - B - NB, D_MODEL), F32)])
    shift, scale, gate = jnp.split(_adaln_mod(c_all, w_ada[l], b_ada[l]), 3, axis=-1)
    packed = _pack_in_proj(w_in[l], b_in[l])
    cmp_w = _pack_compress(cmp_w1[l], cmp_w2[l])
    const_row = _compress_const(cmp_pe[l], cmp_w1[l], cmp_b1[l])
    w_out_b = w_out[l].astype(BF16)
    vecs = (b_out[l].reshape(1, -1), ln_g[l].reshape(1, -1), ln_b[l].reshape(1, -1))
    pm, pa, rc, rs, rw, ct, st, wt = _project(x_p, shift[:B, None], scale[:B, None], packed, B, 256)
    mix_m, c_p, n_p, m_p = _mlstm_prompt(pm, m_norm_g[l], B, T)
    o_c, o_s, o_w, _ = _nsa_prompt(rel_bias, pa, rc, rs, rw, st, wt, bias, offs, cmp_w, const_row, B, T)
    y_p = _out_prompt(x_p, mix_m, o_c, o_s, o_w, pa, gate[:B, None], w_out_b, *vecs, B, T)
    pm_s, pa_s, _, rs_s, rw_s, ct_s, st_s, wt_s = _project(x_s, shift[B:B + NB], scale[B:B + NB], packed, 1, NB)
    mix_s, c_s, n_s, m_s = _mlstm_sample(pm_s, m_norm_g[l], state_mlstm_C[l], state_mlstm_n[l], state_mlstm_m[l])
    ha_s, _, wbuf_s = _nsa_sample(rel_bias, pa_s, rs_s, rw_s, cache_cmp_kv[l], cache_slc_kv[l],
                                  cache_win_kv[l], page_table, bias, offs, cmp_w, const_row, p_len)
    y_s = _out_sample(x_s, mix_s.reshape(NB, M_WIDTH), ha_s, pa_s, gate[B:B + NB], w_out_b, *vecs)

    def kv_prompt(a):
        return _rows_first(a.reshape(1, B, A_KV, 2, A_HD, a.shape[-1]))

    def kv_sample(a):
        return a.reshape(1, 1, A_KV, 2, A_HD, NB).transpose(0, 5, 1, 2, 3, 4)

    return (y_p.reshape(B, T, D_MODEL), y_s.reshape(NB, 1, D_MODEL),
            kv_prompt(ct), kv_sample(ct_s), kv_prompt(st), kv_sample(st_s),
            kv_prompt(wt[:, :, T - WINDOW:]), _rows_first(wbuf_s)[None],
            c_p[None], c_s[None], n_p[None], n_s[None], m_p[None, :, :, 0], m_s[None, :, :, 0])
```

```python
import functools
import math

import numpy as np
import jax
import jax.numpy as jnp
from jax import lax
from jax.experimental import pallas as pl
from jax.experimental.pallas import tpu as pltpu

F32 = jnp.float32
BF16 = jnp.bfloat16
HIGHEST = lax.Precision.HIGHEST

D_MODEL = 1024
M_HEADS = 4
M_HD = 128
M_WIDTH = M_HEADS * M_HD
M_CHUNK = 128
A_HEADS = 8
A_HD = 64
A_KV = 2
A_GROUP = A_HEADS // A_KV
A_WIDTH = A_HEADS * A_HD
A_KVW = A_KV * A_HD
ROW_W = 2 * A_KVW
CMP_LEN = 32
CMP_STRIDE = 16
SLC_BLOCK = 64
N_SEL = 16
WINDOW = 512
N_BUCKETS = 32
MAX_EXACT = N_BUCKETS // 2
MAX_DIST = 128
FAR_BUCKET = N_BUCKETS - 1
LN_EPS = 1e-5
ATT_SCALE = A_HD ** -0.5
DEPTH = 1
DEEPNORM_ALPHA = (2.0 * DEPTH) ** 0.25
IN_SPLITS = (M_WIDTH,) * 5 + (M_HEADS, M_HEADS) + (A_WIDTH,) + (A_KVW,) * 6 + (3 * A_HEADS, A_WIDTH)

LANE = 128
SUBLANE = 8
TQ = 128
NEG = -1e30
LOG2E = math.log2(math.e)
MASKED_ID = N_BUCKETS
VMEM_LIMIT = 56 * 1024 * 1024

PM_W = 5 * M_WIDTH + LANE
PA_W = 2 * A_WIDTH + LANE
PW_TOTAL = PM_W + PA_W + 3 * ROW_W


def _cparams(*sem):
    return pltpu.CompilerParams(dimension_semantics=sem, vmem_limit_bytes=VMEM_LIMIT)


def _nt(a, b):
    return lax.dot_general(a, b, (((1,), (1,)), ((), ())), preferred_element_type=F32)


def _dot(a, b, precision=None):
    return jnp.dot(a, b, preferred_element_type=F32, precision=precision)


def _log_sigmoid(x):
    return jnp.minimum(x, 0.0) - jnp.log(1.0 + jnp.exp(-jnp.abs(x)))


def _silu(x):
    return x * jax.nn.sigmoid(x)


def _gelu_tanh(x):
    return 0.5 * x * (1.0 + jnp.tanh(math.sqrt(2.0 / math.pi) * (x + 0.044715 * (x * x * x))))


def _ln_rows(x):
    mu = jnp.mean(x, axis=-1, keepdims=True)
    xc = x - mu
    var = jnp.mean(xc * xc, axis=-1, keepdims=True)
    return xc * lax.rsqrt(var + LN_EPS)


def _bucket_np(dist):
    dist = np.asarray(dist, np.int64)
    n = np.maximum(dist, 0)
    nf = np.maximum(n, 1).astype(np.float32)
    large = MAX_EXACT + (np.log(nf / np.float32(MAX_EXACT)) / np.float32(math.log(MAX_DIST / MAX_EXACT))
                         * np.float32(N_BUCKETS - MAX_EXACT)).astype(np.int32)
    large = np.minimum(large, N_BUCKETS - 1)
    b = np.where(n < MAX_EXACT, n, large)
    return np.where(dist < 0, MASKED_ID, b).astype(np.int32)


def _bucket_dyn(dist):
    n = jnp.maximum(dist, 0)
    nf = jnp.maximum(n, 1).astype(F32)
    large = MAX_EXACT + (jnp.log(nf / MAX_EXACT) / math.log(MAX_DIST / MAX_EXACT)
                         * (N_BUCKETS - MAX_EXACT)).astype(jnp.int32)
    large = jnp.minimum(large, N_BUCKETS - 1)
    return jnp.where(n < MAX_EXACT, n, large)


def _mod_kernel(c_ref, w_ref, b_ref, o_ref):
    a = _silu(c_ref[...])
    o_ref[...] = _dot(a, w_ref[...]) + b_ref[...]


def _adaln_mod(c, w_ada, b_ada):
    rows = c.shape[0]
    n3 = w_ada.shape[1]
    tn = D_MODEL
    return pl.pallas_call(
        _mod_kernel,
        grid=(n3 // tn,),
        in_specs=[pl.BlockSpec((rows, D_MODEL), lambda j: (0, 0)),
                  pl.BlockSpec((D_MODEL, tn), lambda j: (0, j)),
                  pl.BlockSpec((1, tn), lambda j: (0, j))],
        out_specs=pl.BlockSpec((rows, tn), lambda j: (0, j)),
        out_shape=jax.ShapeDtypeStruct((rows, n3), F32),
        compiler_params=_cparams("arbitrary"),
        name="adaln_mod",
    )(c, w_ada, b_ada.reshape(1, n3))


def _bias_kernel(rb_ref, ids_ref, o_ref, *, n_groups):
    def body(i, carry):
        r0 = pl.multiple_of(i * SUBLANE, SUBLANE)
        ids = ids_ref[pl.ds(r0, SUBLANE), :]
        for h in range(A_HEADS):
            acc = jnp.full((SUBLANE, LANE), NEG, F32)
            for b in range(N_BUCKETS):
                acc = jnp.where(ids == b, rb_ref[b, h], acc)
            o_ref[h, pl.ds(r0, SUBLANE), :] = acc
        return carry

    lax.fori_loop(0, n_groups, body, 0)


def _bias_tables(rel_bias, ids):
    rows = ids.shape[0]
    return pl.pallas_call(
        functools.partial(_bias_kernel, n_groups=rows // SUBLANE),
        in_specs=[pl.BlockSpec(memory_space=pltpu.SMEM),
                  pl.BlockSpec((rows, LANE), lambda: (0, 0))],
        out_specs=pl.BlockSpec((A_HEADS, rows, LANE), lambda: (0, 0, 0)),
        out_shape=jax.ShapeDtypeStruct((A_HEADS, rows, LANE), F32),
        name="bias_tables",
    )(rel_bias, jnp.asarray(ids))


def _pack_in_proj(w_in, b_in):
    offs = np.cumsum((0,) + IN_SPLITS)
    names = ("mq", "mk", "mv", "mo", "mz", "mi", "mf", "aq", "ck", "cv", "sk", "sv", "wk", "wv", "ga", "za")
    sl = {n: (int(offs[i]), int(offs[i + 1])) for i, n in enumerate(names)}

    def cols(a, name, lo=None, hi=None):
        s, e = sl[name]
        if lo is not None:
            s, e = s + lo, s + hi
        return a[..., s:e]

    def rows_of(a, kn, vn):
        return [cols(a, kn, 0, A_HD), cols(a, vn, 0, A_HD), cols(a, kn, A_HD, 2 * A_HD), cols(a, vn, A_HD, 2 * A_HD)]

    def pack(a):
        def zeros(n):
            return jnp.zeros(a.shape[:-1] + (n,), a.dtype)
        parts = [cols(a, n) for n in ("mq", "mk", "mv", "mo", "mz")]
        parts += [cols(a, "mi"), cols(a, "mf"), zeros(LANE - 2 * M_HEADS)]
        parts += [cols(a, "aq"), cols(a, "za"), cols(a, "ga"), zeros(LANE - 3 * A_HEADS)]
        parts += rows_of(a, "ck", "cv") + rows_of(a, "sk", "sv") + rows_of(a, "wk", "wv")
        return jnp.concatenate(parts, axis=-1)

    w = pack(w_in)
    b = pack(b_in.reshape(1, -1))
    wt = w[:, PM_W + PA_W:].T
    bt = b[:, PM_W + PA_W:].reshape(-1, 1)
    return w.astype(BF16), b, wt.astype(BF16), bt


def _proj_kernel(x_ref, sh_ref, sc_ref, w_ref, b_ref, wt_ref, bt_ref,
                 om_ref, oa_ref, oc_ref, os_ref, ow_ref, oct_ref, ost_ref, owt_ref):
    h = _ln_rows(x_ref[...]) * (1.0 + sc_ref[...]) + sh_ref[...]
    hb = h.astype(BF16)
    lo = 0
    for o_ref in (om_ref, oa_ref, oc_ref, os_ref, ow_ref):
        n = o_ref.shape[-1]
        o_ref[...] = _dot(hb, w_ref[:, lo:lo + n]) + b_ref[:, lo:lo + n]
        lo += n
    t = _nt(wt_ref[...], hb) + bt_ref[...]
    for i, o_ref in enumerate((oct_ref, ost_ref, owt_ref)):
        o_ref[0] = t[i * ROW_W:(i + 1) * ROW_W]


def _project(x, shift, scale, packed, groups, tm):
    w, b, wt, bt = packed
    rows = x.shape[0]
    per = rows // groups // tm
    if shift.ndim == 3:
        mod_spec = pl.BlockSpec((None, 1, D_MODEL), lambda i: (i // per, 0, 0))
    else:
        mod_spec = pl.BlockSpec((tm, D_MODEL), lambda i: (i, 0))
    widths = (PM_W, PA_W, ROW_W, ROW_W, ROW_W)
    return pl.pallas_call(
        _proj_kernel,
        grid=(rows // tm,),
        in_specs=[pl.BlockSpec((tm, D_MODEL), lambda i: (i, 0)), mod_spec, mod_spec,
                  pl.BlockSpec((D_MODEL, PW_TOTAL), lambda i: (0, 0)),
                  pl.BlockSpec((1, PW_TOTAL), lambda i: (0, 0)),
                  pl.BlockSpec((3 * ROW_W, D_MODEL), lambda i: (0, 0)),
                  pl.BlockSpec((3 * ROW_W, 1), lambda i: (0, 0))],
        out_specs=[pl.BlockSpec((tm, n), lambda i: (i, 0)) for n in widths]
                  + [pl.BlockSpec((1, ROW_W, tm), lambda i: (i // per, 0, i % per))] * 3,
        out_shape=[jax.ShapeDtypeStruct((rows, n), F32) for n in widths]
                  + [jax.ShapeDtypeStruct((groups, ROW_W, rows // groups), F32)] * 3,
        compiler_params=_cparams("arbitrary"),
        name="in_proj",
    )(x, shift, scale, w, b, wt, bt)


def _mlstm_head_out(h, o_pre, z_pre, g_row):
    return jax.nn.sigmoid(o_pre) * (_ln_rows(h) * g_row) * _silu(z_pre)


def _mlstm_prompt_kernel(q_ref, k_ref, v_ref, o_ref, z_ref, g_ref, ng_ref, mix_ref, c_ref, n_ref, m_ref):
    L = M_CHUNK

    @pl.when(pl.program_id(1) == 0)
    def _():
        c_ref[...] = jnp.zeros_like(c_ref)
        n_ref[...] = jnp.zeros_like(n_ref)
        m_ref[...] = jnp.zeros_like(m_ref)

    gates = g_ref[...]
    gates_t = gates.T
    row = lax.broadcasted_iota(jnp.int32, (L, L), 0)
    col = lax.broadcasted_iota(jnp.int32, (L, L), 1)
    tril = col <= row
    cum = _dot(tril.astype(F32), _log_sigmoid(gates), HIGHEST)
    cum_t = _dot(_log_sigmoid(gates_t), (row <= col).astype(F32), HIGHEST)
    for h in range(M_HEADS):
        hs = slice(h * M_HD, (h + 1) * M_HD)
        b_col = cum[:, M_HEADS + h:M_HEADS + h + 1]
        b_row = cum_t[M_HEADS + h:M_HEADS + h + 1, :]
        ig_col = gates[:, h:h + 1]
        ig_row = gates_t[h:h + 1, :]
        m_prev = m_ref[0, h:h + 1, 0:1]
        c_prev = c_ref[0, h]
        n_prev = n_ref[0, h:h + 1, :]
        d = jnp.where(tril, b_col - b_row + ig_row, NEG)
        inter = b_col + m_prev
        m_t = jnp.maximum(inter, jnp.max(d, axis=1, keepdims=True))
        w_inter = jnp.exp(inter - m_t)
        q = q_ref[:, hs]
        ks = k_ref[:, hs] * (M_HD ** -0.5)
        v = v_ref[:, hs]
        qb, kb, vb = q.astype(BF16), ks.astype(BF16), v.astype(BF16)
        qk = _nt(qb, kb) * jnp.exp(d - m_t)
        num = w_inter * _dot(qb, c_prev.astype(BF16)) + _dot(qk.astype(BF16), vb)
        den = w_inter * jnp.sum(q * n_prev, axis=1, keepdims=True) + jnp.sum(qk, axis=1, keepdims=True)
        hh = num / jnp.maximum(jnp.abs(den), jnp.exp(-m_t))
        m_new = m_t[L - 1:L, :]
        b_last = b_col[L - 1:L, :]
        w_c = jnp.exp(b_last + m_prev - m_new)
        w_s = jnp.exp(b_last - b_col + ig_col - m_new)
        kw = ks * w_s
        c_ref[0, h] = w_c * c_prev + _dot(kw.T.astype(BF16), vb)
        n_ref[0, h:h + 1, :] = w_c * n_prev + jnp.sum(kw, axis=0, keepdims=True)
        m_ref[0, h:h + 1, :] = jnp.broadcast_to(m_new, (1, M_HD))
        mix_ref[:, hs] = _mlstm_head_out(hh, o_ref[:, hs], z_ref[:, hs], ng_ref[:, hs])


def _mlstm_prompt(pm, norm_g, batch, seq):
    nc = seq // M_CHUNK
    rows = batch * seq

    def col_spec(j, width=M_WIDTH):
        return pl.BlockSpec((M_CHUNK, width), lambda b, c: (b * nc + c, j))

    return pl.pallas_call(
        _mlstm_prompt_kernel,
        grid=(batch, nc),
        in_specs=[col_spec(0), col_spec(1), col_spec(2), col_spec(3), col_spec(4),
                  pl.BlockSpec((M_CHUNK, LANE), lambda b, c: (b * nc + c, 5 * M_WIDTH // LANE)),
                  pl.BlockSpec((1, M_WIDTH), lambda b, c: (0, 0))],
        out_specs=[pl.BlockSpec((M_CHUNK, M_WIDTH), lambda b, c: (b * nc + c, 0)),
                   pl.BlockSpec((1, M_HEADS, M_HD, M_HD), lambda b, c: (b, 0, 0, 0)),
                   pl.BlockSpec((1, M_HEADS, M_HD), lambda b, c: (b, 0, 0)),
                   pl.BlockSpec((1, M_HEADS, M_HD), lambda b, c: (b, 0, 0))],
        out_shape=[jax.ShapeDtypeStruct((rows, M_WIDTH), F32),
                   jax.ShapeDtypeStruct((batch, M_HEADS, M_HD, M_HD), F32),
                   jax.ShapeDtypeStruct((batch, M_HEADS, M_HD), F32),
                   jax.ShapeDtypeStruct((batch, M_HEADS, M_HD), F32)],
        compiler_params=_cparams("arbitrary", "arbitrary"),
        name="mlstm_prompt",
    )(pm, pm, pm, pm, pm, pm, norm_g.reshape(1, M_WIDTH))


MS_BB = 16
MS_ROWS = PM_W // LANE


def _mlstm_sample_kernel(x_ref, c_ref, n_ref, m_ref, ng_ref, mix_ref, co_ref, no_ref, mo_ref):
    qk_cols = x_ref[:, 0:2 * M_HEADS, :].reshape(MS_BB * 2 * M_HEADS, M_HD).T
    for bi in range(MS_BB):
        xb = x_ref[bi]
        for h in range(M_HEADS):
            cb = bi * 2 * M_HEADS + h
            q_col = qk_cols[:, cb:cb + 1]
            k_col = qk_cols[:, cb + M_HEADS:cb + M_HEADS + 1] * (M_HD ** -0.5)
            q_row = xb[h:h + 1, :]
            k_row = xb[M_HEADS + h:M_HEADS + h + 1, :] * (M_HD ** -0.5)
            v_row = xb[2 * M_HEADS + h:2 * M_HEADS + h + 1, :]
            o_row = xb[3 * M_HEADS + h:3 * M_HEADS + h + 1, :]
            z_row = xb[4 * M_HEADS + h:4 * M_HEADS + h + 1, :]
            ig = xb[5 * M_HEADS:5 * M_HEADS + 1, h:h + 1]
            lf = _log_sigmoid(xb[5 * M_HEADS:5 * M_HEADS + 1, M_HEADS + h:M_HEADS + h + 1])
            c_prev = c_ref[bi, h]
            n_prev = n_ref[bi, h:h + 1, :]
            m_prev = m_ref[bi, h:h + 1, 0:1]
            inter = lf + m_prev
            m_t = jnp.maximum(inter, ig)
            w_inter = jnp.exp(inter - m_t)
            w_s = jnp.exp(ig - m_t)
            qk = jnp.sum(q_row * k_row, axis=1, keepdims=True) * w_s
            q_c = jnp.sum(q_col * c_prev, axis=0, keepdims=True)
            num = w_inter * q_c + qk * v_row
            den = w_inter * jnp.sum(q_row * n_prev, axis=1, keepdims=True) + qk
            hh = num / jnp.maximum(jnp.abs(den), jnp.exp(-m_t))
            co_ref[bi, h] = w_inter * c_prev + (w_s * k_col) * v_row
            no_ref[bi, h:h + 1, :] = w_inter * n_prev + w_s * k_row
            mo_ref[bi, h:h + 1, :] = jnp.broadcast_to(m_t, (1, M_HD))
            hs = slice(h * M_HD, (h + 1) * M_HD)
            mix_ref[bi, h:h + 1, :] = _mlstm_head_out(hh, o_row, z_row, ng_ref[:, hs])


def _mlstm_sample(pm, norm_g, c0, n0, m0):
    nb = pm.shape[0]
    x3 = pm.reshape(nb, MS_ROWS, LANE)
    m_b = jnp.broadcast_to(m0[:, :, None], (nb, M_HEADS, M_HD))
    bb = MS_BB
    state_specs = [pl.BlockSpec((bb, M_HEADS, M_HD, M_HD), lambda i: (i, 0, 0, 0)),
                   pl.BlockSpec((bb, M_HEADS, M_HD), lambda i: (i, 0, 0)),
                   pl.BlockSpec((bb, M_HEADS, M_HD), lambda i: (i, 0, 0))]
    return pl.pallas_call(
        _mlstm_sample_kernel,
        grid=(nb // bb,),
        in_specs=[pl.BlockSpec((bb, MS_ROWS, LANE), lambda i: (i, 0, 0))] + state_specs
                 + [pl.BlockSpec((1, M_WIDTH), lambda i: (0, 0))],
        out_specs=[pl.BlockSpec((bb, M_HEADS, M_HD), lambda i: (i, 0, 0))] + state_specs,
        out_shape=[jax.ShapeDtypeStruct((nb, M_HEADS, M_HD), F32),
                   jax.ShapeDtypeStruct((nb, M_HEADS, M_HD, M_HD), F32),
                   jax.ShapeDtypeStruct((nb, M_HEADS, M_HD), F32),
                   jax.ShapeDtypeStruct((nb, M_HEADS, M_HD), F32)],
        compiler_params=_cparams("arbitrary"),
        name="mlstm_sample",
    )(x3, c0, n0, m_b, norm_g.reshape(1, M_WIDTH))


KVROW_W = 2 * A_HD
HALF_W = 2 * KVROW_W


def _pack_compress(w1, w2):
    def block_diag(k, v):
        z = jnp.zeros_like(k)
        return jnp.concatenate([jnp.concatenate([k, z], axis=-1), jnp.concatenate([z, v], axis=-1)], axis=-2)

    wbd = block_diag(w1[0], w1[1])
    wcat = jnp.concatenate([wbd[:CMP_STRIDE], wbd[CMP_STRIDE:]], axis=-1)
    return wcat.reshape(CMP_STRIDE * KVROW_W, HALF_W).astype(BF16), block_diag(w2[0], w2[1]).astype(BF16)


def _cmp_const_kernel(pe_ref, w_ref, b_ref, o_ref):
    for c in range(2):
        o_ref[c] = _dot(pe_ref[c], w_ref[c], HIGHEST) + b_ref[c]


def _compress_const(pe, w1, b1):
    k = CMP_LEN * A_HD
    pe8 = jnp.broadcast_to(pe.reshape(2, 1, k), (2, SUBLANE, k))
    out = pl.pallas_call(
        _cmp_const_kernel,
        out_shape=jax.ShapeDtypeStruct((2, SUBLANE, A_HD), F32),
        name="compress_const",
    )(pe8, w1.reshape(2, k, A_HD), b1.reshape(2, 1, A_HD))
    return jnp.concatenate([out[0, 0:1], out[1, 0:1]], axis=-1)


def _compress_halves(load_rows, wcat_ref, const_ref, w2_ref, n_half):
    halves = jnp.concatenate(
        [jnp.concatenate([load_rows(p, kv) for kv in range(A_KV)], axis=0).astype(BF16) for p in range(CMP_STRIDE)],
        axis=1)
    acc = _dot(halves, wcat_ref[...])
    pre = acc[:, :KVROW_W] + pltpu.roll(acc[:, KVROW_W:], A_KV * n_half - 1, 0) + const_ref[...]
    return _dot(_gelu_tanh(pre).astype(BF16), w2_ref[...])


def _compress_prompt_kernel(x0_ref, x1_ref, wcat_ref, const_ref, w2_ref, kk_ref, kvt_ref, *, n_half):
    x_refs = (x0_ref, x1_ref)
    kc = _compress_halves(lambda p, kv: x_refs[kv][pl.ds(p, n_half, stride=CMP_STRIDE), :],
                          wcat_ref, const_ref, w2_ref, n_half)
    kct = kc.T
    for kv in range(A_KV):
        kk_ref[0, kv] = kc[kv * n_half:(kv + 1) * n_half, 0:A_HD]
        kvt_ref[0, kv] = kct[A_HD:, kv * n_half:(kv + 1) * n_half]


def _compress_prompt(rows, wcat, const_row, w2bd, batch, seq):
    n_half = seq // CMP_STRIDE
    return pl.pallas_call(
        functools.partial(_compress_prompt_kernel, n_half=n_half),
        grid=(batch,),
        in_specs=[pl.BlockSpec((seq, KVROW_W), lambda b: (b, 0)),
                  pl.BlockSpec((seq, KVROW_W), lambda b: (b, 1)),
                  pl.BlockSpec((CMP_STRIDE * KVROW_W, HALF_W), lambda b: (0, 0)),
                  pl.BlockSpec((1, KVROW_W), lambda b: (0, 0)),
                  pl.BlockSpec((KVROW_W, KVROW_W), lambda b: (0, 0))],
        out_specs=[pl.BlockSpec((1, A_KV, n_half, A_HD), lambda b: (b, 0, 0, 0)),
                   pl.BlockSpec((1, A_KV, A_HD, n_half), lambda b: (b, 0, 0, 0))],
        out_shape=[jax.ShapeDtypeStruct((batch, A_KV, n_half, A_HD), F32),
                   jax.ShapeDtypeStruct((batch, A_KV, A_HD, n_half), F32)],
        compiler_params=_cparams("arbitrary"),
        name="compress_prompt",
    )(rows, rows, wcat, const_row, w2bd)


CMP_PAT = 16


def _static_ids(p_len):
    i = np.arange(TQ)[None, :]
    c = np.arange(CMP_PAT)[:, None]
    cmp_a = _bucket_np(i + (TQ - (CMP_LEN - 1)) - CMP_STRIDE * c)
    cmp_b = _bucket_np(i - CMP_STRIDE * c - (CMP_LEN - 1))
    r = np.arange(TQ)[:, None]
    slc_diag = _bucket_np(i - r)
    slc_sub = _bucket_np(TQ + i - r)
    slc_far = np.full((TQ, TQ), FAR_BUCKET, np.int32)
    rw = np.arange(WINDOW + TQ)[:, None]
    dw = WINDOW + i - rw
    win = np.where(dw > WINDOW, MASKED_ID, _bucket_np(dw))
    n_half = p_len // CMP_STRIDE
    n = np.arange(n_half)
    cs = _bucket_np(p_len - (CMP_STRIDE * n + CMP_LEN - 1))
    cs[n_half - 1] = MASKED_ID
    cs_rows = -(-n_half // LANE)
    cs_pad = np.full((cs_rows * LANE,), MASKED_ID, np.int32)
    cs_pad[:n_half] = cs
    ws = _bucket_np(WINDOW - np.arange(WINDOW))
    parts = [cmp_a, cmp_b, slc_diag, slc_sub, slc_far, win, cs_pad.reshape(cs_rows, LANE),
             ws.reshape(WINDOW // LANE, LANE)]
    offs = np.cumsum([0] + [p.shape[0] for p in parts])
    total = -(-int(offs[-1]) // SUBLANE) * SUBLANE
    ids = np.full((total, LANE), MASKED_ID, np.int32)
    ids[:offs[-1]] = np.concatenate(parts, axis=0)
    return ids, [int(o) for o in offs]


def _cover_np(n_cmp_rows, n_cmp, n_slc_rows, n_slc):
    cs = np.arange(n_cmp_rows)[:, None] * CMP_STRIDE
    ss = np.arange(n_slc_rows)[None, :] * SLC_BLOCK
    cov = (cs <= ss + SLC_BLOCK - 1) & (cs + CMP_LEN - 1 >= ss)
    cov &= (np.arange(n_cmp_rows)[:, None] < n_cmp) & (np.arange(n_slc_rows)[None, :] < n_slc)
    return cov.astype(np.float32)


def _softmax_keys_on_rows(s):
    m = jnp.max(s, axis=0, keepdims=True)
    m = jnp.where(m > 0.5 * NEG, m, 0.0)
    e = jnp.exp(s - m)
    tot = jnp.sum(e, axis=0, keepdims=True)
    return e / jnp.where(tot > 0.0, tot, 1.0)


def _softmax_keys_on_lanes(s, s_new=None):
    m = jnp.max(s, axis=1, keepdims=True)
    if s_new is not None:
        m = jnp.maximum(m, s_new)
    m = jnp.where(m > 0.5 * NEG, m, 0.0)
    e = jnp.exp(s - m)
    tot = jnp.sum(e, axis=1, keepdims=True)
    e_new = None
    if s_new is not None:
        e_new = jnp.exp(s_new - m)
        tot = tot + e_new
    return e, e_new, 1.0 / jnp.where(tot > 0.0, tot, 1.0)


def _cmp_attend_kernel(rb_ref, q_ref, kk_ref, kvt_ref, pt_ref, cov_ref, o_ref, sel_ref, bscr, *, nc, ns):
    k = pl.program_id(1)
    start = pl.multiple_of(jnp.maximum(SUBLANE * k - SUBLANE, 0), SUBLANE)
    variant = jnp.where(k == 0, 1, 0)
    row = lax.broadcasted_iota(jnp.int32, (nc, TQ), 0)
    t = k * TQ + lax.broadcasted_iota(jnp.int32, (ns, TQ), 1)
    blk = lax.broadcasted_iota(jnp.int32, (ns, TQ), 0)
    cur = t // SLC_BLOCK
    valid = blk * SLC_BLOCK <= t
    forced = (blk == 0) | (blk == cur) | (blk == cur - 1)
    for kv in range(A_KV):
        kk = kk_ref[0, kv].astype(BF16)
        kvt = kvt_ref[0, kv].astype(BF16)
        imp = jnp.zeros((nc, TQ), F32)
        for g in range(A_GROUP):
            h = kv * A_GROUP + g
            hs = slice(h * A_HD, (h + 1) * A_HD)
            bscr[...] = jnp.where(row < start, rb_ref[FAR_BUCKET, h], NEG)
            bscr[pl.ds(start, CMP_PAT), :] = pt_ref[variant, h]
            s = _nt(kk, q_ref[:, hs].astype(BF16)) * ATT_SCALE + bscr[...]
            p = _softmax_keys_on_rows(s)
            o_ref[0, hs, :] = _dot(kvt, p.astype(BF16))
            imp = imp + p
        score = _dot(cov_ref[...], imp, HIGHEST)
        sc = jnp.where(forced, jnp.inf, jnp.where(valid, score, -jnp.inf))
        cnt = jnp.zeros((ns, TQ), jnp.int32)
        for j in range(ns):
            r = sc[j:j + 1, :]
            before = (r > sc) | ((r == sc) & (blk > j))
            cnt = cnt + before.astype(jnp.int32)
        sel_ref[0, kv] = jnp.where(cnt < N_SEL, 0.0, NEG)


def _cmp_attend(rel_bias, pa, kk, kvt, pat, cov_t, batch, seq):
    nq = seq // TQ
    nc = seq // CMP_STRIDE
    ns = seq // SLC_BLOCK
    return pl.pallas_call(
        functools.partial(_cmp_attend_kernel, nc=nc, ns=ns),
        grid=(batch, nq),
        in_specs=[pl.BlockSpec(memory_space=pltpu.SMEM),
                  pl.BlockSpec((TQ, A_WIDTH), lambda b, k: (b * nq + k, 0)),
                  pl.BlockSpec((1, A_KV, nc, A_HD), lambda b, k: (b, 0, 0, 0)),
                  pl.BlockSpec((1, A_KV, A_HD, nc), lambda b, k: (b, 0, 0, 0)),
                  pl.BlockSpec((2, A_HEADS, CMP_PAT, TQ), lambda b, k: (0, 0, 0, 0)),
                  pl.BlockSpec((ns, nc), lambda b, k: (0, 0))],
        out_specs=[pl.BlockSpec((1, A_WIDTH, TQ), lambda b, k: (b, 0, k)),
                   pl.BlockSpec((1, A_KV, ns, TQ), lambda b, k: (b, 0, 0, k))],
        out_shape=[jax.ShapeDtypeStruct((batch, A_WIDTH, seq), F32),
                   jax.ShapeDtypeStruct((batch, A_KV, ns, seq), F32)],
        scratch_shapes=[pltpu.VMEM((nc, TQ), F32)],
        compiler_params=_cparams("arbitrary", "arbitrary"),
        name="cmp_attend",
    )(rel_bias, pa, kk, kvt, pat, cov_t)


SLC_CK = 4 * TQ
GROUP_LANES = A_GROUP * TQ
SLC_CLASSES = 4


def _stacked_queries(q_ref, kv, scale=ATT_SCALE):
    heads = [q_ref[:, (kv * A_GROUP + g) * A_HD:(kv * A_GROUP + g + 1) * A_HD] for g in range(A_GROUP)]
    return (jnp.concatenate(heads, axis=0) * scale).astype(BF16)


def _slc_attend_kernel(q_ref, k_ref, vt_ref, sel_ref, tab_ref, o_ref):
    k = pl.program_id(1)
    sub = SLC_CK // TQ
    q4 = [_stacked_queries(q_ref, kv, ATT_SCALE * LOG2E) for kv in range(A_KV)]
    upper = lax.broadcasted_iota(jnp.int32, (TQ, GROUP_LANES), 0) < SLC_BLOCK

    def body(j, carry):
        j0 = pl.multiple_of(j * SLC_CK, SLC_CK)
        out = []
        for kv in range(A_KV):
            m_run, l_run, acc = carry[kv]
            kj = k_ref[pl.ds(j0, SLC_CK), kv * LANE:kv * LANE + A_HD].astype(BF16)
            vt = vt_ref[0, kv * LANE + A_HD:(kv + 1) * LANE, pl.ds(j0, SLC_CK)].astype(BF16)
            s_all = _nt(kj, q4[kv])
            parts = []
            for u in range(sub):
                jj = j * sub + u
                cls = jnp.where(jj > k, SLC_CLASSES - 1, jnp.minimum(k - jj, 2))
                sel0 = jnp.concatenate([sel_ref[0, kv, pl.ds(2 * jj, 1), :]] * A_GROUP, axis=1)
                sel1 = jnp.concatenate([sel_ref[0, kv, pl.ds(2 * jj + 1, 1), :]] * A_GROUP, axis=1)
                parts.append(s_all[u * TQ:(u + 1) * TQ] + tab_ref[cls, kv] + jnp.where(upper, sel0, sel1))
            s = jnp.concatenate(parts, axis=0)
            m_new = jnp.maximum(m_run, jnp.max(s, axis=0, keepdims=True))
            alpha = jnp.exp2(m_run - m_new)
            p = jnp.exp2(s - m_new)
            l_new = alpha * l_run + jnp.sum(p, axis=0, keepdims=True)
            out.append((m_new, l_new, alpha * acc + _dot(vt, p.astype(BF16))))
        return tuple(out)

    init = tuple((jnp.full((1, GROUP_LANES), NEG, F32), jnp.zeros((1, GROUP_LANES), F32),
                  jnp.zeros((A_HD, GROUP_LANES), F32)) for _ in range(A_KV))
    res = lax.fori_loop(0, (k + sub) // sub, body, init)
    for kv in range(A_KV):
        _, l_run, acc = res[kv]
        o = acc / l_run
        for g in range(A_GROUP):
            h = kv * A_GROUP + g
            o_ref[0, h * A_HD:(h + 1) * A_HD, :] = o[:, g * TQ:(g + 1) * TQ]


def _slc_attend(pa, rows, rows_t, sel, tab, batch, seq):
    nq = seq // TQ
    ns = seq // SLC_BLOCK
    return pl.pallas_call(
        _slc_attend_kernel,
        grid=(batch, nq),
        in_specs=[pl.BlockSpec((TQ, A_WIDTH), lambda b, k: (b * nq + k, 0)),
                  pl.BlockSpec((seq, ROW_W), lambda b, k: (b, 0)),
                  pl.BlockSpec((1, ROW_W, seq), lambda b, k: (b, 0, 0)),
                  pl.BlockSpec((1, A_KV, ns, TQ), lambda b, k: (b, 0, 0, k)),
                  pl.BlockSpec((SLC_CLASSES, A_KV, TQ, GROUP_LANES), lambda b, k: (0, 0, 0, 0))],
        out_specs=pl.BlockSpec((1, A_WIDTH, TQ), lambda b, k: (b, 0, k)),
        out_shape=jax.ShapeDtypeStruct((batch, A_WIDTH, seq), F32),
        compiler_params=_cparams("arbitrary", "arbitrary"),
        name="slc_attend",
    )(pa, rows, rows_t, sel, tab)


WIN_SPAN = WINDOW + TQ


def _win_attend_kernel(q_ref, k_ref, vt_ref, bias_ref, o_ref):
    k = pl.program_id(1)
    r0 = pl.multiple_of(k * TQ, TQ)
    exists = lax.broadcasted_iota(jnp.int32, (WIN_SPAN, GROUP_LANES), 0) + k * TQ >= WINDOW
    for kv in range(A_KV):
        kw = k_ref[0, pl.ds(r0, WIN_SPAN), kv * LANE:kv * LANE + A_HD].astype(BF16)
        vt = vt_ref[0, kv * LANE + A_HD:(kv + 1) * LANE, pl.ds(r0, WIN_SPAN)].astype(BF16)
        s = _nt(kw, _stacked_queries(q_ref, kv)) + bias_ref[kv]
        p = _softmax_keys_on_rows(jnp.where(exists, s, NEG))
        o = _dot(vt, p.astype(BF16))
        for g in range(A_GROUP):
            h = kv * A_GROUP + g
            o_ref[0, h * A_HD:(h + 1) * A_HD, :] = o[:, g * TQ:(g + 1) * TQ]


def _win_attend(pa, rows_pad, rows_t_pad, bias_w, batch, seq):
    nq = seq // TQ
    return pl.pallas_call(
        _win_attend_kernel,
        grid=(batch, nq),
        in_specs=[pl.BlockSpec((TQ, A_WIDTH), lambda b, k: (b * nq + k, 0)),
                  pl.BlockSpec((1, seq + WINDOW, ROW_W), lambda b, k: (b, 0, 0)),
                  pl.BlockSpec((1, ROW_W, seq + WINDOW), lambda b, k: (b, 0, 0)),
                  pl.BlockSpec((A_KV, WIN_SPAN, GROUP_LANES), lambda b, k: (0, 0, 0))],
        out_specs=pl.BlockSpec((1, A_WIDTH, TQ), lambda b, k: (b, 0, k)),
        out_shape=jax.ShapeDtypeStruct((batch, A_WIDTH, seq), F32),
        compiler_params=_cparams("arbitrary", "arbitrary"),
        name="win_attend",
    )(pa, rows_pad, rows_t_pad, bias_w)


def _out_tail(x, mix_m, mix_a, gate, w_ref, b_ref, g_ref, beta_ref):
    y = (_dot(mix_m.astype(BF16), w_ref[:M_WIDTH]) + _dot(mix_a.astype(BF16), w_ref[M_WIDTH:]) + b_ref[...])
    return _ln_rows(DEEPNORM_ALPHA * x + gate * y) * g_ref[...] + beta_ref[...]


def _out_prompt_kernel(x_ref, mm_ref, oc_ref, os_ref, ow_ref, ga_ref, za_ref, gate_ref,
                       w_ref, b_ref, g_ref, beta_ref, y_ref):
    sig = jax.nn.sigmoid(ga_ref[...].T)
    parts = []
    for h in range(A_HEADS):
        hs = slice(h * A_HD, (h + 1) * A_HD)
        parts.append(sig[h:h + 1] * oc_ref[0, hs, :] + sig[A_HEADS + h:A_HEADS + h + 1] * os_ref[0, hs, :]
                     + sig[2 * A_HEADS + h:2 * A_HEADS + h + 1] * ow_ref[0, hs, :])
    ha = jnp.concatenate(parts, axis=0).T
    mix_a = ha * _silu(za_ref[...])
    y_ref[...] = _out_tail(x_ref[...], mm_ref[...], mix_a, gate_ref[...], w_ref, b_ref, g_ref, beta_ref)


def _out_prompt(x, mix_m, o_c, o_s, o_w, pa, gate, w_out, b_out, ln_g, ln_b, batch, seq):
    nq = seq // TQ
    rows = batch * seq
    branch = pl.BlockSpec((1, A_WIDTH, TQ), lambda i: (i // nq, 0, i % nq))
    vec = pl.BlockSpec((1, D_MODEL), lambda i: (0, 0))
    return pl.pallas_call(
        _out_prompt_kernel,
        grid=(rows // TQ,),
        in_specs=[pl.BlockSpec((TQ, D_MODEL), lambda i: (i, 0)),
                  pl.BlockSpec((TQ, M_WIDTH), lambda i: (i, 0)),
                  branch, branch, branch,
                  pl.BlockSpec((TQ, LANE), lambda i: (i, 2 * A_WIDTH // LANE)),
                  pl.BlockSpec((TQ, A_WIDTH), lambda i: (i, 1)),
                  pl.BlockSpec((None, 1, D_MODEL), lambda i: (i // nq, 0, 0)),
                  pl.BlockSpec((D_MODEL, D_MODEL), lambda i: (0, 0)),
                  vec, vec, vec],
        out_specs=pl.BlockSpec((TQ, D_MODEL), lambda i: (i, 0)),
        out_shape=jax.ShapeDtypeStruct((rows, D_MODEL), F32),
        compiler_params=_cparams("arbitrary"),
        name="out_prompt",
    )(x, mix_m, o_c, o_s, o_w, pa, pa, gate, w_out, b_out, ln_g, ln_b)


def _out_sample_kernel(x_ref, mm_ref, ha_ref, za_ref, gate_ref, w_ref, b_ref, g_ref, beta_ref, y_ref):
    mix_a = ha_ref[...] * _silu(za_ref[...])
    y_ref[...] = _out_tail(x_ref[...], mm_ref[...], mix_a, gate_ref[...], w_ref, b_ref, g_ref, beta_ref)


def _out_sample(x, mix_m, ha, pa, gate, w_out, b_out, ln_g, ln_b):
    rows = x.shape[0]
    vec = pl.BlockSpec((1, D_MODEL), lambda i: (0, 0))
    return pl.pallas_call(
        _out_sample_kernel,
        grid=(1,),
        in_specs=[pl.BlockSpec((rows, D_MODEL), lambda i: (0, 0)),
                  pl.BlockSpec((rows, M_WIDTH), lambda i: (0, 0)),
                  pl.BlockSpec((rows, A_WIDTH), lambda i: (0, 0)),
                  pl.BlockSpec((rows, A_WIDTH), lambda i: (0, 1)),
                  pl.BlockSpec((rows, D_MODEL), lambda i: (0, 0)),
                  pl.BlockSpec((D_MODEL, D_MODEL), lambda i: (0, 0)),
                  vec, vec, vec],
        out_specs=pl.BlockSpec((rows, D_MODEL), lambda i: (0, 0)),
        out_shape=jax.ShapeDtypeStruct((rows, D_MODEL), F32),
        compiler_params=_cparams("arbitrary"),
        name="out_sample",
    )(x, mix_m, ha, pa, gate, w_out, b_out, ln_g, ln_b)


def _nsa_prompt(rel_bias, pa, rc, rs, rw, st, wt, bias, offs, cmp_w, const_row, batch, seq):
    wcat, w2bd = cmp_w
    nc = seq // CMP_STRIDE
    ns = seq // SLC_BLOCK
    kk, kvt = _compress_prompt(rc, wcat, const_row, w2bd, batch, seq)
    pat = bias[:, offs[0]:offs[2]].reshape(A_HEADS, 2, CMP_PAT, TQ).transpose(1, 0, 2, 3)
    cov_t = jnp.asarray(_cover_np(nc, nc - 1, ns, ns).T)
    o_c, sel = _cmp_attend(rel_bias, pa, kk, kvt, pat, cov_t, batch, seq)

    def group_lanes(tiles):
        rows = tiles.shape[1]
        return tiles.reshape(A_KV, A_GROUP, rows, TQ).transpose(0, 2, 1, 3).reshape(A_KV, rows, GROUP_LANES)

    tab = group_lanes(bias[:, offs[2]:offs[5]]).reshape(A_KV, 3, TQ, GROUP_LANES).transpose(1, 0, 2, 3)
    tab = jnp.concatenate([tab * LOG2E, jnp.full((1,) + tab.shape[1:], NEG, F32)], axis=0)
    o_s = _slc_attend(pa, rs, st, sel, tab, batch, seq)
    rows_pad = jnp.pad(rw.reshape(batch, seq, ROW_W), ((0, 0), (WINDOW, 0), (0, 0)))
    rows_t_pad = jnp.pad(wt, ((0, 0), (0, 0), (WINDOW, 0)))
    o_w = _win_attend(pa, rows_pad, rows_t_pad, group_lanes(bias[:, offs[5]:offs[6]]), batch, seq)
    return o_c, o_s, o_w, sel


HALVES_PER_PAGE = 8
SEL_LANES = 256


def _cmp_sample_kernel(pt_ref, *refs, n_pages, p_len):
    del pt_ref
    pages = refs[:n_pages]
    q_ref, perm_ref, wcat_ref, const_ref, w2_ref, bias_ref, cov_ref, o_ref, idx_ref, rows_ref = refs[n_pages:]
    n_half = n_pages * HALVES_PER_PAGE
    perm = perm_ref[...]
    group = 8
    for j0 in range(0, n_pages, group):
        tiles = [pages[j][0, kv].reshape(KVROW_W, PAGE_ROWS).astype(BF16)
                 for j in range(j0, j0 + group) for kv in range(A_KV)]
        moved = _dot(jnp.concatenate(tiles, axis=0), perm)
        for i in range(group * A_KV):
            j, kv = j0 + i // A_KV, i % A_KV
            rows = moved[i * KVROW_W:(i + 1) * KVROW_W].T
            for p in range(CMP_STRIDE):
                rows_ref[kv, p, j * HALVES_PER_PAGE:(j + 1) * HALVES_PER_PAGE, :] = (
                    rows[p * HALVES_PER_PAGE:(p + 1) * HALVES_PER_PAGE])

    kc = _compress_halves(lambda p, kv: rows_ref[kv, p],
                          wcat_ref, const_ref, w2_ref, n_half).astype(BF16)
    kc0, kc1 = kc[:n_half], kc[n_half:]
    q8 = q_ref[0].astype(BF16)
    first = lax.broadcasted_iota(jnp.int32, (A_HEADS, n_half), 0) < A_GROUP
    s = jnp.where(first, _nt(q8, kc0[:, :A_HD]), _nt(q8, kc1[:, :A_HD])) * ATT_SCALE + bias_ref[...]
    e, _, inv = _softmax_keys_on_lanes(s)
    p = e * inv
    pb = p.astype(BF16)
    first_o = lax.broadcasted_iota(jnp.int32, (A_HEADS, A_HD), 0) < A_GROUP
    o_ref[0] = jnp.where(first_o, _dot(pb, kc0[:, A_HD:]), _dot(pb, kc1[:, A_HD:]))
    hrow = lax.broadcasted_iota(jnp.int32, (A_HEADS, n_half), 0)
    imp0 = jnp.sum(jnp.where(first, p, 0.0), axis=0, keepdims=True)
    imp1 = jnp.sum(jnp.where(first, 0.0, p), axis=0, keepdims=True)
    imp = jnp.where(hrow == 0, imp0, jnp.where(hrow == 1, imp1, 0.0))
    score = _dot(imp, cov_ref[...], HIGHEST)
    n_slc = p_len // SLC_BLOCK + 1
    cur = p_len // SLC_BLOCK
    lane = lax.broadcasted_iota(jnp.int32, (A_HEADS, SEL_LANES), 1)
    forced = (lane == 0) | (lane == cur) | (lane == cur - 1)
    valid = lane * SLC_BLOCK <= p_len
    sc = jnp.where(forced, jnp.inf, jnp.where(valid, score, -jnp.inf))
    k_sel = float(min(N_SEL, n_slc))
    sub = lax.broadcasted_iota(jnp.int32, (SEL_LANES, SEL_LANES), 0)
    lan = lax.broadcasted_iota(jnp.int32, (SEL_LANES, SEL_LANES), 1)
    slot_l = lax.broadcasted_iota(jnp.int32, (SEL_LANES, LANE), 1).astype(F32)
    blk_s = lax.broadcasted_iota(jnp.int32, (SEL_LANES, LANE), 0).astype(F32)
    out_row = lax.broadcasted_iota(jnp.int32, (A_HEADS, LANE), 0)
    picks = jnp.zeros((A_HEADS, LANE), F32)
    for kv in range(A_KV):
        row = sc[kv:kv + 1, :]
        col = jnp.sum(jnp.where(sub == lan, row, 0.0), axis=1, keepdims=True)
        before_c = (lan < n_slc) & ((row > col) | ((row == col) & (lan < sub)))
        sel_c = (jnp.sum(before_c.astype(F32), axis=1, keepdims=True) < k_sel) & (sub[:, :1] < n_slc)
        before_r = (sub < n_slc) & ((col > row) | ((col == row) & (sub < lan)))
        sel_r = (jnp.sum(before_r.astype(F32), axis=0, keepdims=True) < k_sel) & (lan[:1] < n_slc)
        slot_c = jnp.sum(jnp.where((lan < sub) & sel_r, 1.0, 0.0), axis=1, keepdims=True)
        hit = sel_c & (slot_c == slot_l)
        picks_kv = jnp.sum(jnp.where(hit, blk_s, 0.0), axis=0, keepdims=True)
        picks = jnp.where(out_row == kv, picks_kv, picks)
    idx_ref[0] = picks.astype(jnp.int32)


def _cmp_sample(page_table, cache_pages, q3, wcat, const_row, w2bd, bias_cs, cov, p_len):
    nb, n_pages = page_table.shape
    n_half = n_pages * HALVES_PER_PAGE

    def page_spec(j):
        return pl.BlockSpec((1, A_KV, 2, A_HD, PAGE_ROWS), lambda b, pt: (pt[b * n_pages + j], 0, 0, 0, 0))

    r = np.arange(PAGE_ROWS)
    perm = np.zeros((PAGE_ROWS, PAGE_ROWS), np.float32)
    perm[r, (r % CMP_STRIDE) * HALVES_PER_PAGE + r // CMP_STRIDE] = 1.0
    const2 = lambda b, pt: (0, 0)
    grid_spec = pltpu.PrefetchScalarGridSpec(
        num_scalar_prefetch=1,
        grid=(nb,),
        in_specs=[page_spec(j) for j in range(n_pages)]
                 + [pl.BlockSpec((1, A_HEADS, A_HD), lambda b, pt: (b, 0, 0)),
                    pl.BlockSpec((PAGE_ROWS, PAGE_ROWS), const2),
                    pl.BlockSpec((CMP_STRIDE * KVROW_W, HALF_W), const2),
                    pl.BlockSpec((1, KVROW_W), const2),
                    pl.BlockSpec((KVROW_W, KVROW_W), const2),
                    pl.BlockSpec((A_HEADS, n_half), const2),
                    pl.BlockSpec((n_half, SEL_LANES), const2)],
        out_specs=[pl.BlockSpec((1, A_HEADS, A_HD), lambda b, pt: (b, 0, 0)),
                   pl.BlockSpec((1, A_HEADS, LANE), lambda b, pt: (b, 0, 0))],
        scratch_shapes=[pltpu.VMEM((A_KV, CMP_STRIDE, n_half, KVROW_W), F32)],
    )
    return pl.pallas_call(
        functools.partial(_cmp_sample_kernel, n_pages=n_pages, p_len=p_len),
        grid_spec=grid_spec,
        out_shape=[jax.ShapeDtypeStruct((nb, A_HEADS, A_HD), F32),
                   jax.ShapeDtypeStruct((nb, A_HEADS, LANE), jnp.int32)],
        compiler_params=_cparams("arbitrary"),
        name="cmp_sample",
    )(page_table.reshape(-1), *([cache_pages] * n_pages), q3, jnp.asarray(perm, BF16), wcat, const_row, w2bd,
      bias_cs, cov)


PAGE_ROWS = 128
BLOCKS_PER_PAGE = PAGE_ROWS // SLC_BLOCK


def _slc_sample_kernel(idx_ref, pt_ref, *refs, p_len):
    del pt_ref
    n_blk = A_KV * N_SEL
    blocks = refs[:n_blk]
    q_ref, snew_ref, win_ref, wnew_ref, wcol_ref, oc_ref, g_ref, rbt_ref, bw_ref, ha_ref, wbuf_ref = refs[n_blk:]
    b = pl.program_id(0)
    past_blocks = p_len // SLC_BLOCK
    n_keys = N_SEL * PAGE_ROWS
    n_buf = win_ref.shape[-1]
    qf = q_ref[0]
    q8 = qf.astype(BF16)
    first_o = lax.broadcasted_iota(jnp.int32, (A_HEADS, A_HD), 0) < A_GROUP
    lane = lax.broadcasted_iota(jnp.int32, (1, n_keys), 1)
    slot = lane // PAGE_ROWS
    in_page = lane % PAGE_ROWS
    bucket_row = lax.broadcasted_iota(jnp.int32, (N_BUCKETS, n_keys), 0)
    bias_new = rbt_ref[:, 0:1]

    def new_key_logit(row_ref, kv):
        k_new = row_ref[0, :, kv * LANE:kv * LANE + A_HD]
        v_new = row_ref[0, :, kv * LANE + A_HD:(kv + 1) * LANE]
        return jnp.sum(qf * k_new, axis=1, keepdims=True) * ATT_SCALE + bias_new, v_new

    o_s, o_w = [], []
    for kv in range(A_KV):
        blk_of = jnp.zeros((1, n_keys), jnp.int32)
        has_new = False
        for j in range(N_SEL):
            blk = idx_ref[(b * A_KV + kv) * N_SEL + j]
            blk_of = jnp.where(slot == j, blk, blk_of)
            has_new = jnp.logical_or(has_new, blk == past_blocks)
        pos = jnp.minimum(blk_of, past_blocks - 1) // BLOCKS_PER_PAGE * PAGE_ROWS + in_page
        valid = (pos // SLC_BLOCK == blk_of) & (pos < p_len)
        onehot = (bucket_row == _bucket_dyn(p_len - pos)).astype(F32)
        bias = _dot(rbt_ref[...], onehot, HIGHEST)
        kt = jnp.concatenate([blocks[kv * N_SEL + j][0, 0, 0] for j in range(N_SEL)], axis=1).astype(BF16)
        vt = jnp.concatenate([blocks[kv * N_SEL + j][0, 0, 1] for j in range(N_SEL)], axis=1).astype(BF16)
        s = jnp.where(valid, _dot(q8, kt) * ATT_SCALE + bias, NEG)
        s_new, v_new = new_key_logit(snew_ref, kv)
        s_new = jnp.where(has_new, s_new, NEG)
        e, e_new, inv = _softmax_keys_on_lanes(s, s_new)
        o_s.append((_nt(e.astype(BF16), vt) + e_new * v_new) * inv)
        sw = _dot(q8, win_ref[0, kv, 0].astype(BF16)) * ATT_SCALE + bw_ref[...]
        sw_new, vw_new = new_key_logit(wnew_ref, kv)
        e, e_new, inv = _softmax_keys_on_lanes(sw, sw_new)
        o_w.append((_nt(e.astype(BF16), win_ref[0, kv, 1].astype(BF16)) + e_new * vw_new) * inv)
    g = jax.nn.sigmoid(g_ref[0])
    ha_ref[0] = (g[0] * oc_ref[0] + g[1] * jnp.where(first_o, o_s[0], o_s[1])
                 + g[2] * jnp.where(first_o, o_w[0], o_w[1]))
    last = lax.broadcasted_iota(jnp.int32, (A_HD, n_buf), 1) == n_buf - 1
    for kv in range(A_KV):
        for c in range(2):
            r0 = (kv * 2 + c) * A_HD
            wbuf_ref[0, kv, c] = jnp.where(last, wcol_ref[0, r0:r0 + A_HD, :],
                                           pltpu.roll(win_ref[0, kv, c], n_buf - 1, 1))


def _slc_sample(idx, page_table, cache_t, q3, slc_new, win_t, win_new, win_new_col, o_c, gates, rb_t, bias_ws, p_len):
    nb, n_pages = page_table.shape
    past_blocks = p_len // SLC_BLOCK
    n_buf = win_t.shape[-1]

    def block_spec(kv, j):
        def index_map(b, idx_ref, pt_ref):
            blk = jnp.minimum(idx_ref[(b * A_KV + kv) * N_SEL + j], past_blocks - 1)
            return (pt_ref[b * n_pages + blk // BLOCKS_PER_PAGE], kv, 0, 0, 0)
        return pl.BlockSpec((1, 1, 2, A_HD, PAGE_ROWS), index_map)

    per_seq3 = lambda b, i, p: (b, 0, 0)
    win_spec = pl.BlockSpec((1, A_KV, 2, A_HD, n_buf), lambda b, i, p: (b, 0, 0, 0, 0))
    grid_spec = pltpu.PrefetchScalarGridSpec(
        num_scalar_prefetch=2,
        grid=(nb,),
        in_specs=[block_spec(kv, j) for kv in range(A_KV) for j in range(N_SEL)]
                 + [pl.BlockSpec((1, A_HEADS, A_HD), per_seq3),
                    pl.BlockSpec((1, 1, ROW_W), per_seq3),
                    win_spec,
                    pl.BlockSpec((1, 1, ROW_W), per_seq3),
                    pl.BlockSpec((1, ROW_W, 1), per_seq3),
                    pl.BlockSpec((1, A_HEADS, A_HD), per_seq3),
                    pl.BlockSpec((1, 3, A_HEADS, 1), lambda b, i, p: (b, 0, 0, 0)),
                    pl.BlockSpec((A_HEADS, N_BUCKETS), lambda b, i, p: (0, 0)),
                    pl.BlockSpec((A_HEADS, n_buf), lambda b, i, p: (0, 0))],
        out_specs=[pl.BlockSpec((1, A_HEADS, A_HD), per_seq3), win_spec],
    )
    return pl.pallas_call(
        functools.partial(_slc_sample_kernel, p_len=p_len),
        grid_spec=grid_spec,
        out_shape=[jax.ShapeDtypeStruct((nb, A_HEADS, A_HD), F32),
                   jax.ShapeDtypeStruct(win_t.shape, F32)],
        compiler_params=_cparams("arbitrary"),
        name="slc_win_sample",
    )(idx.reshape(-1), page_table.reshape(-1), *([cache_t] * (A_KV * N_SEL)),
      q3, slc_new, win_t, win_new, win_new_col, o_c, gates, rb_t, bias_ws)


def _rows_last(a):
    n = a.ndim
    return a.transpose(*range(n - 4), n - 3, n - 2, n - 1, n - 4)


def _rows_first(a):
    n = a.ndim
    return a.transpose(*range(n - 4), n - 1, n - 4, n - 3, n - 2)


def _nsa_sample(rel_bias, pa, rs_new, rw_new, cache_cmp, cache_slc, win_cache, page_table,
                bias, offs, cmp_w, const_row, p_len):
    wcat, w2bd = cmp_w
    nb, n_pages = page_table.shape
    n_half = p_len // CMP_STRIDE
    n_slc = p_len // SLC_BLOCK + 1
    n_buf = win_cache.shape[1]
    q3 = pa[:, :A_WIDTH].reshape(nb, A_HEADS, A_HD)
    gates = pa[:, 2 * A_WIDTH:2 * A_WIDTH + 3 * A_HEADS].reshape(nb, 3, A_HEADS, 1)
    bias_cs = bias[:, offs[6]:offs[7]].reshape(A_HEADS, -1)[:, :n_half]
    bias_ws = bias[:, offs[7]:offs[8]].reshape(A_HEADS, -1)[:, :n_buf]
    cov = jnp.asarray(_cover_np(n_half, n_half - 1, SEL_LANES, n_slc))
    o_c, picks = _cmp_sample(page_table, _rows_last(cache_cmp), q3, wcat, const_row, w2bd, bias_cs, cov, p_len)
    idx = picks[:, :A_KV, :N_SEL]
    ha, wbuf = _slc_sample(idx, page_table, _rows_last(cache_slc), q3, rs_new.reshape(nb, 1, ROW_W),
                           _rows_last(win_cache), rw_new.reshape(nb, 1, ROW_W), rw_new.reshape(nb, ROW_W, 1),
                           o_c, gates, rel_bias.T, bias_ws, p_len)
    return ha.reshape(nb, A_WIDTH), idx, wbuf


def kernel(x_prompt, x_sample, cache_cmp_kv, cache_slc_kv, cache_win_kv, state_mlstm_C, state_mlstm_n, state_mlstm_m, page_table, c_prompt, c_sample, rel_bias, w_ada, b_ada, w_in, b_in, m_norm_g, cmp_pe, cmp_w1, cmp_b1, cmp_w2, w_out, b_out, ln_g, ln_b):
    B, T, _ = x_prompt.shape
    NB = x_sample.shape[0]
    n_pages = page_table.shape[1]
    p_len = n_pages * PAGE_ROWS
    depth = w_in.shape[0]
    assert depth == 1 and x_sample.shape[1] == 1 and cache_win_kv.shape[2] == WINDOW
    ids, offs = _static_ids(p_len)
    bias = _bias_tables(rel_bias, ids)
    x_p = x_prompt.reshape(B * T, D_MODEL)
    x_s = x_sample.reshape(NB, D_MODEL)
    l = 0
    n_mod = -(-(B + NB) // SUBLANE) * SUBLANE
    c_all = jnp.concatenate([c_prompt, c_sample, jnp.zeros((n_mod - B - NB, D_MODEL), F32)])
    shift, scale, gate = jnp.split(_adaln_mod(c_all, w_ada[l], b_ada[l]), 3, axis=-1)
    packed = _pack_in_proj(w_in[l], b_in[l])
    cmp_w = _pack_compress(cmp_w1[l], cmp_w2[l])
    const_row = _compress_const(cmp_pe[l], cmp_w1[l], cmp_b1[l])
    w_out_b = w_out[l].astype(BF16)
    vecs = (b_out[l].reshape(1, -1), ln_g[l].reshape(1, -1), ln_b[l].reshape(1, -1))
    pm, pa, rc, rs, rw, ct, st, wt = _project(x_p, shift[:B, None], scale[:B, None], packed, B, 256)
    mix_m, c_p, n_p, m_p = _mlstm_prompt(pm, m_norm_g[l], B, T)
    o_c, o_s, o_w, _ = _nsa_prompt(rel_bias, pa, rc, rs, rw, st, wt, bias, offs, cmp_w, const_row, B, T)
    y_p = _out_prompt(x_p, mix_m, o_c, o_s, o_w, pa, gate[:B, None], w_out_b, *vecs, B, T)
    pm_s, pa_s, _, rs_s, rw_s, ct_s, st_s, wt_s = _project(x_s, shift[B:B + NB], scale[B:B + NB], packed, 1, NB)
    mix_s, c_s, n_s, m_s = _mlstm_sample(pm_s, m_norm_g[l], state_mlstm_C[l], state_mlstm_n[l], state_mlstm_m[l])
    ha_s, _, wbuf_s = _nsa_sample(rel_bias, pa_s, rs_s, rw_s, cache_cmp_kv[l], cache_slc_kv[l],
                                  cache_win_kv[l], page_table, bias, offs, cmp_w, const_row, p_len)
    y_s = _out_sample(x_s, mix_s.reshape(NB, M_WIDTH), ha_s, pa_s, gate[B:B + NB], w_out_b, *vecs)

    def kv_prompt(a):
        return _rows_first(a.reshape(1, B, A_KV, 2, A_HD, a.shape[-1]))

    def kv_sample(a):
        return a.reshape(1, 1, A_KV, 2, A_HD, NB).transpose(0, 5, 1, 2, 3, 4)

    return (y_p.reshape(B, T, D_MODEL), y_s.reshape(NB, 1, D_MODEL),
            kv_prompt(ct), kv_sample(ct_s), kv_prompt(st), kv_sample(st_s),
            kv_prompt(wt[:, :, T - WINDOW:]), _rows_first(wbuf_s)[None],
            c_p[None], c_s[None], n_p[None], n_s[None], m_p[None, :, :, 0], m_s[None, :, :, 0])
```

```python
import functools
import math

import numpy as np
import jax
import jax.numpy as jnp
from jax import lax
from jax.experimental import pallas as pl
from jax.experimental.pallas import tpu as pltpu

F32 = jnp.float32
BF16 = jnp.bfloat16
HIGHEST = lax.Precision.HIGHEST

D_MODEL = 1024
M_HEADS = 4
M_HD = 128
M_WIDTH = M_HEADS * M_HD
M_CHUNK = 128
A_HEADS = 8
A_HD = 64
A_KV = 2
A_GROUP = A_HEADS // A_KV
A_WIDTH = A_HEADS * A_HD
A_KVW = A_KV * A_HD
ROW_W = 2 * A_KVW
CMP_LEN = 32
CMP_STRIDE = 16
SLC_BLOCK = 64
N_SEL = 16
WINDOW = 512
N_BUCKETS = 32
MAX_EXACT = N_BUCKETS // 2
MAX_DIST = 128
FAR_BUCKET = N_BUCKETS - 1
LN_EPS = 1e-5
ATT_SCALE = A_HD ** -0.5
DEPTH = 1
DEEPNORM_ALPHA = (2.0 * DEPTH) ** 0.25
IN_SPLITS = (M_WIDTH,) * 5 + (M_HEADS, M_HEADS) + (A_WIDTH,) + (A_KVW,) * 6 + (3 * A_HEADS, A_WIDTH)

LANE = 128
SUBLANE = 8
TQ = 128
NEG = -1e30
LOG2E = math.log2(math.e)
MASKED_ID = N_BUCKETS
VMEM_LIMIT = 56 * 1024 * 1024

PM_W = 5 * M_WIDTH + LANE
PA_W = 2 * A_WIDTH + LANE
PW_TOTAL = PM_W + PA_W + 3 * ROW_W


def _cparams(*sem):
    return pltpu.CompilerParams(dimension_semantics=sem, vmem_limit_bytes=VMEM_LIMIT)


def _nt(a, b):
    return lax.dot_general(a, b, (((1,), (1,)), ((), ())), preferred_element_type=F32)


def _dot(a, b, precision=None):
    return jnp.dot(a, b, preferred_element_type=F32, precision=precision)


def _log_sigmoid(x):
    return jnp.minimum(x, 0.0) - jnp.log(1.0 + jnp.exp(-jnp.abs(x)))


def _silu(x):
    return x * jax.nn.sigmoid(x)


def _gelu_tanh(x):
    return 0.5 * x * (1.0 + jnp.tanh(math.sqrt(2.0 / math.pi) * (x + 0.044715 * (x * x * x))))


def _ln_rows(x):
    mu = jnp.mean(x, axis=-1, keepdims=True)
    xc = x - mu
    var = jnp.mean(xc * xc, axis=-1, keepdims=True)
    return xc * lax.rsqrt(var + LN_EPS)


def _bucket_np(dist):
    dist = np.asarray(dist, np.int64)
    n = np.maximum(dist, 0)
    nf = np.maximum(n, 1).astype(np.float32)
    large = MAX_EXACT + (np.log(nf / np.float32(MAX_EXACT)) / np.float32(math.log(MAX_DIST / MAX_EXACT))
                         * np.float32(N_BUCKETS - MAX_EXACT)).astype(np.int32)
    large = np.minimum(large, N_BUCKETS - 1)
    b = np.where(n < MAX_EXACT, n, large)
    return np.where(dist < 0, MASKED_ID, b).astype(np.int32)


def _bucket_dyn(dist):
    n = jnp.maximum(dist, 0)
    nf = jnp.maximum(n, 1).astype(F32)
    large = MAX_EXACT + jnp.floor(jnp.log(nf / MAX_EXACT) / math.log(MAX_DIST / MAX_EXACT)
                                  * (N_BUCKETS - MAX_EXACT))
    large = jnp.minimum(large, float(N_BUCKETS - 1))
    return jnp.where(n < MAX_EXACT, n.astype(F32), large)


def _mod_kernel(c_ref, w_ref, b_ref, o_ref):
    a = _silu(c_ref[...])
    o_ref[...] = _dot(a, w_ref[...]) + b_ref[...]


def _adaln_mod(c, w_ada, b_ada):
    rows = c.shape[0]
    n3 = w_ada.shape[1]
    tn = D_MODEL
    return pl.pallas_call(
        _mod_kernel,
        grid=(n3 // tn,),
        in_specs=[pl.BlockSpec((rows, D_MODEL), lambda j: (0, 0)),
                  pl.BlockSpec((D_MODEL, tn), lambda j: (0, j)),
                  pl.BlockSpec((1, tn), lambda j: (0, j))],
        out_specs=pl.BlockSpec((rows, tn), lambda j: (0, j)),
        out_shape=jax.ShapeDtypeStruct((rows, n3), F32),
        compiler_params=_cparams("arbitrary"),
        name="adaln_mod",
    )(c, w_ada, b_ada.reshape(1, n3))


def _bias_kernel(rb_ref, ids_ref, o_ref, *, n_groups):
    def body(i, carry):
        r0 = pl.multiple_of(i * SUBLANE, SUBLANE)
        ids = ids_ref[pl.ds(r0, SUBLANE), :]
        for h in range(A_HEADS):
            acc = jnp.full((SUBLANE, LANE), NEG, F32)
            for b in range(N_BUCKETS):
                acc = jnp.where(ids == b, rb_ref[b, h], acc)
            o_ref[h, pl.ds(r0, SUBLANE), :] = acc
        return carry

    lax.fori_loop(0, n_groups, body, 0)


def _bias_tables(rel_bias, ids):
    rows = ids.shape[0]
    return pl.pallas_call(
        functools.partial(_bias_kernel, n_groups=rows // SUBLANE),
        in_specs=[pl.BlockSpec(memory_space=pltpu.SMEM),
                  pl.BlockSpec((rows, LANE), lambda: (0, 0))],
        out_specs=pl.BlockSpec((A_HEADS, rows, LANE), lambda: (0, 0, 0)),
        out_shape=jax.ShapeDtypeStruct((A_HEADS, rows, LANE), F32),
        name="bias_tables",
    )(rel_bias, jnp.asarray(ids))


def _pack_in_proj(w_in, b_in, dtype):
    offs = np.cumsum((0,) + IN_SPLITS)
    names = ("mq", "mk", "mv", "mo", "mz", "mi", "mf", "aq", "ck", "cv", "sk", "sv", "wk", "wv", "ga", "za")
    sl = {n: (int(offs[i]), int(offs[i + 1])) for i, n in enumerate(names)}

    def cols(a, name, lo=None, hi=None):
        s, e = sl[name]
        if lo is not None:
            s, e = s + lo, s + hi
        return a[..., s:e]

    def rows_of(a, kn, vn):
        return [cols(a, kn, 0, A_HD), cols(a, vn, 0, A_HD), cols(a, kn, A_HD, 2 * A_HD), cols(a, vn, A_HD, 2 * A_HD)]

    def pack(a):
        def zeros(n):
            return jnp.zeros(a.shape[:-1] + (n,), a.dtype)
        parts = [cols(a, n) for n in ("mq", "mk", "mv", "mo", "mz")]
        parts += [cols(a, "mi"), cols(a, "mf"), zeros(LANE - 2 * M_HEADS)]
        parts += [cols(a, "aq"), cols(a, "za"), cols(a, "ga"), zeros(LANE - 3 * A_HEADS)]
        parts += rows_of(a, "ck", "cv") + rows_of(a, "sk", "sv") + rows_of(a, "wk", "wv")
        return jnp.concatenate(parts, axis=-1)

    w = pack(w_in)
    b = pack(b_in.reshape(1, -1))
    wt = w[:, PM_W + PA_W:].T
    bt = b[:, PM_W + PA_W:].reshape(-1, 1)
    return w.astype(dtype), b, wt.astype(dtype), bt


def _proj_kernel(x_ref, sh_ref, sc_ref, w_ref, b_ref, wt_ref, bt_ref,
                 om_ref, oa_ref, oc_ref, os_ref, ow_ref, oct_ref, ost_ref, owt_ref):
    h = _ln_rows(x_ref[...]) * (1.0 + sc_ref[...]) + sh_ref[...]
    hb = h.astype(w_ref.dtype)
    precision = HIGHEST if w_ref.dtype == F32 else None
    lo = 0
    for o_ref in (om_ref, oa_ref, oc_ref, os_ref, ow_ref):
        n = o_ref.shape[-1]
        o_ref[...] = _dot(hb, w_ref[:, lo:lo + n], precision) + b_ref[:, lo:lo + n]
        lo += n
    t = lax.dot_general(wt_ref[...], hb, (((1,), (1,)), ((), ())), preferred_element_type=F32,
                        precision=precision) + bt_ref[...]
    for i, o_ref in enumerate((oct_ref, ost_ref, owt_ref)):
        o_ref[0] = t[i * ROW_W:(i + 1) * ROW_W]


def _project(x, shift, scale, packed, groups, tm):
    w, b, wt, bt = packed
    rows = x.shape[0]
    per = rows // groups // tm
    if shift.ndim == 3:
        mod_spec = pl.BlockSpec((None, 1, D_MODEL), lambda i: (i // per, 0, 0))
    else:
        mod_spec = pl.BlockSpec((tm, D_MODEL), lambda i: (i, 0))
    widths = (PM_W, PA_W, ROW_W, ROW_W, ROW_W)
    return pl.pallas_call(
        _proj_kernel,
        grid=(rows // tm,),
        in_specs=[pl.BlockSpec((tm, D_MODEL), lambda i: (i, 0)), mod_spec, mod_spec,
                  pl.BlockSpec((D_MODEL, PW_TOTAL), lambda i: (0, 0)),
                  pl.BlockSpec((1, PW_TOTAL), lambda i: (0, 0)),
                  pl.BlockSpec((3 * ROW_W, D_MODEL), lambda i: (0, 0)),
                  pl.BlockSpec((3 * ROW_W, 1), lambda i: (0, 0))],
        out_specs=[pl.BlockSpec((tm, n), lambda i: (i, 0)) for n in widths]
                  + [pl.BlockSpec((1, ROW_W, tm), lambda i: (i // per, 0, i % per))] * 3,
        out_shape=[jax.ShapeDtypeStruct((rows, n), F32) for n in widths]
                  + [jax.ShapeDtypeStruct((groups, ROW_W, rows // groups), F32)] * 3,
        compiler_params=_cparams("arbitrary"),
        name="in_proj",
    )(x, shift, scale, w, b, wt, bt)


def _mlstm_head_out(h, o_pre, z_pre, g_row):
    return jax.nn.sigmoid(o_pre) * (_ln_rows(h) * g_row) * _silu(z_pre)


def _mlstm_prompt_kernel(q_ref, k_ref, v_ref, o_ref, z_ref, g_ref, ng_ref, mix_ref, c_ref, n_ref, m_ref):
    L = M_CHUNK

    @pl.when(pl.program_id(1) == 0)
    def _():
        c_ref[...] = jnp.zeros_like(c_ref)
        n_ref[...] = jnp.zeros_like(n_ref)
        m_ref[...] = jnp.zeros_like(m_ref)

    gates = g_ref[...]
    gates_t = gates.T
    row = lax.broadcasted_iota(jnp.int32, (L, L), 0)
    col = lax.broadcasted_iota(jnp.int32, (L, L), 1)
    tril = col <= row
    cum = _dot(tril.astype(F32), _log_sigmoid(gates), HIGHEST)
    cum_t = _dot(_log_sigmoid(gates_t), (row <= col).astype(F32), HIGHEST)
    for h in range(M_HEADS):
        hs = slice(h * M_HD, (h + 1) * M_HD)
        b_col = cum[:, M_HEADS + h:M_HEADS + h + 1]
        b_row = cum_t[M_HEADS + h:M_HEADS + h + 1, :]
        ig_col = gates[:, h:h + 1]
        ig_row = gates_t[h:h + 1, :]
        m_prev = m_ref[0, h:h + 1, 0:1]
        c_prev = c_ref[0, h]
        n_prev = n_ref[0, h:h + 1, :]
        d = jnp.where(tril, b_col - b_row + ig_row, NEG)
        inter = b_col + m_prev
        m_t = jnp.maximum(inter, jnp.max(d, axis=1, keepdims=True))
        w_inter = jnp.exp(inter - m_t)
        q = q_ref[:, hs]
        ks = k_ref[:, hs] * (M_HD ** -0.5)
        v = v_ref[:, hs]
        qb, kb, vb = q.astype(BF16), ks.astype(BF16), v.astype(BF16)
        qk = _nt(qb, kb) * jnp.exp(d - m_t)
        num = w_inter * _dot(qb, c_prev.astype(BF16)) + _dot(qk.astype(BF16), vb)
        den = w_inter * jnp.sum(q * n_prev, axis=1, keepdims=True) + jnp.sum(qk, axis=1, keepdims=True)
        hh = num / jnp.maximum(jnp.abs(den), jnp.exp(-m_t))
        m_new = m_t[L - 1:L, :]
        b_last = b_col[L - 1:L, :]
        w_c = jnp.exp(b_last + m_prev - m_new)
        w_s = jnp.exp(b_last - b_col + ig_col - m_new)
        kw = ks * w_s
        c_ref[0, h] = w_c * c_prev + _dot(kw.T.astype(BF16), vb)
        n_ref[0, h:h + 1, :] = w_c * n_prev + jnp.sum(kw, axis=0, keepdims=True)
        m_ref[0, h:h + 1, :] = jnp.broadcast_to(m_new, (1, M_HD))
        mix_ref[:, hs] = _mlstm_head_out(hh, o_ref[:, hs], z_ref[:, hs], ng_ref[:, hs])


def _mlstm_prompt(pm, norm_g, batch, seq):
    nc = seq // M_CHUNK
    rows = batch * seq

    def col_spec(j, width=M_WIDTH):
        return pl.BlockSpec((M_CHUNK, width), lambda b, c: (b * nc + c, j))

    return pl.pallas_call(
        _mlstm_prompt_kernel,
        grid=(batch, nc),
        in_specs=[col_spec(0), col_spec(1), col_spec(2), col_spec(3), col_spec(4),
                  pl.BlockSpec((M_CHUNK, LANE), lambda b, c: (b * nc + c, 5 * M_WIDTH // LANE)),
                  pl.BlockSpec((1, M_WIDTH), lambda b, c: (0, 0))],
        out_specs=[pl.BlockSpec((M_CHUNK, M_WIDTH), lambda b, c: (b * nc + c, 0)),
                   pl.BlockSpec((1, M_HEADS, M_HD, M_HD), lambda b, c: (b, 0, 0, 0)),
                   pl.BlockSpec((1, M_HEADS, M_HD), lambda b, c: (b, 0, 0)),
                   pl.BlockSpec((1, M_HEADS, M_HD), lambda b, c: (b, 0, 0))],
        out_shape=[jax.ShapeDtypeStruct((rows, M_WIDTH), F32),
                   jax.ShapeDtypeStruct((batch, M_HEADS, M_HD, M_HD), F32),
                   jax.ShapeDtypeStruct((batch, M_HEADS, M_HD), F32),
                   jax.ShapeDtypeStruct((batch, M_HEADS, M_HD), F32)],
        compiler_params=_cparams("arbitrary", "arbitrary"),
        name="mlstm_prompt",
    )(pm, pm, pm, pm, pm, pm, norm_g.reshape(1, M_WIDTH))


MS_BB = 16
MS_ROWS = PM_W // LANE


def _mlstm_sample_kernel(x_ref, c_ref, n_ref, m_ref, ng_ref, mix_ref, co_ref, no_ref, mo_ref):
    qk_cols = x_ref[:, 0:2 * M_HEADS, :].reshape(MS_BB * 2 * M_HEADS, M_HD).T
    for bi in range(MS_BB):
        xb = x_ref[bi]
        for h in range(M_HEADS):
            cb = bi * 2 * M_HEADS + h
            q_col = qk_cols[:, cb:cb + 1]
            k_col = qk_cols[:, cb + M_HEADS:cb + M_HEADS + 1] * (M_HD ** -0.5)
            q_row = xb[h:h + 1, :]
            k_row = xb[M_HEADS + h:M_HEADS + h + 1, :] * (M_HD ** -0.5)
            v_row = xb[2 * M_HEADS + h:2 * M_HEADS + h + 1, :]
            o_row = xb[3 * M_HEADS + h:3 * M_HEADS + h + 1, :]
            z_row = xb[4 * M_HEADS + h:4 * M_HEADS + h + 1, :]
            ig = xb[5 * M_HEADS:5 * M_HEADS + 1, h:h + 1]
            lf = _log_sigmoid(xb[5 * M_HEADS:5 * M_HEADS + 1, M_HEADS + h:M_HEADS + h + 1])
            c_prev = c_ref[bi, h]
            n_prev = n_ref[bi, h:h + 1, :]
            m_prev = m_ref[bi, h:h + 1, 0:1]
            inter = lf + m_prev
            m_t = jnp.maximum(inter, ig)
            w_inter = jnp.exp(inter - m_t)
            w_s = jnp.exp(ig - m_t)
            qk = jnp.sum(q_row * k_row, axis=1, keepdims=True) * w_s
            q_c = jnp.sum(q_col * c_prev, axis=0, keepdims=True)
            num = w_inter * q_c + qk * v_row
            den = w_inter * jnp.sum(q_row * n_prev, axis=1, keepdims=True) + qk
            hh = num / jnp.maximum(jnp.abs(den), jnp.exp(-m_t))
            co_ref[bi, h] = w_inter * c_prev + (w_s * k_col) * v_row
            no_ref[bi, h:h + 1, :] = w_inter * n_prev + w_s * k_row
            mo_ref[bi, h:h + 1, :] = jnp.broadcast_to(m_t, (1, M_HD))
            hs = slice(h * M_HD, (h + 1) * M_HD)
            mix_ref[bi, h:h + 1, :] = _mlstm_head_out(hh, o_row, z_row, ng_ref[:, hs])


def _mlstm_sample(pm, norm_g, c0, n0, m0):
    nb = pm.shape[0]
    x3 = pm.reshape(nb, MS_ROWS, LANE)
    m_b = jnp.broadcast_to(m0[:, :, None], (nb, M_HEADS, M_HD))
    bb = MS_BB
    state_specs = [pl.BlockSpec((bb, M_HEADS, M_HD, M_HD), lambda i: (i, 0, 0, 0)),
                   pl.BlockSpec((bb, M_HEADS, M_HD), lambda i: (i, 0, 0)),
                   pl.BlockSpec((bb, M_HEADS, M_HD), lambda i: (i, 0, 0))]
    return pl.pallas_call(
        _mlstm_sample_kernel,
        grid=(nb // bb,),
        in_specs=[pl.BlockSpec((bb, MS_ROWS, LANE), lambda i: (i, 0, 0))] + state_specs
                 + [pl.BlockSpec((1, M_WIDTH), lambda i: (0, 0))],
        out_specs=[pl.BlockSpec((bb, M_HEADS, M_HD), lambda i: (i, 0, 0))] + state_specs,
        out_shape=[jax.ShapeDtypeStruct((nb, M_HEADS, M_HD), F32),
                   jax.ShapeDtypeStruct((nb, M_HEADS, M_HD, M_HD), F32),
                   jax.ShapeDtypeStruct((nb, M_HEADS, M_HD), F32),
                   jax.ShapeDtypeStruct((nb, M_HEADS, M_HD), F32)],
        compiler_params=_cparams("arbitrary"),
        name="mlstm_sample",
    )(x3, c0, n0, m_b, norm_g.reshape(1, M_WIDTH))


KVROW_W = 2 * A_HD
HALF_W = 2 * KVROW_W


def _pack_compress(w1, w2):
    def block_diag(k, v):
        z = jnp.zeros_like(k)
        return jnp.concatenate([jnp.concatenate([k, z], axis=-1), jnp.concatenate([z, v], axis=-1)], axis=-2)

    wbd = block_diag(w1[0], w1[1])
    wcat = jnp.concatenate([wbd[:CMP_STRIDE], wbd[CMP_STRIDE:]], axis=-1)
    return wcat.reshape(CMP_STRIDE * KVROW_W, HALF_W).astype(BF16), block_diag(w2[0], w2[1]).astype(BF16)


def _cmp_const_kernel(pe_ref, w_ref, b_ref, o_ref):
    for c in range(2):
        o_ref[c] = _dot(pe_ref[c], w_ref[c], HIGHEST) + b_ref[c]


def _compress_const(pe, w1, b1):
    k = CMP_LEN * A_HD
    pe8 = jnp.broadcast_to(pe.reshape(2, 1, k), (2, SUBLANE, k))
    out = pl.pallas_call(
        _cmp_const_kernel,
        out_shape=jax.ShapeDtypeStruct((2, SUBLANE, A_HD), F32),
        name="compress_const",
    )(pe8, w1.reshape(2, k, A_HD), b1.reshape(2, 1, A_HD))
    return jnp.concatenate([out[0, 0:1], out[1, 0:1]], axis=-1)


def _compress_halves(load_rows, wcat_ref, const_ref, w2_ref, n_half):
    halves = jnp.concatenate(
        [jnp.concatenate([load_rows(p, kv) for kv in range(A_KV)], axis=0).astype(BF16) for p in range(CMP_STRIDE)],
        axis=1)
    acc = _dot(halves, wcat_ref[...])
    pre = acc[:, :KVROW_W] + pltpu.roll(acc[:, KVROW_W:], A_KV * n_half - 1, 0) + const_ref[...]
    return _dot(_gelu_tanh(pre).astype(BF16), w2_ref[...])


def _compress_prompt_kernel(x0_ref, x1_ref, wcat_ref, const_ref, w2_ref, kk_ref, kvt_ref, *, n_half):
    x_refs = (x0_ref, x1_ref)
    kc = _compress_halves(lambda p, kv: x_refs[kv][pl.ds(p, n_half, stride=CMP_STRIDE), :],
                          wcat_ref, const_ref, w2_ref, n_half)
    kct = kc.T
    for kv in range(A_KV):
        kk_ref[0, kv] = kc[kv * n_half:(kv + 1) * n_half, 0:A_HD]
        kvt_ref[0, kv] = kct[A_HD:, kv * n_half:(kv + 1) * n_half]


def _compress_prompt(rows, wcat, const_row, w2bd, batch, seq):
    n_half = seq // CMP_STRIDE
    return pl.pallas_call(
        functools.partial(_compress_prompt_kernel, n_half=n_half),
        grid=(batch,),
        in_specs=[pl.BlockSpec((seq, KVROW_W), lambda b: (b, 0)),
                  pl.BlockSpec((seq, KVROW_W), lambda b: (b, 1)),
                  pl.BlockSpec((CMP_STRIDE * KVROW_W, HALF_W), lambda b: (0, 0)),
                  pl.BlockSpec((1, KVROW_W), lambda b: (0, 0)),
                  pl.BlockSpec((KVROW_W, KVROW_W), lambda b: (0, 0))],
        out_specs=[pl.BlockSpec((1, A_KV, n_half, A_HD), lambda b: (b, 0, 0, 0)),
                   pl.BlockSpec((1, A_KV, A_HD, n_half), lambda b: (b, 0, 0, 0))],
        out_shape=[jax.ShapeDtypeStruct((batch, A_KV, n_half, A_HD), F32),
                   jax.ShapeDtypeStruct((batch, A_KV, A_HD, n_half), F32)],
        compiler_params=_cparams("arbitrary"),
        name="compress_prompt",
    )(rows, rows, wcat, const_row, w2bd)


CMP_PAT = 16


def _static_ids(p_len):
    i = np.arange(TQ)[None, :]
    c = np.arange(CMP_PAT)[:, None]
    cmp_a = _bucket_np(i + (TQ - (CMP_LEN - 1)) - CMP_STRIDE * c)
    cmp_b = _bucket_np(i - CMP_STRIDE * c - (CMP_LEN - 1))
    r = np.arange(TQ)[:, None]
    slc_diag = _bucket_np(i - r)
    slc_sub = _bucket_np(TQ + i - r)
    slc_far = np.full((TQ, TQ), FAR_BUCKET, np.int32)
    rw = np.arange(WINDOW + TQ)[:, None]
    dw = WINDOW + i - rw
    win = np.where(dw > WINDOW, MASKED_ID, _bucket_np(dw))
    n_half = p_len // CMP_STRIDE
    n = np.arange(n_half)
    cs = _bucket_np(p_len - (CMP_STRIDE * n + CMP_LEN - 1))
    cs[n_half - 1] = MASKED_ID
    cs_rows = -(-n_half // LANE)
    cs_pad = np.full((cs_rows * LANE,), MASKED_ID, np.int32)
    cs_pad[:n_half] = cs
    ws = _bucket_np(WINDOW - np.arange(WINDOW))
    parts = [cmp_a, cmp_b, slc_diag, slc_sub, slc_far, win, cs_pad.reshape(cs_rows, LANE),
             ws.reshape(WINDOW // LANE, LANE)]
    offs = np.cumsum([0] + [p.shape[0] for p in parts])
    total = -(-int(offs[-1]) // SUBLANE) * SUBLANE
    ids = np.full((total, LANE), MASKED_ID, np.int32)
    ids[:offs[-1]] = np.concatenate(parts, axis=0)
    return ids, [int(o) for o in offs]


def _cover_np(n_cmp_rows, n_cmp, n_slc_rows, n_slc):
    cs = np.arange(n_cmp_rows)[:, None] * CMP_STRIDE
    ss = np.arange(n_slc_rows)[None, :] * SLC_BLOCK
    cov = (cs <= ss + SLC_BLOCK - 1) & (cs + CMP_LEN - 1 >= ss)
    cov &= (np.arange(n_cmp_rows)[:, None] < n_cmp) & (np.arange(n_slc_rows)[None, :] < n_slc)
    return cov.astype(np.float32)


def _softmax_keys_on_rows(s):
    m = jnp.max(s, axis=0, keepdims=True)
    m = jnp.where(m > 0.5 * NEG, m, 0.0)
    e = jnp.exp(s - m)
    tot = jnp.sum(e, axis=0, keepdims=True)
    return e / jnp.where(tot > 0.0, tot, 1.0)


def _softmax_keys_on_lanes(s, s_new=None):
    m = jnp.max(s, axis=1, keepdims=True)
    if s_new is not None:
        m = jnp.maximum(m, s_new)
    m = jnp.where(m > 0.5 * NEG, m, 0.0)
    e = jnp.exp(s - m)
    tot = jnp.sum(e, axis=1, keepdims=True)
    e_new = None
    if s_new is not None:
        e_new = jnp.exp(s_new - m)
        tot = tot + e_new
    return e, e_new, 1.0 / jnp.where(tot > 0.0, tot, 1.0)


def _cmp_attend_kernel(rb_ref, q_ref, kk_ref, kvt_ref, pt_ref, cov_ref, o_ref, sel_ref, bscr, *, nc, ns):
    k = pl.program_id(1)
    start = pl.multiple_of(jnp.maximum(SUBLANE * k - SUBLANE, 0), SUBLANE)
    variant = jnp.where(k == 0, 1, 0)
    row = lax.broadcasted_iota(jnp.int32, (nc, TQ), 0)
    t = k * TQ + lax.broadcasted_iota(jnp.int32, (ns, TQ), 1)
    blk = lax.broadcasted_iota(jnp.int32, (ns, TQ), 0)
    cur = t // SLC_BLOCK
    valid = blk * SLC_BLOCK <= t
    forced = (blk == 0) | (blk == cur) | (blk == cur - 1)
    for kv in range(A_KV):
        kk = kk_ref[0, kv].astype(BF16)
        kvt = kvt_ref[0, kv].astype(BF16)
        imp = jnp.zeros((nc, TQ), F32)
        for g in range(A_GROUP):
            h = kv * A_GROUP + g
            hs = slice(h * A_HD, (h + 1) * A_HD)
            bscr[...] = jnp.where(row < start, rb_ref[FAR_BUCKET, h], NEG)
            bscr[pl.ds(start, CMP_PAT), :] = pt_ref[variant, h]
            s = _nt(kk, q_ref[:, hs].astype(BF16)) * ATT_SCALE + bscr[...]
            p = _softmax_keys_on_rows(s)
            o_ref[0, hs, :] = _dot(kvt, p.astype(BF16))
            imp = imp + p
        score = _dot(cov_ref[...], imp, HIGHEST)
        sc = jnp.where(forced, jnp.inf, jnp.where(valid, score, -jnp.inf))
        cnt = jnp.zeros((ns, TQ), jnp.int32)
        for j in range(ns):
            r = sc[j:j + 1, :]
            before = (r > sc) | ((r == sc) & (blk > j))
            cnt = cnt + before.astype(jnp.int32)
        sel_ref[0, kv] = jnp.where(cnt < N_SEL, 0.0, NEG)


def _cmp_attend(rel_bias, pa, kk, kvt, pat, cov_t, batch, seq):
    nq = seq // TQ
    nc = seq // CMP_STRIDE
    ns = seq // SLC_BLOCK
    return pl.pallas_call(
        functools.partial(_cmp_attend_kernel, nc=nc, ns=ns),
        grid=(batch, nq),
        in_specs=[pl.BlockSpec(memory_space=pltpu.SMEM),
                  pl.BlockSpec((TQ, A_WIDTH), lambda b, k: (b * nq + k, 0)),
                  pl.BlockSpec((1, A_KV, nc, A_HD), lambda b, k: (b, 0, 0, 0)),
                  pl.BlockSpec((1, A_KV, A_HD, nc), lambda b, k: (b, 0, 0, 0)),
                  pl.BlockSpec((2, A_HEADS, CMP_PAT, TQ), lambda b, k: (0, 0, 0, 0)),
                  pl.BlockSpec((ns, nc), lambda b, k: (0, 0))],
        out_specs=[pl.BlockSpec((1, A_WIDTH, TQ), lambda b, k: (b, 0, k)),
                   pl.BlockSpec((1, A_KV, ns, TQ), lambda b, k: (b, 0, 0, k))],
        out_shape=[jax.ShapeDtypeStruct((batch, A_WIDTH, seq), F32),
                   jax.ShapeDtypeStruct((batch, A_KV, ns, seq), F32)],
        scratch_shapes=[pltpu.VMEM((nc, TQ), F32)],
        compiler_params=_cparams("arbitrary", "arbitrary"),
        name="cmp_attend",
    )(rel_bias, pa, kk, kvt, pat, cov_t)


SLC_CK = 4 * TQ
GROUP_LANES = A_GROUP * TQ
SLC_CLASSES = 4


def _stacked_queries(q_ref, kv, scale=ATT_SCALE):
    heads = [q_ref[:, (kv * A_GROUP + g) * A_HD:(kv * A_GROUP + g + 1) * A_HD] for g in range(A_GROUP)]
    return (jnp.concatenate(heads, axis=0) * scale).astype(BF16)


def _slc_attend_kernel(q_ref, k_ref, vt_ref, sel_ref, tab_ref, o_ref):
    k = pl.program_id(1)
    sub = SLC_CK // TQ
    q4 = [_stacked_queries(q_ref, kv, ATT_SCALE * LOG2E) for kv in range(A_KV)]
    upper = lax.broadcasted_iota(jnp.int32, (TQ, GROUP_LANES), 0) < SLC_BLOCK

    def body(j, carry):
        j0 = pl.multiple_of(j * SLC_CK, SLC_CK)
        out = []
        for kv in range(A_KV):
            m_run, l_run, acc = carry[kv]
            kj = k_ref[pl.ds(j0, SLC_CK), kv * LANE:kv * LANE + A_HD].astype(BF16)
            vt = vt_ref[0, kv * LANE + A_HD:(kv + 1) * LANE, pl.ds(j0, SLC_CK)].astype(BF16)
            s_all = _nt(kj, q4[kv])
            parts = []
            for u in range(sub):
                jj = j * sub + u
                cls = jnp.where(jj > k, SLC_CLASSES - 1, jnp.minimum(k - jj, 2))
                sel0 = jnp.concatenate([sel_ref[0, kv, pl.ds(2 * jj, 1), :]] * A_GROUP, axis=1)
                sel1 = jnp.concatenate([sel_ref[0, kv, pl.ds(2 * jj + 1, 1), :]] * A_GROUP, axis=1)
                parts.append(s_all[u * TQ:(u + 1) * TQ] + tab_ref[cls, kv] + jnp.where(upper, sel0, sel1))
            s = jnp.concatenate(parts, axis=0)
            m_new = jnp.maximum(m_run, jnp.max(s, axis=0, keepdims=True))
            alpha = jnp.exp2(m_run - m_new)
            p = jnp.exp2(s - m_new)
            l_new = alpha * l_run + jnp.sum(p, axis=0, keepdims=True)
            out.append((m_new, l_new, alpha * acc + _dot(vt, p.astype(BF16))))
        return tuple(out)

    init = tuple((jnp.full((1, GROUP_LANES), NEG, F32), jnp.zeros((1, GROUP_LANES), F32),
                  jnp.zeros((A_HD, GROUP_LANES), F32)) for _ in range(A_KV))
    res = lax.fori_loop(0, (k + sub) // sub, body, init)
    for kv in range(A_KV):
        _, l_run, acc = res[kv]
        o = acc / l_run
        for g in range(A_GROUP):
            h = kv * A_GROUP + g
            o_ref[0, h * A_HD:(h + 1) * A_HD, :] = o[:, g * TQ:(g + 1) * TQ]


def _slc_attend(pa, rows, rows_t, sel, tab, batch, seq):
    nq = seq // TQ
    ns = seq // SLC_BLOCK
    return pl.pallas_call(
        _slc_attend_kernel,
        grid=(batch, nq),
        in_specs=[pl.BlockSpec((TQ, A_WIDTH), lambda b, k: (b * nq + k, 0)),
                  pl.BlockSpec((seq, ROW_W), lambda b, k: (b, 0)),
                  pl.BlockSpec((1, ROW_W, seq), lambda b, k: (b, 0, 0)),
                  pl.BlockSpec((1, A_KV, ns, TQ), lambda b, k: (b, 0, 0, k)),
                  pl.BlockSpec((SLC_CLASSES, A_KV, TQ, GROUP_LANES), lambda b, k: (0, 0, 0, 0))],
        out_specs=pl.BlockSpec((1, A_WIDTH, TQ), lambda b, k: (b, 0, k)),
        out_shape=jax.ShapeDtypeStruct((batch, A_WIDTH, seq), F32),
        compiler_params=_cparams("arbitrary", "arbitrary"),
        name="slc_attend",
    )(pa, rows, rows_t, sel, tab)


WIN_SPAN = WINDOW + TQ


def _win_attend_kernel(q_ref, k_ref, vt_ref, bias_ref, o_ref):
    k = pl.program_id(1)
    r0 = pl.multiple_of(k * TQ, TQ)
    exists = lax.broadcasted_iota(jnp.int32, (WIN_SPAN, GROUP_LANES), 0) + k * TQ >= WINDOW
    for kv in range(A_KV):
        kw = k_ref[0, pl.ds(r0, WIN_SPAN), kv * LANE:kv * LANE + A_HD].astype(BF16)
        vt = vt_ref[0, kv * LANE + A_HD:(kv + 1) * LANE, pl.ds(r0, WIN_SPAN)].astype(BF16)
        s = _nt(kw, _stacked_queries(q_ref, kv)) + bias_ref[kv]
        p = _softmax_keys_on_rows(jnp.where(exists, s, NEG))
        o = _dot(vt, p.astype(BF16))
        for g in range(A_GROUP):
            h = kv * A_GROUP + g
            o_ref[0, h * A_HD:(h + 1) * A_HD, :] = o[:, g * TQ:(g + 1) * TQ]


def _win_attend(pa, rows_pad, rows_t_pad, bias_w, batch, seq):
    nq = seq // TQ
    return pl.pallas_call(
        _win_attend_kernel,
        grid=(batch, nq),
        in_specs=[pl.BlockSpec((TQ, A_WIDTH), lambda b, k: (b * nq + k, 0)),
                  pl.BlockSpec((1, seq + WINDOW, ROW_W), lambda b, k: (b, 0, 0)),
                  pl.BlockSpec((1, ROW_W, seq + WINDOW), lambda b, k: (b, 0, 0)),
                  pl.BlockSpec((A_KV, WIN_SPAN, GROUP_LANES), lambda b, k: (0, 0, 0))],
        out_specs=pl.BlockSpec((1, A_WIDTH, TQ), lambda b, k: (b, 0, k)),
        out_shape=jax.ShapeDtypeStruct((batch, A_WIDTH, seq), F32),
        compiler_params=_cparams("arbitrary", "arbitrary"),
        name="win_attend",
    )(pa, rows_pad, rows_t_pad, bias_w)


def _out_tail(x, mix_m, mix_a, gate, w_ref, b_ref, g_ref, beta_ref):
    y = (_dot(mix_m.astype(BF16), w_ref[:M_WIDTH]) + _dot(mix_a.astype(BF16), w_ref[M_WIDTH:]) + b_ref[...])
    return _ln_rows(DEEPNORM_ALPHA * x + gate * y) * g_ref[...] + beta_ref[...]


def _out_prompt_kernel(x_ref, mm_ref, oc_ref, os_ref, ow_ref, ga_ref, za_ref, gate_ref,
                       w_ref, b_ref, g_ref, beta_ref, y_ref):
    sig = jax.nn.sigmoid(ga_ref[...].T)
    parts = []
    for h in range(A_HEADS):
        hs = slice(h * A_HD, (h + 1) * A_HD)
        parts.append(sig[h:h + 1] * oc_ref[0, hs, :] + sig[A_HEADS + h:A_HEADS + h + 1] * os_ref[0, hs, :]
                     + sig[2 * A_HEADS + h:2 * A_HEADS + h + 1] * ow_ref[0, hs, :])
    ha = jnp.concatenate(parts, axis=0).T
    mix_a = ha * _silu(za_ref[...])
    y_ref[...] = _out_tail(x_ref[...], mm_ref[...], mix_a, gate_ref[...], w_ref, b_ref, g_ref, beta_ref)


def _out_prompt(x, mix_m, o_c, o_s, o_w, pa, gate, w_out, b_out, ln_g, ln_b, batch, seq):
    nq = seq // TQ
    rows = batch * seq
    branch = pl.BlockSpec((1, A_WIDTH, TQ), lambda i: (i // nq, 0, i % nq))
    vec = pl.BlockSpec((1, D_MODEL), lambda i: (0, 0))
    return pl.pallas_call(
        _out_prompt_kernel,
        grid=(rows // TQ,),
        in_specs=[pl.BlockSpec((TQ, D_MODEL), lambda i: (i, 0)),
                  pl.BlockSpec((TQ, M_WIDTH), lambda i: (i, 0)),
                  branch, branch, branch,
                  pl.BlockSpec((TQ, LANE), lambda i: (i, 2 * A_WIDTH // LANE)),
                  pl.BlockSpec((TQ, A_WIDTH), lambda i: (i, 1)),
                  pl.BlockSpec((None, 1, D_MODEL), lambda i: (i // nq, 0, 0)),
                  pl.BlockSpec((D_MODEL, D_MODEL), lambda i: (0, 0)),
                  vec, vec, vec],
        out_specs=pl.BlockSpec((TQ, D_MODEL), lambda i: (i, 0)),
        out_shape=jax.ShapeDtypeStruct((rows, D_MODEL), F32),
        compiler_params=_cparams("arbitrary"),
        name="out_prompt",
    )(x, mix_m, o_c, o_s, o_w, pa, pa, gate, w_out, b_out, ln_g, ln_b)


def _out_sample_kernel(x_ref, mm_ref, ha_ref, za_ref, gate_ref, w_ref, b_ref, g_ref, beta_ref, y_ref):
    mix_a = ha_ref[...] * _silu(za_ref[...])
    y_ref[...] = _out_tail(x_ref[...], mm_ref[...], mix_a, gate_ref[...], w_ref, b_ref, g_ref, beta_ref)


def _out_sample(x, mix_m, ha, pa, gate, w_out, b_out, ln_g, ln_b):
    rows = x.shape[0]
    vec = pl.BlockSpec((1, D_MODEL), lambda i: (0, 0))
    return pl.pallas_call(
        _out_sample_kernel,
        grid=(1,),
        in_specs=[pl.BlockSpec((rows, D_MODEL), lambda i: (0, 0)),
                  pl.BlockSpec((rows, M_WIDTH), lambda i: (0, 0)),
                  pl.BlockSpec((rows, A_WIDTH), lambda i: (0, 0)),
                  pl.BlockSpec((rows, A_WIDTH), lambda i: (0, 1)),
                  pl.BlockSpec((rows, D_MODEL), lambda i: (0, 0)),
                  pl.BlockSpec((D_MODEL, D_MODEL), lambda i: (0, 0)),
                  vec, vec, vec],
        out_specs=pl.BlockSpec((rows, D_MODEL), lambda i: (0, 0)),
        out_shape=jax.ShapeDtypeStruct((rows, D_MODEL), F32),
        compiler_params=_cparams("arbitrary"),
        name="out_sample",
    )(x, mix_m, ha, pa, gate, w_out, b_out, ln_g, ln_b)


def _nsa_prompt(rel_bias, pa, rc, rs, rw, st, wt, bias, offs, cmp_w, const_row, batch, seq):
    wcat, w2bd = cmp_w
    nc = seq // CMP_STRIDE
    ns = seq // SLC_BLOCK
    kk, kvt = _compress_prompt(rc, wcat, const_row, w2bd, batch, seq)
    pat = bias[:, offs[0]:offs[2]].reshape(A_HEADS, 2, CMP_PAT, TQ).transpose(1, 0, 2, 3)
    cov_t = jnp.asarray(_cover_np(nc, nc - 1, ns, ns).T)
    o_c, sel = _cmp_attend(rel_bias, pa, kk, kvt, pat, cov_t, batch, seq)

    def group_lanes(tiles):
        rows = tiles.shape[1]
        return tiles.reshape(A_KV, A_GROUP, rows, TQ).transpose(0, 2, 1, 3).reshape(A_KV, rows, GROUP_LANES)

    tab = group_lanes(bias[:, offs[2]:offs[5]]).reshape(A_KV, 3, TQ, GROUP_LANES).transpose(1, 0, 2, 3)
    tab = jnp.concatenate([tab * LOG2E, jnp.full((1,) + tab.shape[1:], NEG, F32)], axis=0)
    o_s = _slc_attend(pa, rs, st, sel, tab, batch, seq)
    rows_pad = jnp.pad(rw.reshape(batch, seq, ROW_W), ((0, 0), (WINDOW, 0), (0, 0)))
    rows_t_pad = jnp.pad(wt, ((0, 0), (0, 0), (WINDOW, 0)))
    o_w = _win_attend(pa, rows_pad, rows_t_pad, group_lanes(bias[:, offs[5]:offs[6]]), batch, seq)
    return o_c, o_s, o_w, sel


HALVES_PER_PAGE = 8
SEL_LANES = 256


def _cmp_sample_kernel(pt_ref, *refs, n_pages, p_len):
    del pt_ref
    pages = refs[:n_pages]
    q_ref, perm_ref, wcat_ref, const_ref, w2_ref, bias_ref, cov_ref, o_ref, idx_ref, rows_ref = refs[n_pages:]
    n_half = n_pages * HALVES_PER_PAGE
    perm = perm_ref[...]
    group = 8
    for j0 in range(0, n_pages, group):
        tiles = [pages[j][0, kv].reshape(KVROW_W, PAGE_ROWS).astype(BF16)
                 for j in range(j0, j0 + group) for kv in range(A_KV)]
        moved = _dot(jnp.concatenate(tiles, axis=0), perm)
        for i in range(group * A_KV):
            j, kv = j0 + i // A_KV, i % A_KV
            rows = moved[i * KVROW_W:(i + 1) * KVROW_W].T
            for p in range(CMP_STRIDE):
                rows_ref[kv, p, j * HALVES_PER_PAGE:(j + 1) * HALVES_PER_PAGE, :] = (
                    rows[p * HALVES_PER_PAGE:(p + 1) * HALVES_PER_PAGE])

    kc = _compress_halves(lambda p, kv: rows_ref[kv, p],
                          wcat_ref, const_ref, w2_ref, n_half)
    kc0, kc1 = kc[:n_half], kc[n_half:]
    q8 = q_ref[0]
    first = lax.broadcasted_iota(jnp.int32, (A_HEADS, n_half), 0) < A_GROUP

    def logits(keys):
        return lax.dot_general(q8, keys, (((1,), (1,)), ((), ())), preferred_element_type=F32, precision=HIGHEST)

    s = jnp.where(first, logits(kc0[:, :A_HD]), logits(kc1[:, :A_HD])) * ATT_SCALE + bias_ref[...]
    e, _, inv = _softmax_keys_on_lanes(s)
    p = e * inv
    pb = p.astype(BF16)
    first_o = lax.broadcasted_iota(jnp.int32, (A_HEADS, A_HD), 0) < A_GROUP
    o_ref[0] = jnp.where(first_o, _dot(pb, kc0[:, A_HD:].astype(BF16)), _dot(pb, kc1[:, A_HD:].astype(BF16)))
    hrow = lax.broadcasted_iota(jnp.int32, (A_HEADS, n_half), 0)
    imp0 = jnp.sum(jnp.where(first, p, 0.0), axis=0, keepdims=True)
    imp1 = jnp.sum(jnp.where(first, 0.0, p), axis=0, keepdims=True)
    imp = jnp.where(hrow == 0, imp0, jnp.where(hrow == 1, imp1, 0.0))
    score = _dot(imp, cov_ref[...], HIGHEST)
    n_slc = p_len // SLC_BLOCK + 1
    cur = p_len // SLC_BLOCK
    lane = lax.broadcasted_iota(jnp.int32, (A_HEADS, SEL_LANES), 1)
    forced = (lane == 0) | (lane == cur) | (lane == cur - 1)
    valid = lane * SLC_BLOCK <= p_len
    sc = jnp.where(forced, jnp.inf, jnp.where(valid, score, -jnp.inf))
    k_sel = float(min(N_SEL, n_slc))
    sub = lax.broadcasted_iota(jnp.int32, (SEL_LANES, SEL_LANES), 0)
    lan = lax.broadcasted_iota(jnp.int32, (SEL_LANES, SEL_LANES), 1)
    slot_l = lax.broadcasted_iota(jnp.int32, (SEL_LANES, LANE), 1).astype(F32)
    blk_s = lax.broadcasted_iota(jnp.int32, (SEL_LANES, LANE), 0).astype(F32)
    out_row = lax.broadcasted_iota(jnp.int32, (A_HEADS, LANE), 0)
    picks = jnp.zeros((A_HEADS, LANE), F32)
    for kv in range(A_KV):
        row = sc[kv:kv + 1, :]
        col = jnp.sum(jnp.where(sub == lan, row, 0.0), axis=1, keepdims=True)
        before_c = (lan < n_slc) & ((row > col) | ((row == col) & (lan < sub)))
        sel_c = (jnp.sum(before_c.astype(F32), axis=1, keepdims=True) < k_sel) & (sub[:, :1] < n_slc)
        before_r = (sub < n_slc) & ((col > row) | ((col == row) & (sub < lan)))
        sel_r = (jnp.sum(before_r.astype(F32), axis=0, keepdims=True) < k_sel) & (lan[:1] < n_slc)
        slot_c = jnp.sum(jnp.where((lan < sub) & sel_r, 1.0, 0.0), axis=1, keepdims=True)
        hit = sel_c & (slot_c == slot_l)
        picks_kv = jnp.sum(jnp.where(hit, blk_s, 0.0), axis=0, keepdims=True)
        picks = jnp.where(out_row == kv, picks_kv, picks)
    idx_ref[0] = picks.astype(jnp.int32)


def _cmp_sample(page_table, cache_pages, q3, wcat, const_row, w2bd, bias_cs, cov, p_len):
    nb, n_pages = page_table.shape
    n_half = n_pages * HALVES_PER_PAGE

    def page_spec(j):
        return pl.BlockSpec((1, A_KV, 2, A_HD, PAGE_ROWS), lambda b, pt: (pt[b * n_pages + j], 0, 0, 0, 0))

    r = np.arange(PAGE_ROWS)
    perm = np.zeros((PAGE_ROWS, PAGE_ROWS), np.float32)
    perm[r, (r % CMP_STRIDE) * HALVES_PER_PAGE + r // CMP_STRIDE] = 1.0
    const2 = lambda b, pt: (0, 0)
    grid_spec = pltpu.PrefetchScalarGridSpec(
        num_scalar_prefetch=1,
        grid=(nb,),
        in_specs=[page_spec(j) for j in range(n_pages)]
                 + [pl.BlockSpec((1, A_HEADS, A_HD), lambda b, pt: (b, 0, 0)),
                    pl.BlockSpec((PAGE_ROWS, PAGE_ROWS), const2),
                    pl.BlockSpec((CMP_STRIDE * KVROW_W, HALF_W), const2),
                    pl.BlockSpec((1, KVROW_W), const2),
                    pl.BlockSpec((KVROW_W, KVROW_W), const2),
                    pl.BlockSpec((A_HEADS, n_half), const2),
                    pl.BlockSpec((n_half, SEL_LANES), const2)],
        out_specs=[pl.BlockSpec((1, A_HEADS, A_HD), lambda b, pt: (b, 0, 0)),
                   pl.BlockSpec((1, A_HEADS, LANE), lambda b, pt: (b, 0, 0))],
        scratch_shapes=[pltpu.VMEM((A_KV, CMP_STRIDE, n_half, KVROW_W), F32)],
    )
    return pl.pallas_call(
        functools.partial(_cmp_sample_kernel, n_pages=n_pages, p_len=p_len),
        grid_spec=grid_spec,
        out_shape=[jax.ShapeDtypeStruct((nb, A_HEADS, A_HD), F32),
                   jax.ShapeDtypeStruct((nb, A_HEADS, LANE), jnp.int32)],
        compiler_params=_cparams("arbitrary"),
        name="cmp_sample",
    )(page_table.reshape(-1), *([cache_pages] * n_pages), q3, jnp.asarray(perm, BF16), wcat, const_row, w2bd,
      bias_cs, cov)


PAGE_ROWS = 128
BLOCKS_PER_PAGE = PAGE_ROWS // SLC_BLOCK


def _slc_sample_kernel(idx_ref, pt_ref, *refs, p_len):
    del pt_ref
    n_blk = A_KV * N_SEL
    blocks = refs[:n_blk]
    q_ref, snew_ref, win_ref, wnew_ref, wcol_ref, oc_ref, g_ref, rbt_ref, bw_ref, ha_ref, wbuf_ref = refs[n_blk:]
    b = pl.program_id(0)
    past_blocks = p_len // SLC_BLOCK
    n_keys = N_SEL * PAGE_ROWS
    n_buf = win_ref.shape[-1]
    qf = q_ref[0]
    q8 = qf.astype(BF16)
    first_o = lax.broadcasted_iota(jnp.int32, (A_HEADS, A_HD), 0) < A_GROUP
    lane = lax.broadcasted_iota(jnp.int32, (1, n_keys), 1)
    slot = lane // PAGE_ROWS
    in_page = lane % PAGE_ROWS
    bucket_row = lax.broadcasted_iota(jnp.int32, (N_BUCKETS, n_keys), 0).astype(F32)
    bias_new = rbt_ref[:, 0:1]

    def new_key_logit(row_ref, kv):
        k_new = row_ref[0, :, kv * LANE:kv * LANE + A_HD]
        v_new = row_ref[0, :, kv * LANE + A_HD:(kv + 1) * LANE]
        return jnp.sum(qf * k_new, axis=1, keepdims=True) * ATT_SCALE + bias_new, v_new

    o_s, o_w = [], []
    for kv in range(A_KV):
        blk_of = jnp.zeros((1, n_keys), jnp.int32)
        has_new = False
        for j in range(N_SEL):
            blk = idx_ref[(b * A_KV + kv) * N_SEL + j]
            blk_of = jnp.where(slot == j, blk, blk_of)
            has_new = jnp.logical_or(has_new, blk == past_blocks)
        pos = jnp.minimum(blk_of, past_blocks - 1) // BLOCKS_PER_PAGE * PAGE_ROWS + in_page
        valid = (pos // SLC_BLOCK == blk_of) & (pos < p_len)
        onehot = (bucket_row == _bucket_dyn(p_len - pos)).astype(F32)
        bias = _dot(rbt_ref[...], onehot, HIGHEST)
        kt = jnp.concatenate([blocks[kv * N_SEL + j][0, 0, 0] for j in range(N_SEL)], axis=1).astype(BF16)
        vt = jnp.concatenate([blocks[kv * N_SEL + j][0, 0, 1] for j in range(N_SEL)], axis=1).astype(BF16)
        s = jnp.where(valid, _dot(q8, kt) * ATT_SCALE + bias, NEG)
        s_new, v_new = new_key_logit(snew_ref, kv)
        s_new = jnp.where(has_new, s_new, NEG)
        e, e_new, inv = _softmax_keys_on_lanes(s, s_new)
        o_s.append((_nt(e.astype(BF16), vt) + e_new * v_new) * inv)
        sw = _dot(q8, win_ref[0, kv, 0].astype(BF16)) * ATT_SCALE + bw_ref[...]
        sw_new, vw_new = new_key_logit(wnew_ref, kv)
        e, e_new, inv = _softmax_keys_on_lanes(sw, sw_new)
        o_w.append((_nt(e.astype(BF16), win_ref[0, kv, 1].astype(BF16)) + e_new * vw_new) * inv)
    g = jax.nn.sigmoid(g_ref[0])
    ha_ref[0] = (g[0] * oc_ref[0] + g[1] * jnp.where(first_o, o_s[0], o_s[1])
                 + g[2] * jnp.where(first_o, o_w[0], o_w[1]))
    last = lax.broadcasted_iota(jnp.int32, (A_HD, n_buf), 1) == n_buf - 1
    for kv in range(A_KV):
        for c in range(2):
            r0 = (kv * 2 + c) * A_HD
            wbuf_ref[0, kv, c] = jnp.where(last, wcol_ref[0, r0:r0 + A_HD, :],
                                           pltpu.roll(win_ref[0, kv, c], n_buf - 1, 1))


def _slc_sample(idx, page_table, cache_t, q3, slc_new, win_t, win_new, win_new_col, o_c, gates, rb_t, bias_ws, p_len):
    nb, n_pages = page_table.shape
    past_blocks = p_len // SLC_BLOCK
    n_buf = win_t.shape[-1]

    def block_spec(kv, j):
        def index_map(b, idx_ref, pt_ref):
            blk = jnp.minimum(idx_ref[(b * A_KV + kv) * N_SEL + j], past_blocks - 1)
            return (pt_ref[b * n_pages + blk // BLOCKS_PER_PAGE], kv, 0, 0, 0)
        return pl.BlockSpec((1, 1, 2, A_HD, PAGE_ROWS), index_map)

    per_seq3 = lambda b, i, p: (b, 0, 0)
    win_spec = pl.BlockSpec((1, A_KV, 2, A_HD, n_buf), lambda b, i, p: (b, 0, 0, 0, 0))
    grid_spec = pltpu.PrefetchScalarGridSpec(
        num_scalar_prefetch=2,
        grid=(nb,),
        in_specs=[block_spec(kv, j) for kv in range(A_KV) for j in range(N_SEL)]
                 + [pl.BlockSpec((1, A_HEADS, A_HD), per_seq3),
                    pl.BlockSpec((1, 1, ROW_W), per_seq3),
                    win_spec,
                    pl.BlockSpec((1, 1, ROW_W), per_seq3),
                    pl.BlockSpec((1, ROW_W, 1), per_seq3),
                    pl.BlockSpec((1, A_HEADS, A_HD), per_seq3),
                    pl.BlockSpec((1, 3, A_HEADS, 1), lambda b, i, p: (b, 0, 0, 0)),
                    pl.BlockSpec((A_HEADS, N_BUCKETS), lambda b, i, p: (0, 0)),
                    pl.BlockSpec((A_HEADS, n_buf), lambda b, i, p: (0, 0))],
        out_specs=[pl.BlockSpec((1, A_HEADS, A_HD), per_seq3), win_spec],
    )
    return pl.pallas_call(
        functools.partial(_slc_sample_kernel, p_len=p_len),
        grid_spec=grid_spec,
        out_shape=[jax.ShapeDtypeStruct((nb, A_HEADS, A_HD), F32),
                   jax.ShapeDtypeStruct(win_t.shape, F32)],
        compiler_params=_cparams("arbitrary"),
        name="slc_win_sample",
    )(idx.reshape(-1), page_table.reshape(-1), *([cache_t] * (A_KV * N_SEL)),
      q3, slc_new, win_t, win_new, win_new_col, o_c, gates, rb_t, bias_ws)


def _rows_last(a):
    n = a.ndim
    return a.transpose(*range(n - 4), n - 3, n - 2, n - 1, n - 4)


def _rows_first(a):
    n = a.ndim
    return a.transpose(*range(n - 4), n - 1, n - 4, n - 3, n - 2)


def _nsa_sample(rel_bias, pa, rs_new, rw_new, cache_cmp, cache_slc, win_cache, page_table,
                bias, offs, cmp_w, const_row, p_len):
    wcat, w2bd = cmp_w
    nb, n_pages = page_table.shape
    n_half = p_len // CMP_STRIDE
    n_slc = p_len // SLC_BLOCK + 1
    n_buf = win_cache.shape[1]
    q3 = pa[:, :A_WIDTH].reshape(nb, A_HEADS, A_HD)
    gates = pa[:, 2 * A_WIDTH:2 * A_WIDTH + 3 * A_HEADS].reshape(nb, 3, A_HEADS, 1)
    bias_cs = bias[:, offs[6]:offs[7]].reshape(A_HEADS, -1)[:, :n_half]
    bias_ws = bias[:, offs[7]:offs[8]].reshape(A_HEADS, -1)[:, :n_buf]
    cov = jnp.asarray(_cover_np(n_half, n_half - 1, SEL_LANES, n_slc))
    o_c, picks = _cmp_sample(page_table, _rows_last(cache_cmp), q3, wcat, const_row, w2bd, bias_cs, cov, p_len)
    idx = picks[:, :A_KV, :N_SEL]
    ha, wbuf = _slc_sample(idx, page_table, _rows_last(cache_slc), q3, rs_new.reshape(nb, 1, ROW_W),
                           _rows_last(win_cache), rw_new.reshape(nb, 1, ROW_W), rw_new.reshape(nb, ROW_W, 1),
                           o_c, gates, rel_bias.T, bias_ws, p_len)
    return ha.reshape(nb, A_WIDTH), idx, wbuf


def kernel(x_prompt, x_sample, cache_cmp_kv, cache_slc_kv, cache_win_kv, state_mlstm_C, state_mlstm_n, state_mlstm_m, page_table, c_prompt, c_sample, rel_bias, w_ada, b_ada, w_in, b_in, m_norm_g, cmp_pe, cmp_w1, cmp_b1, cmp_w2, w_out, b_out, ln_g, ln_b):
    B, T, _ = x_prompt.shape
    NB = x_sample.shape[0]
    n_pages = page_table.shape[1]
    p_len = n_pages * PAGE_ROWS
    depth = w_in.shape[0]
    assert depth == 1 and x_sample.shape[1] == 1 and cache_win_kv.shape[2] == WINDOW
    ids, offs = _static_ids(p_len)
    bias = _bias_tables(rel_bias, ids)
    x_p = x_prompt.reshape(B * T, D_MODEL)
    x_s = x_sample.reshape(NB, D_MODEL)
    l = 0
    n_mod = -(-(B + NB) // SUBLANE) * SUBLANE
    c_all = jnp.concatenate([c_prompt, c_sample, jnp.zeros((n_mod - B - NB, D_MODEL), F32)])
    shift, scale, gate = jnp.split(_adaln_mod(c_all, w_ada[l], b_ada[l]), 3, axis=-1)
    packed = _pack_in_proj(w_in[l], b_in[l], BF16)
    packed_f32 = _pack_in_proj(w_in[l], b_in[l], F32)
    cmp_w = _pack_compress(cmp_w1[l], cmp_w2[l])
    const_row = _compress_const(cmp_pe[l], cmp_w1[l], cmp_b1[l])
    w_out_b = w_out[l].astype(BF16)
    vecs = (b_out[l].reshape(1, -1), ln_g[l].reshape(1, -1), ln_b[l].reshape(1, -1))
    pm, pa, rc, rs, rw, ct, st, wt = _project(x_p, shift[:B, None], scale[:B, None], packed, B, 256)
    mix_m, c_p, n_p, m_p = _mlstm_prompt(pm, m_norm_g[l], B, T)
    o_c, o_s, o_w, _ = _nsa_prompt(rel_bias, pa, rc, rs, rw, st, wt, bias, offs, cmp_w, const_row, B, T)
    y_p = _out_prompt(x_p, mix_m, o_c, o_s, o_w, pa, gate[:B, None], w_out_b, *vecs, B, T)
    pm_s, pa_s, _, rs_s, rw_s, ct_s, st_s, wt_s = _project(x_s, shift[B:B + NB], scale[B:B + NB], packed_f32, 1, NB)
    mix_s, c_s, n_s, m_s = _mlstm_sample(pm_s, m_norm_g[l], state_mlstm_C[l], state_mlstm_n[l], state_mlstm_m[l])
    ha_s, _, wbuf_s = _nsa_sample(rel_bias, pa_s, rs_s, rw_s, cache_cmp_kv[l], cache_slc_kv[l],
                                  cache_win_kv[l], page_table, bias, offs, cmp_w, const_row, p_len)
    y_s = _out_sample(x_s, mix_s.reshape(NB, M_WIDTH), ha_s, pa_s, gate[B:B + NB], w_out_b, *vecs)

    def kv_prompt(a):
        return _rows_first(a.reshape(1, B, A_KV, 2, A_HD, a.shape[-1]))

    def kv_sample(a):
        return a.reshape(1, 1, A_KV, 2, A_HD, NB).transpose(0, 5, 1, 2, 3, 4)

    return (y_p.reshape(B, T, D_MODEL), y_s.reshape(NB, 1, D_MODEL),
            kv_prompt(ct), kv_sample(ct_s), kv_prompt(st), kv_sample(st_s),
            kv_prompt(wt[:, :, T - WINDOW:]), _rows_first(wbuf_s)[None],
            c_p[None], c_s[None], n_p[None], n_s[None], m_p[None, :, :, 0], m_s[None, :, :, 0])
```

```python
import functools
import math

import numpy as np
import jax
import jax.numpy as jnp
from jax import lax
from jax.experimental import pallas as pl
from jax.experimental.pallas import tpu as pltpu

F32 = jnp.float32
BF16 = jnp.bfloat16
HIGHEST = lax.Precision.HIGHEST

D_MODEL = 1024
M_HEADS = 4
M_HD = 128
M_WIDTH = M_HEADS * M_HD
M_CHUNK = 128
A_HEADS = 8
A_HD = 64
A_KV = 2
A_GROUP = A_HEADS // A_KV
A_WIDTH = A_HEADS * A_HD
A_KVW = A_KV * A_HD
ROW_W = 2 * A_KVW
CMP_LEN = 32
CMP_STRIDE = 16
SLC_BLOCK = 64
N_SEL = 16
WINDOW = 512
N_BUCKETS = 32
MAX_EXACT = N_BUCKETS // 2
MAX_DIST = 128
FAR_BUCKET = N_BUCKETS - 1
LN_EPS = 1e-5
ATT_SCALE = A_HD ** -0.5
DEPTH = 1
DEEPNORM_ALPHA = (2.0 * DEPTH) ** 0.25
IN_SPLITS = (M_WIDTH,) * 5 + (M_HEADS, M_HEADS) + (A_WIDTH,) + (A_KVW,) * 6 + (3 * A_HEADS, A_WIDTH)

LANE = 128
SUBLANE = 8
TQ = 128
NEG = -1e30
LOG2E = math.log2(math.e)
MASKED_ID = N_BUCKETS
VMEM_LIMIT = 56 * 1024 * 1024

PM_W = 5 * M_WIDTH + LANE
PA_W = 2 * A_WIDTH + LANE
PW_TOTAL = PM_W + PA_W + 3 * ROW_W


def _cparams(*sem):
    return pltpu.CompilerParams(dimension_semantics=sem, vmem_limit_bytes=VMEM_LIMIT)


def _nt(a, b):
    return lax.dot_general(a, b, (((1,), (1,)), ((), ())), preferred_element_type=F32)


def _dot(a, b, precision=None):
    return jnp.dot(a, b, preferred_element_type=F32, precision=precision)


def _log_sigmoid(x):
    return jnp.minimum(x, 0.0) - jnp.log(1.0 + jnp.exp(-jnp.abs(x)))


def _silu(x):
    return x * jax.nn.sigmoid(x)


def _gelu_tanh(x):
    return 0.5 * x * (1.0 + jnp.tanh(math.sqrt(2.0 / math.pi) * (x + 0.044715 * (x * x * x))))


def _ln_rows(x):
    mu = jnp.mean(x, axis=-1, keepdims=True)
    xc = x - mu
    var = jnp.mean(xc * xc, axis=-1, keepdims=True)
    return xc * lax.rsqrt(var + LN_EPS)


def _bucket_np(dist):
    dist = np.asarray(dist, np.int64)
    n = np.maximum(dist, 0)
    nf = np.maximum(n, 1).astype(np.float32)
    large = MAX_EXACT + (np.log(nf / np.float32(MAX_EXACT)) / np.float32(math.log(MAX_DIST / MAX_EXACT))
                         * np.float32(N_BUCKETS - MAX_EXACT)).astype(np.int32)
    large = np.minimum(large, N_BUCKETS - 1)
    b = np.where(n < MAX_EXACT, n, large)
    return np.where(dist < 0, MASKED_ID, b).astype(np.int32)


def _bucket_dyn(dist):
    n = jnp.maximum(dist, 0)
    nf = jnp.maximum(n, 1).astype(F32)
    large = MAX_EXACT + jnp.floor(jnp.log(nf / MAX_EXACT) / math.log(MAX_DIST / MAX_EXACT)
                                  * (N_BUCKETS - MAX_EXACT))
    large = jnp.minimum(large, float(N_BUCKETS - 1))
    return jnp.where(n < MAX_EXACT, n.astype(F32), large)


def _mod_kernel(c_ref, w_ref, b_ref, o_ref):
    a = _silu(c_ref[...])
    o_ref[...] = _dot(a, w_ref[...]) + b_ref[...]


def _adaln_mod(c, w_ada, b_ada):
    rows = c.shape[0]
    n3 = w_ada.shape[1]
    tn = D_MODEL
    return pl.pallas_call(
        _mod_kernel,
        grid=(n3 // tn,),
        in_specs=[pl.BlockSpec((rows, D_MODEL), lambda j: (0, 0)),
                  pl.BlockSpec((D_MODEL, tn), lambda j: (0, j)),
                  pl.BlockSpec((1, tn), lambda j: (0, j))],
        out_specs=pl.BlockSpec((rows, tn), lambda j: (0, j)),
        out_shape=jax.ShapeDtypeStruct((rows, n3), F32),
        compiler_params=_cparams("arbitrary"),
        name="adaln_mod",
    )(c, w_ada, b_ada.reshape(1, n3))


def _bias_kernel(rb_ref, ids_ref, o_ref, *, n_groups):
    def body(i, carry):
        r0 = pl.multiple_of(i * SUBLANE, SUBLANE)
        ids = ids_ref[pl.ds(r0, SUBLANE), :]
        for h in range(A_HEADS):
            acc = jnp.full((SUBLANE, LANE), NEG, F32)
            for b in range(N_BUCKETS):
                acc = jnp.where(ids == b, rb_ref[b, h], acc)
            o_ref[h, pl.ds(r0, SUBLANE), :] = acc
        return carry

    lax.fori_loop(0, n_groups, body, 0)


def _bias_tables(rel_bias, ids):
    rows = ids.shape[0]
    return pl.pallas_call(
        functools.partial(_bias_kernel, n_groups=rows // SUBLANE),
        in_specs=[pl.BlockSpec(memory_space=pltpu.SMEM),
                  pl.BlockSpec((rows, LANE), lambda: (0, 0))],
        out_specs=pl.BlockSpec((A_HEADS, rows, LANE), lambda: (0, 0, 0)),
        out_shape=jax.ShapeDtypeStruct((A_HEADS, rows, LANE), F32),
        name="bias_tables",
    )(rel_bias, jnp.asarray(ids))


def _pack_in_proj(w_in, b_in, dtype):
    offs = np.cumsum((0,) + IN_SPLITS)
    names = ("mq", "mk", "mv", "mo", "mz", "mi", "mf", "aq", "ck", "cv", "sk", "sv", "wk", "wv", "ga", "za")
    sl = {n: (int(offs[i]), int(offs[i + 1])) for i, n in enumerate(names)}

    def cols(a, name, lo=None, hi=None):
        s, e = sl[name]
        if lo is not None:
            s, e = s + lo, s + hi
        return a[..., s:e]

    def rows_of(a, kn, vn):
        return [cols(a, kn, 0, A_HD), cols(a, vn, 0, A_HD), cols(a, kn, A_HD, 2 * A_HD), cols(a, vn, A_HD, 2 * A_HD)]

    def pack(a):
        def zeros(n):
            return jnp.zeros(a.shape[:-1] + (n,), a.dtype)
        parts = [cols(a, n) for n in ("mq", "mk", "mv", "mo", "mz")]
        parts += [cols(a, "mi"), cols(a, "mf"), zeros(LANE - 2 * M_HEADS)]
        parts += [cols(a, "aq"), cols(a, "za"), cols(a, "ga"), zeros(LANE - 3 * A_HEADS)]
        parts += rows_of(a, "ck", "cv") + rows_of(a, "sk", "sv") + rows_of(a, "wk", "wv")
        return jnp.concatenate(parts, axis=-1)

    w = pack(w_in)
    b = pack(b_in.reshape(1, -1))
    wt = w[:, PM_W + PA_W:].T
    bt = b[:, PM_W + PA_W:].reshape(-1, 1)
    return w.astype(dtype), b, wt.astype(dtype), bt


def _proj_kernel(x_ref, sh_ref, sc_ref, w_ref, b_ref, wt_ref, bt_ref,
                 om_ref, oa_ref, oc_ref, os_ref, ow_ref, oct_ref, ost_ref, owt_ref):
    h = _ln_rows(x_ref[...]) * (1.0 + sc_ref[...]) + sh_ref[...]
    hb = h.astype(w_ref.dtype)
    precision = HIGHEST if w_ref.dtype == F32 else None
    lo = 0
    for o_ref in (om_ref, oa_ref, oc_ref, os_ref, ow_ref):
        n = o_ref.shape[-1]
        o_ref[...] = _dot(hb, w_ref[:, lo:lo + n], precision) + b_ref[:, lo:lo + n]
        lo += n
    t = lax.dot_general(wt_ref[...], hb, (((1,), (1,)), ((), ())), preferred_element_type=F32,
                        precision=precision) + bt_ref[...]
    for i, o_ref in enumerate((oct_ref, ost_ref, owt_ref)):
        o_ref[0] = t[i * ROW_W:(i + 1) * ROW_W]


def _project(x, shift, scale, packed, groups, tm):
    w, b, wt, bt = packed
    rows = x.shape[0]
    per = rows // groups // tm
    if shift.ndim == 3:
        mod_spec = pl.BlockSpec((None, 1, D_MODEL), lambda i: (i // per, 0, 0))
    else:
        mod_spec = pl.BlockSpec((tm, D_MODEL), lambda i: (i, 0))
    widths = (PM_W, PA_W, ROW_W, ROW_W, ROW_W)
    return pl.pallas_call(
        _proj_kernel,
        grid=(rows // tm,),
        in_specs=[pl.BlockSpec((tm, D_MODEL), lambda i: (i, 0)), mod_spec, mod_spec,
                  pl.BlockSpec((D_MODEL, PW_TOTAL), lambda i: (0, 0)),
                  pl.BlockSpec((1, PW_TOTAL), lambda i: (0, 0)),
                  pl.BlockSpec((3 * ROW_W, D_MODEL), lambda i: (0, 0)),
                  pl.BlockSpec((3 * ROW_W, 1), lambda i: (0, 0))],
        out_specs=[pl.BlockSpec((tm, n), lambda i: (i, 0)) for n in widths]
                  + [pl.BlockSpec((1, ROW_W, tm), lambda i: (i // per, 0, i % per))] * 3,
        out_shape=[jax.ShapeDtypeStruct((rows, n), F32) for n in widths]
                  + [jax.ShapeDtypeStruct((groups, ROW_W, rows // groups), F32)] * 3,
        compiler_params=_cparams("arbitrary"),
        name="in_proj",
    )(x, shift, scale, w, b, wt, bt)


def _mlstm_head_out(h, o_pre, z_pre, g_row):
    return jax.nn.sigmoid(o_pre) * (_ln_rows(h) * g_row) * _silu(z_pre)


def _mlstm_prompt_kernel(q_ref, k_ref, v_ref, o_ref, z_ref, g_ref, ng_ref, mix_ref, c_ref, n_ref, m_ref):
    L = M_CHUNK

    @pl.when(pl.program_id(0) == 0)
    def _():
        c_ref[...] = jnp.zeros_like(c_ref)
        n_ref[...] = jnp.zeros_like(n_ref)
        m_ref[...] = jnp.zeros_like(m_ref)

    row = lax.broadcasted_iota(jnp.int32, (L, L), 0)
    col = lax.broadcasted_iota(jnp.int32, (L, L), 1)
    tril = col <= row
    lower = tril.astype(F32)
    upper = (row <= col).astype(F32)
    nb = q_ref.shape[0]
    units = [(b, h) for b in range(nb) for h in range(M_HEADS)]
    gates = [g_ref[b] for b in range(nb)]
    gates_t = [g.T for g in gates]
    cum = [_dot(lower, _log_sigmoid(g), HIGHEST) for g in gates]
    cum_t = [_dot(_log_sigmoid(g), upper, HIGHEST) for g in gates_t]
    st = {}
    for b, h in units:
        hs = slice(h * M_HD, (h + 1) * M_HD)
        b_col = cum[b][:, M_HEADS + h:M_HEADS + h + 1]
        b_row = cum_t[b][M_HEADS + h:M_HEADS + h + 1, :]
        m_prev = m_ref[b, h:h + 1, 0:1]
        d = jnp.where(tril, b_col - b_row + gates_t[b][h:h + 1, :], NEG)
        inter = b_col + m_prev
        m_t = jnp.maximum(inter, jnp.max(d, axis=1, keepdims=True))
        q = q_ref[b, :, hs]
        ks = k_ref[b, :, hs] * (M_HD ** -0.5)
        st[b, h] = dict(hs=hs, b_col=b_col, m_prev=m_prev, d=d, m_t=m_t, w_inter=jnp.exp(inter - m_t), q=q, ks=ks,
                        qb=q.astype(BF16), kb=ks.astype(BF16), vb=v_ref[b, :, hs].astype(BF16),
                        c_prev=c_ref[b, h], n_prev=n_ref[b, h:h + 1, :])
    for u in units:
        s = st[u]
        s["qk"] = _nt(s["qb"], s["kb"]) * jnp.exp(s["d"] - s["m_t"])
        s["qc"] = _dot(s["qb"], s["c_prev"].astype(BF16))
    for u in units:
        s = st[u]
        num = s["w_inter"] * s["qc"] + _dot(s["qk"].astype(BF16), s["vb"])
        den = (s["w_inter"] * jnp.sum(s["q"] * s["n_prev"], axis=1, keepdims=True)
               + jnp.sum(s["qk"], axis=1, keepdims=True))
        s["hh"] = num / jnp.maximum(jnp.abs(den), jnp.exp(-s["m_t"]))
    for (b, h) in units:
        s = st[b, h]
        m_new = s["m_t"][L - 1:L, :]
        b_last = s["b_col"][L - 1:L, :]
        w_c = jnp.exp(b_last + s["m_prev"] - m_new)
        w_s = jnp.exp(b_last - s["b_col"] + gates[b][:, h:h + 1] - m_new)
        kw = s["ks"] * w_s
        c_ref[b, h] = w_c * s["c_prev"] + _dot(kw.T.astype(BF16), s["vb"])
        n_ref[b, h:h + 1, :] = w_c * s["n_prev"] + jnp.sum(kw, axis=0, keepdims=True)
        m_ref[b, h:h + 1, :] = jnp.broadcast_to(m_new, (1, M_HD))
    for (b, h) in units:
        s = st[b, h]
        hs = s["hs"]
        mix_ref[b, :, hs] = _mlstm_head_out(s["hh"], o_ref[b, :, hs], z_ref[b, :, hs], ng_ref[:, hs])


def _mlstm_prompt(pm, norm_g, batch, seq):
    nc = seq // M_CHUNK
    pm3 = pm.reshape(batch, seq, PM_W)

    def col_spec(j, width=M_WIDTH):
        return pl.BlockSpec((batch, M_CHUNK, width), lambda c: (0, c, j))

    state = lambda c: (0, 0, 0)
    mix, c_p, n_p, m_p = pl.pallas_call(
        _mlstm_prompt_kernel,
        grid=(nc,),
        in_specs=[col_spec(0), col_spec(1), col_spec(2), col_spec(3), col_spec(4),
                  pl.BlockSpec((batch, M_CHUNK, LANE), lambda c: (0, c, 5 * M_WIDTH // LANE)),
                  pl.BlockSpec((1, M_WIDTH), lambda c: (0, 0))],
        out_specs=[pl.BlockSpec((batch, M_CHUNK, M_WIDTH), lambda c: (0, c, 0)),
                   pl.BlockSpec((batch, M_HEADS, M_HD, M_HD), lambda c: (0, 0, 0, 0)),
                   pl.BlockSpec((batch, M_HEADS, M_HD), state),
                   pl.BlockSpec((batch, M_HEADS, M_HD), state)],
        out_shape=[jax.ShapeDtypeStruct((batch, seq, M_WIDTH), F32),
                   jax.ShapeDtypeStruct((batch, M_HEADS, M_HD, M_HD), F32),
                   jax.ShapeDtypeStruct((batch, M_HEADS, M_HD), F32),
                   jax.ShapeDtypeStruct((batch, M_HEADS, M_HD), F32)],
        compiler_params=_cparams("arbitrary"),
        name="mlstm_prompt",
    )(pm3, pm3, pm3, pm3, pm3, pm3, norm_g.reshape(1, M_WIDTH))
    return mix.reshape(batch * seq, M_WIDTH), c_p, n_p, m_p


MS_G = 128


def _mlstm_sample_kernel(q_ref, k_ref, v_ref, o_ref, z_ref, ig_ref, fg_ref, m_ref, n_ref, c_ref, ng_ref,
                         mix_ref, co_ref, no_ref, mo_ref):
    q = q_ref[...]
    ks = k_ref[...] * (M_HD ** -0.5)
    v = v_ref[...]
    n_prev = n_ref[...]
    ig = ig_ref[...]
    inter = _log_sigmoid(fg_ref[...]) + m_ref[...]
    m_t = jnp.maximum(inter, ig)
    w_inter = jnp.exp(inter - m_t)
    w_s = jnp.exp(ig - m_t)
    qk = jnp.sum(q * ks, axis=1, keepdims=True) * w_s
    q_t = q.T
    kw_t = (ks * w_s).T
    rows = []
    for r in range(MS_G):
        c_prev = c_ref[r]
        rows.append(jnp.sum(q_t[:, r:r + 1] * c_prev, axis=0, keepdims=True))
        co_ref[r] = w_inter[r:r + 1, :] * c_prev + kw_t[:, r:r + 1] * v[r:r + 1, :]
    q_c = jnp.concatenate(rows, axis=0)
    num = w_inter * q_c + qk * v
    den = w_inter * jnp.sum(q * n_prev, axis=1, keepdims=True) + qk
    hh = num / jnp.maximum(jnp.abs(den), jnp.exp(-m_t))
    no_ref[...] = w_inter * n_prev + w_s * ks
    mo_ref[...] = m_t
    mix_ref[...] = _mlstm_head_out(hh, o_ref[...], z_ref[...], ng_ref[...])


def _mlstm_sample(pm, norm_g, c0, n0, m0):
    nb = pm.shape[0]
    rows = nb * M_HEADS

    def head_rows(j):
        return pm[:, j * M_WIDTH:(j + 1) * M_WIDTH].reshape(rows, M_HD)

    gates = pm[:, 5 * M_WIDTH:5 * M_WIDTH + 2 * M_HEADS]
    ig = gates[:, :M_HEADS].reshape(rows, 1)
    fg = gates[:, M_HEADS:].reshape(rows, 1)
    ng_rows = jnp.tile(norm_g.reshape(M_HEADS, M_HD), (MS_G // M_HEADS, 1))
    vec = pl.BlockSpec((MS_G, M_HD), lambda i: (i, 0))
    one = pl.BlockSpec((MS_G, 1), lambda i: (i, 0))
    mat = pl.BlockSpec((MS_G, M_HD, M_HD), lambda i: (i, 0, 0))
    mix, c1, n1, m1 = pl.pallas_call(
        _mlstm_sample_kernel,
        grid=(rows // MS_G,),
        in_specs=[vec] * 5 + [one] * 3 + [vec, mat, pl.BlockSpec((MS_G, M_HD), lambda i: (0, 0))],
        out_specs=[vec, mat, vec, one],
        out_shape=[jax.ShapeDtypeStruct((rows, M_HD), F32),
                   jax.ShapeDtypeStruct((rows, M_HD, M_HD), F32),
                   jax.ShapeDtypeStruct((rows, M_HD), F32),
                   jax.ShapeDtypeStruct((rows, 1), F32)],
        compiler_params=_cparams("arbitrary"),
        name="mlstm_sample",
    )(*[head_rows(j) for j in range(5)], ig, fg, m0.reshape(rows, 1), n0.reshape(rows, M_HD),
      c0.reshape(rows, M_HD, M_HD), ng_rows)
    return (mix.reshape(nb, M_WIDTH), c1.reshape(nb, M_HEADS, M_HD, M_HD), n1.reshape(nb, M_HEADS, M_HD),
            m1.reshape(nb, M_HEADS))


KVROW_W = 2 * A_HD
HALF_W = 2 * KVROW_W


def _pack_compress(w1, w2):
    def block_diag(k, v):
        z = jnp.zeros_like(k)
        return jnp.concatenate([jnp.concatenate([k, z], axis=-1), jnp.concatenate([z, v], axis=-1)], axis=-2)

    wbd = block_diag(w1[0], w1[1])
    wcat = jnp.concatenate([wbd[:CMP_STRIDE], wbd[CMP_STRIDE:]], axis=-1)
    return wcat.reshape(CMP_STRIDE * KVROW_W, HALF_W).astype(BF16), block_diag(w2[0], w2[1]).astype(BF16)


def _cmp_const_kernel(pe_ref, w_ref, b_ref, o_ref):
    for c in range(2):
        o_ref[c] = _dot(pe_ref[c], w_ref[c], HIGHEST) + b_ref[c]


def _compress_const(pe, w1, b1):
    k = CMP_LEN * A_HD
    pe8 = jnp.broadcast_to(pe.reshape(2, 1, k), (2, SUBLANE, k))
    out = pl.pallas_call(
        _cmp_const_kernel,
        out_shape=jax.ShapeDtypeStruct((2, SUBLANE, A_HD), F32),
        name="compress_const",
    )(pe8, w1.reshape(2, k, A_HD), b1.reshape(2, 1, A_HD))
    return jnp.concatenate([out[0, 0:1], out[1, 0:1]], axis=-1)


def _compress_halves(load_rows, wcat_ref, const_ref, w2_ref, n_half):
    halves = jnp.concatenate(
        [jnp.concatenate([load_rows(p, kv) for kv in range(A_KV)], axis=0).astype(BF16) for p in range(CMP_STRIDE)],
        axis=1)
    acc = _dot(halves, wcat_ref[...])
    pre = acc[:, :KVROW_W] + pltpu.roll(acc[:, KVROW_W:], A_KV * n_half - 1, 0) + const_ref[...]
    return _dot(_gelu_tanh(pre).astype(BF16), w2_ref[...])


def _compress_prompt_kernel(x0_ref, x1_ref, wcat_ref, const_ref, w2_ref, kk_ref, kvt_ref, *, n_half):
    x_refs = (x0_ref, x1_ref)
    kc = _compress_halves(lambda p, kv: x_refs[kv][pl.ds(p, n_half, stride=CMP_STRIDE), :],
                          wcat_ref, const_ref, w2_ref, n_half)
    kct = kc.T
    for kv in range(A_KV):
        kk_ref[0, kv] = kc[kv * n_half:(kv + 1) * n_half, 0:A_HD]
        kvt_ref[0, kv] = kct[A_HD:, kv * n_half:(kv + 1) * n_half]


def _compress_prompt(rows, wcat, const_row, w2bd, batch, seq):
    n_half = seq // CMP_STRIDE
    return pl.pallas_call(
        functools.partial(_compress_prompt_kernel, n_half=n_half),
        grid=(batch,),
        in_specs=[pl.BlockSpec((seq, KVROW_W), lambda b: (b, 0)),
                  pl.BlockSpec((seq, KVROW_W), lambda b: (b, 1)),
                  pl.BlockSpec((CMP_STRIDE * KVROW_W, HALF_W), lambda b: (0, 0)),
                  pl.BlockSpec((1, KVROW_W), lambda b: (0, 0)),
                  pl.BlockSpec((KVROW_W, KVROW_W), lambda b: (0, 0))],
        out_specs=[pl.BlockSpec((1, A_KV, n_half, A_HD), lambda b: (b, 0, 0, 0)),
                   pl.BlockSpec((1, A_KV, A_HD, n_half), lambda b: (b, 0, 0, 0))],
        out_shape=[jax.ShapeDtypeStruct((batch, A_KV, n_half, A_HD), F32),
                   jax.ShapeDtypeStruct((batch, A_KV, A_HD, n_half), F32)],
        compiler_params=_cparams("arbitrary"),
        name="compress_prompt",
    )(rows, rows, wcat, const_row, w2bd)


CMP_PAT = 16


def _static_ids(p_len):
    i = np.arange(TQ)[None, :]
    c = np.arange(CMP_PAT)[:, None]
    cmp_a = _bucket_np(i + (TQ - (CMP_LEN - 1)) - CMP_STRIDE * c)
    cmp_b = _bucket_np(i - CMP_STRIDE * c - (CMP_LEN - 1))
    r = np.arange(TQ)[:, None]
    slc_diag = _bucket_np(i - r)
    slc_sub = _bucket_np(TQ + i - r)
    slc_far = np.full((TQ, TQ), FAR_BUCKET, np.int32)
    rw = np.arange(WINDOW + TQ)[:, None]
    dw = WINDOW + i - rw
    win = np.where(dw > WINDOW, MASKED_ID, _bucket_np(dw))
    n_half = p_len // CMP_STRIDE
    n = np.arange(n_half)
    cs = _bucket_np(p_len - (CMP_STRIDE * n + CMP_LEN - 1))
    cs[n_half - 1] = MASKED_ID
    cs_rows = -(-n_half // LANE)
    cs_pad = np.full((cs_rows * LANE,), MASKED_ID, np.int32)
    cs_pad[:n_half] = cs
    ws = _bucket_np(WINDOW - np.arange(WINDOW))
    parts = [cmp_a, cmp_b, slc_diag, slc_sub, slc_far, win, cs_pad.reshape(cs_rows, LANE),
             ws.reshape(WINDOW // LANE, LANE)]
    offs = np.cumsum([0] + [p.shape[0] for p in parts])
    total = -(-int(offs[-1]) // SUBLANE) * SUBLANE
    ids = np.full((total, LANE), MASKED_ID, np.int32)
    ids[:offs[-1]] = np.concatenate(parts, axis=0)
    return ids, [int(o) for o in offs]


def _cover_np(n_cmp_rows, n_cmp, n_slc_rows, n_slc):
    cs = np.arange(n_cmp_rows)[:, None] * CMP_STRIDE
    ss = np.arange(n_slc_rows)[None, :] * SLC_BLOCK
    cov = (cs <= ss + SLC_BLOCK - 1) & (cs + CMP_LEN - 1 >= ss)
    cov &= (np.arange(n_cmp_rows)[:, None] < n_cmp) & (np.arange(n_slc_rows)[None, :] < n_slc)
    return cov.astype(np.float32)


def _softmax_keys_on_rows(s):
    m = jnp.max(s, axis=0, keepdims=True)
    m = jnp.where(m > 0.5 * NEG, m, 0.0)
    e = jnp.exp(s - m)
    tot = jnp.sum(e, axis=0, keepdims=True)
    return e / jnp.where(tot > 0.0, tot, 1.0)


def _softmax_keys_on_lanes(s, s_new=None):
    m = jnp.max(s, axis=1, keepdims=True)
    if s_new is not None:
        m = jnp.maximum(m, s_new)
    m = jnp.where(m > 0.5 * NEG, m, 0.0)
    e = jnp.exp(s - m)
    tot = jnp.sum(e, axis=1, keepdims=True)
    e_new = None
    if s_new is not None:
        e_new = jnp.exp(s_new - m)
        tot = tot + e_new
    return e, e_new, 1.0 / jnp.where(tot > 0.0, tot, 1.0)


def _cmp_attend_kernel(far_ref, q_ref, kk_ref, kvt_ref, pt_ref, cov_ref, o_ref, sel_ref, bscr, *, nc, ns):
    k = pl.program_id(1)
    start = pl.multiple_of(jnp.maximum(SUBLANE * k - SUBLANE, 0), SUBLANE)
    variant = jnp.where(k == 0, 1, 0)
    row = lax.broadcasted_iota(jnp.int32, (nc, GROUP_LANES), 0)
    t = k * TQ + lax.broadcasted_iota(jnp.int32, (ns, TQ), 1)
    blk = lax.broadcasted_iota(jnp.int32, (ns, TQ), 0)
    cur = t // SLC_BLOCK
    valid = blk * SLC_BLOCK <= t
    forced = (blk == 0) | (blk == cur) | (blk == cur - 1)
    kvs = range(A_KV)
    for kv in kvs:
        bscr[kv] = jnp.where(row < start, far_ref[kv], NEG)
        bscr[kv, pl.ds(start, CMP_PAT), :] = pt_ref[variant, kv]
    s = [_nt(kk_ref[0, kv].astype(BF16), _stacked_queries(q_ref, kv)) + bscr[kv] for kv in kvs]
    p = [_softmax_keys_on_rows(s[kv]) for kv in kvs]
    o = [_dot(kvt_ref[0, kv].astype(BF16), p[kv].astype(BF16)) for kv in kvs]
    for kv in kvs:
        for g in range(A_GROUP):
            h = kv * A_GROUP + g
            o_ref[0, h * A_HD:(h + 1) * A_HD, :] = o[kv][:, g * TQ:(g + 1) * TQ]
    imp = [sum(p[kv][:, g * TQ:(g + 1) * TQ] for g in range(A_GROUP)) for kv in kvs]
    sc = [jnp.where(forced, jnp.inf, jnp.where(valid, _dot(cov_ref[...], imp[kv], HIGHEST), -jnp.inf)) for kv in kvs]
    cnt = [jnp.zeros((ns, TQ), jnp.int32) for _ in kvs]
    for j in range(ns):
        for kv in kvs:
            r = sc[kv][j:j + 1, :]
            before = (r > sc[kv]) | ((r == sc[kv]) & (blk > j))
            cnt[kv] = cnt[kv] + before.astype(jnp.int32)
    for kv in kvs:
        sel_ref[0, kv] = jnp.where(cnt[kv] < N_SEL, 0.0, NEG)


def _cmp_attend(far, pa, kk, kvt, pat, cov_t, batch, seq):
    nq = seq // TQ
    nc = seq // CMP_STRIDE
    ns = seq // SLC_BLOCK
    return pl.pallas_call(
        functools.partial(_cmp_attend_kernel, nc=nc, ns=ns),
        grid=(batch, nq),
        in_specs=[pl.BlockSpec((A_KV, 1, GROUP_LANES), lambda b, k: (0, 0, 0)),
                  pl.BlockSpec((TQ, A_WIDTH), lambda b, k: (b * nq + k, 0)),
                  pl.BlockSpec((1, A_KV, nc, A_HD), lambda b, k: (b, 0, 0, 0)),
                  pl.BlockSpec((1, A_KV, A_HD, nc), lambda b, k: (b, 0, 0, 0)),
                  pl.BlockSpec((2, A_KV, CMP_PAT, GROUP_LANES), lambda b, k: (0, 0, 0, 0)),
                  pl.BlockSpec((ns, nc), lambda b, k: (0, 0))],
        out_specs=[pl.BlockSpec((1, A_WIDTH, TQ), lambda b, k: (b, 0, k)),
                   pl.BlockSpec((1, A_KV, ns, TQ), lambda b, k: (b, 0, 0, k))],
        out_shape=[jax.ShapeDtypeStruct((batch, A_WIDTH, seq), F32),
                   jax.ShapeDtypeStruct((batch, A_KV, ns, seq), F32)],
        scratch_shapes=[pltpu.VMEM((A_KV, nc, GROUP_LANES), F32)],
        compiler_params=_cparams("arbitrary", "arbitrary"),
        name="cmp_attend",
    )(far, pa, kk, kvt, pat, cov_t)


SLC_CK = 4 * TQ
GROUP_LANES = A_GROUP * TQ
SLC_CLASSES = 4


def _stacked_queries(q_ref, kv, scale=ATT_SCALE):
    heads = [q_ref[:, (kv * A_GROUP + g) * A_HD:(kv * A_GROUP + g + 1) * A_HD] for g in range(A_GROUP)]
    return (jnp.concatenate(heads, axis=0) * scale).astype(BF16)


def _slc_attend_kernel(q_ref, k_ref, vt_ref, sel_ref, tab_ref, o_ref):
    k = pl.program_id(1)
    sub = SLC_CK // TQ
    q4 = [_stacked_queries(q_ref, kv, ATT_SCALE * LOG2E) for kv in range(A_KV)]
    upper = lax.broadcasted_iota(jnp.int32, (TQ, GROUP_LANES), 0) < SLC_BLOCK

    def body(j, carry):
        j0 = pl.multiple_of(j * SLC_CK, SLC_CK)
        kvs = range(A_KV)
        s_all = [_nt(k_ref[pl.ds(j0, SLC_CK), kv * LANE:kv * LANE + A_HD].astype(BF16), q4[kv]) for kv in kvs]
        vt = [vt_ref[0, kv * LANE + A_HD:(kv + 1) * LANE, pl.ds(j0, SLC_CK)].astype(BF16) for kv in kvs]
        s = []
        for kv in kvs:
            parts = []
            for u in range(sub):
                jj = j * sub + u
                cls = jnp.where(jj > k, SLC_CLASSES - 1, jnp.minimum(k - jj, 2))
                sel0 = jnp.concatenate([sel_ref[0, kv, pl.ds(2 * jj, 1), :]] * A_GROUP, axis=1)
                sel1 = jnp.concatenate([sel_ref[0, kv, pl.ds(2 * jj + 1, 1), :]] * A_GROUP, axis=1)
                parts.append(s_all[kv][u * TQ:(u + 1) * TQ] + tab_ref[cls, kv] + jnp.where(upper, sel0, sel1))
            s.append(jnp.concatenate(parts, axis=0))
        m_new = [jnp.maximum(carry[kv][0], jnp.max(s[kv], axis=0, keepdims=True)) for kv in kvs]
        p = [jnp.exp2(s[kv] - m_new[kv]) for kv in kvs]
        pv = [_dot(vt[kv], p[kv].astype(BF16)) for kv in kvs]
        out = []
        for kv in kvs:
            m_run, l_run, acc = carry[kv]
            alpha = jnp.exp2(m_run - m_new[kv])
            l_new = alpha * l_run + jnp.sum(p[kv], axis=0, keepdims=True)
            out.append((m_new[kv], l_new, alpha * acc + pv[kv]))
        return tuple(out)

    init = tuple((jnp.full((1, GROUP_LANES), NEG, F32), jnp.zeros((1, GROUP_LANES), F32),
                  jnp.zeros((A_HD, GROUP_LANES), F32)) for _ in range(A_KV))
    res = lax.fori_loop(0, (k + sub) // sub, body, init)
    for kv in range(A_KV):
        _, l_run, acc = res[kv]
        o = acc / l_run
        for g in range(A_GROUP):
            h = kv * A_GROUP + g
            o_ref[0, h * A_HD:(h + 1) * A_HD, :] = o[:, g * TQ:(g + 1) * TQ]


def _slc_attend(pa, rows, rows_t, sel, tab, batch, seq):
    nq = seq // TQ
    ns = seq // SLC_BLOCK
    return pl.pallas_call(
        _slc_attend_kernel,
        grid=(batch, nq),
        in_specs=[pl.BlockSpec((TQ, A_WIDTH), lambda b, k: (b * nq + k, 0)),
                  pl.BlockSpec((seq, ROW_W), lambda b, k: (b, 0)),
                  pl.BlockSpec((1, ROW_W, seq), lambda b, k: (b, 0, 0)),
                  pl.BlockSpec((1, A_KV, ns, TQ), lambda b, k: (b, 0, 0, k)),
                  pl.BlockSpec((SLC_CLASSES, A_KV, TQ, GROUP_LANES), lambda b, k: (0, 0, 0, 0))],
        out_specs=pl.BlockSpec((1, A_WIDTH, TQ), lambda b, k: (b, 0, k)),
        out_shape=jax.ShapeDtypeStruct((batch, A_WIDTH, seq), F32),
        compiler_params=_cparams("arbitrary", "arbitrary"),
        name="slc_attend",
    )(pa, rows, rows_t, sel, tab)


WIN_SPAN = WINDOW + TQ


def _win_attend_kernel(q_ref, k_ref, vt_ref, bias_ref, o_ref):
    k = pl.program_id(1)
    r0 = pl.multiple_of(k * TQ, TQ)
    exists = lax.broadcasted_iota(jnp.int32, (WIN_SPAN, GROUP_LANES), 0) + k * TQ >= WINDOW
    kvs = range(A_KV)
    s = [_nt(k_ref[0, pl.ds(r0, WIN_SPAN), kv * LANE:kv * LANE + A_HD].astype(BF16), _stacked_queries(q_ref, kv))
         + bias_ref[kv] for kv in kvs]
    p = [_softmax_keys_on_rows(jnp.where(exists, s[kv], NEG)) for kv in kvs]
    o = [_dot(vt_ref[0, kv * LANE + A_HD:(kv + 1) * LANE, pl.ds(r0, WIN_SPAN)].astype(BF16), p[kv].astype(BF16))
         for kv in kvs]
    for kv in kvs:
        for g in range(A_GROUP):
            h = kv * A_GROUP + g
            o_ref[0, h * A_HD:(h + 1) * A_HD, :] = o[kv][:, g * TQ:(g + 1) * TQ]


def _win_attend(pa, rows_pad, rows_t_pad, bias_w, batch, seq):
    nq = seq // TQ
    return pl.pallas_call(
        _win_attend_kernel,
        grid=(batch, nq),
        in_specs=[pl.BlockSpec((TQ, A_WIDTH), lambda b, k: (b * nq + k, 0)),
                  pl.BlockSpec((1, seq + WINDOW, ROW_W), lambda b, k: (b, 0, 0)),
                  pl.BlockSpec((1, ROW_W, seq + WINDOW), lambda b, k: (b, 0, 0)),
                  pl.BlockSpec((A_KV, WIN_SPAN, GROUP_LANES), lambda b, k: (0, 0, 0))],
        out_specs=pl.BlockSpec((1, A_WIDTH, TQ), lambda b, k: (b, 0, k)),
        out_shape=jax.ShapeDtypeStruct((batch, A_WIDTH, seq), F32),
        compiler_params=_cparams("arbitrary", "arbitrary"),
        name="win_attend",
    )(pa, rows_pad, rows_t_pad, bias_w)


def _out_tail(x, mix_m, mix_a, gate, w_ref, b_ref, g_ref, beta_ref):
    y = (_dot(mix_m.astype(BF16), w_ref[:M_WIDTH]) + _dot(mix_a.astype(BF16), w_ref[M_WIDTH:]) + b_ref[...])
    return _ln_rows(DEEPNORM_ALPHA * x + gate * y) * g_ref[...] + beta_ref[...]


OUT_TM = 512


def _out_prompt_kernel(x_ref, mm_ref, oc_ref, os_ref, ow_ref, ga_ref, za_ref, gate_ref,
                       w_ref, b_ref, g_ref, beta_ref, y_ref):
    sig = jax.nn.sigmoid(ga_ref[...].T)
    parts = []
    for h in range(A_HEADS):
        hs = slice(h * A_HD, (h + 1) * A_HD)
        parts.append(sig[h:h + 1] * oc_ref[0, hs, :] + sig[A_HEADS + h:A_HEADS + h + 1] * os_ref[0, hs, :]
                     + sig[2 * A_HEADS + h:2 * A_HEADS + h + 1] * ow_ref[0, hs, :])
    ha = jnp.concatenate(parts, axis=0).T
    mix_a = ha * _silu(za_ref[...])
    y_ref[...] = _out_tail(x_ref[...], mm_ref[...], mix_a, gate_ref[...], w_ref, b_ref, g_ref, beta_ref)


def _out_prompt(x, mix_m, o_c, o_s, o_w, pa, gate, w_out, b_out, ln_g, ln_b, batch, seq):
    tm = OUT_TM
    nq = seq // tm
    rows = batch * seq
    branch = pl.BlockSpec((1, A_WIDTH, tm), lambda i: (i // nq, 0, i % nq))
    vec = pl.BlockSpec((1, D_MODEL), lambda i: (0, 0))
    return pl.pallas_call(
        _out_prompt_kernel,
        grid=(rows // tm,),
        in_specs=[pl.BlockSpec((tm, D_MODEL), lambda i: (i, 0)),
                  pl.BlockSpec((tm, M_WIDTH), lambda i: (i, 0)),
                  branch, branch, branch,
                  pl.BlockSpec((tm, LANE), lambda i: (i, 2 * A_WIDTH // LANE)),
                  pl.BlockSpec((tm, A_WIDTH), lambda i: (i, 1)),
                  pl.BlockSpec((None, 1, D_MODEL), lambda i: (i // nq, 0, 0)),
                  pl.BlockSpec((D_MODEL, D_MODEL), lambda i: (0, 0)),
                  vec, vec, vec],
        out_specs=pl.BlockSpec((tm, D_MODEL), lambda i: (i, 0)),
        out_shape=jax.ShapeDtypeStruct((rows, D_MODEL), F32),
        compiler_params=_cparams("arbitrary"),
        name="out_prompt",
    )(x, mix_m, o_c, o_s, o_w, pa, pa, gate, w_out, b_out, ln_g, ln_b)


def _out_sample_kernel(x_ref, mm_ref, ha_ref, za_ref, gate_ref, w_ref, b_ref, g_ref, beta_ref, y_ref):
    mix_a = ha_ref[...] * _silu(za_ref[...])
    y_ref[...] = _out_tail(x_ref[...], mm_ref[...], mix_a, gate_ref[...], w_ref, b_ref, g_ref, beta_ref)


def _out_sample(x, mix_m, ha, pa, gate, w_out, b_out, ln_g, ln_b):
    rows = x.shape[0]
    vec = pl.BlockSpec((1, D_MODEL), lambda i: (0, 0))
    return pl.pallas_call(
        _out_sample_kernel,
        grid=(1,),
        in_specs=[pl.BlockSpec((rows, D_MODEL), lambda i: (0, 0)),
                  pl.BlockSpec((rows, M_WIDTH), lambda i: (0, 0)),
                  pl.BlockSpec((rows, A_WIDTH), lambda i: (0, 0)),
                  pl.BlockSpec((rows, A_WIDTH), lambda i: (0, 1)),
                  pl.BlockSpec((rows, D_MODEL), lambda i: (0, 0)),
                  pl.BlockSpec((D_MODEL, D_MODEL), lambda i: (0, 0)),
                  vec, vec, vec],
        out_specs=pl.BlockSpec((rows, D_MODEL), lambda i: (0, 0)),
        out_shape=jax.ShapeDtypeStruct((rows, D_MODEL), F32),
        compiler_params=_cparams("arbitrary"),
        name="out_sample",
    )(x, mix_m, ha, pa, gate, w_out, b_out, ln_g, ln_b)


def _nsa_prompt(rel_bias, pa, rc, rs, rw, st, wt, bias, offs, cmp_w, const_row, batch, seq):
    wcat, w2bd = cmp_w
    nc = seq // CMP_STRIDE
    ns = seq // SLC_BLOCK
    kk, kvt = _compress_prompt(rc, wcat, const_row, w2bd, batch, seq)

    def group_lanes(tiles):
        rows = tiles.shape[1]
        return tiles.reshape(A_KV, A_GROUP, rows, TQ).transpose(0, 2, 1, 3).reshape(A_KV, rows, GROUP_LANES)

    pat = group_lanes(bias[:, offs[0]:offs[2]]).reshape(A_KV, 2, CMP_PAT, GROUP_LANES).transpose(1, 0, 2, 3)
    far = jnp.repeat(rel_bias[FAR_BUCKET].reshape(A_KV, A_GROUP), TQ, axis=1).reshape(A_KV, 1, GROUP_LANES)
    cov_t = jnp.asarray(_cover_np(nc, nc - 1, ns, ns).T)
    o_c, sel = _cmp_attend(far, pa, kk, kvt, pat, cov_t, batch, seq)

    tab = group_lanes(bias[:, offs[2]:offs[5]]).reshape(A_KV, 3, TQ, GROUP_LANES).transpose(1, 0, 2, 3)
    tab = jnp.concatenate([tab * LOG2E, jnp.full((1,) + tab.shape[1:], NEG, F32)], axis=0)
    o_s = _slc_attend(pa, rs, st, sel, tab, batch, seq)
    rows_pad = jnp.pad(rw.reshape(batch, seq, ROW_W), ((0, 0), (WINDOW, 0), (0, 0)))
    rows_t_pad = jnp.pad(wt, ((0, 0), (0, 0), (WINDOW, 0)))
    o_w = _win_attend(pa, rows_pad, rows_t_pad, group_lanes(bias[:, offs[5]:offs[6]]), batch, seq)
    return o_c, o_s, o_w, sel


HALVES_PER_PAGE = 8
SEL_LANES = 256


def _cmp_sample_kernel(pt_ref, *refs, n_pages, p_len):
    del pt_ref
    pages = refs[:n_pages]
    q_ref, perm_ref, wcat_ref, const_ref, w2_ref, bias_ref, cov_ref, o_ref, idx_ref, rows_ref = refs[n_pages:]
    n_half = n_pages * HALVES_PER_PAGE
    perm = perm_ref[...]
    group = 8
    for j0 in range(0, n_pages, group):
        tiles = [pages[j][0, kv].reshape(KVROW_W, PAGE_ROWS).astype(BF16)
                 for j in range(j0, j0 + group) for kv in range(A_KV)]
        moved = _dot(jnp.concatenate(tiles, axis=0), perm)
        for i in range(group * A_KV):
            j, kv = j0 + i // A_KV, i % A_KV
            rows = moved[i * KVROW_W:(i + 1) * KVROW_W].T
            for p in range(CMP_STRIDE):
                rows_ref[kv, p, j * HALVES_PER_PAGE:(j + 1) * HALVES_PER_PAGE, :] = (
                    rows[p * HALVES_PER_PAGE:(p + 1) * HALVES_PER_PAGE])

    kc = _compress_halves(lambda p, kv: rows_ref[kv, p],
                          wcat_ref, const_ref, w2_ref, n_half)
    kc0, kc1 = kc[:n_half], kc[n_half:]
    q8 = q_ref[0]
    first = lax.broadcasted_iota(jnp.int32, (A_HEADS, n_half), 0) < A_GROUP

    def logits(keys):
        return lax.dot_general(q8, keys, (((1,), (1,)), ((), ())), preferred_element_type=F32, precision=HIGHEST)

    s = jnp.where(first, logits(kc0[:, :A_HD]), logits(kc1[:, :A_HD])) * ATT_SCALE + bias_ref[...]
    e, _, inv = _softmax_keys_on_lanes(s)
    p = e * inv
    pb = p.astype(BF16)
    first_o = lax.broadcasted_iota(jnp.int32, (A_HEADS, A_HD), 0) < A_GROUP
    o_ref[0] = jnp.where(first_o, _dot(pb, kc0[:, A_HD:].astype(BF16)), _dot(pb, kc1[:, A_HD:].astype(BF16)))
    hrow = lax.broadcasted_iota(jnp.int32, (A_HEADS, n_half), 0)
    imp0 = jnp.sum(jnp.where(first, p, 0.0), axis=0, keepdims=True)
    imp1 = jnp.sum(jnp.where(first, 0.0, p), axis=0, keepdims=True)
    imp = jnp.where(hrow == 0, imp0, jnp.where(hrow == 1, imp1, 0.0))
    score = _dot(imp, cov_ref[...], HIGHEST)
    n_slc = p_len // SLC_BLOCK + 1
    cur = p_len // SLC_BLOCK
    lane = lax.broadcasted_iota(jnp.int32, (A_HEADS, SEL_LANES), 1)
    forced = (lane == 0) | (lane == cur) | (lane == cur - 1)
    valid = lane * SLC_BLOCK <= p_len
    sc = jnp.where(forced, jnp.inf, jnp.where(valid, score, -jnp.inf))
    k_sel = float(min(N_SEL, n_slc))
    sub = lax.broadcasted_iota(jnp.int32, (SEL_LANES, SEL_LANES), 0)
    lan = lax.broadcasted_iota(jnp.int32, (SEL_LANES, SEL_LANES), 1)
    slot_l = lax.broadcasted_iota(jnp.int32, (SEL_LANES, LANE), 1).astype(F32)
    blk_s = lax.broadcasted_iota(jnp.int32, (SEL_LANES, LANE), 0).astype(F32)
    out_row = lax.broadcasted_iota(jnp.int32, (A_HEADS, LANE), 0)
    picks = jnp.zeros((A_HEADS, LANE), F32)
    for kv in range(A_KV):
        row = sc[kv:kv + 1, :]
        col = jnp.sum(jnp.where(sub == lan, row, 0.0), axis=1, keepdims=True)
        before_c = (lan < n_slc) & ((row > col) | ((row == col) & (lan < sub)))
        sel_c = (jnp.sum(before_c.astype(F32), axis=1, keepdims=True) < k_sel) & (sub[:, :1] < n_slc)
        before_r = (sub < n_slc) & ((col > row) | ((col == row) & (sub < lan)))
        sel_r = (jnp.sum(before_r.astype(F32), axis=0, keepdims=True) < k_sel) & (lan[:1] < n_slc)
        slot_c = jnp.sum(jnp.where((lan < sub) & sel_r, 1.0, 0.0), axis=1, keepdims=True)
        hit = sel_c & (slot_c == slot_l)
        picks_kv = jnp.sum(jnp.where(hit, blk_s, 0.0), axis=0, keepdims=True)
        picks = jnp.where(out_row == kv, picks_kv, picks)
    idx_ref[0] = picks.astype(jnp.int32)


def _cmp_sample(page_table, cache_pages, q3, wcat, const_row, w2bd, bias_cs, cov, p_len):
    nb, n_pages = page_table.shape
    n_half = n_pages * HALVES_PER_PAGE

    def page_spec(j):
        return pl.BlockSpec((1, A_KV, 2, A_HD, PAGE_ROWS), lambda b, pt: (pt[b * n_pages + j], 0, 0, 0, 0))

    r = np.arange(PAGE_ROWS)
    perm = np.zeros((PAGE_ROWS, PAGE_ROWS), np.float32)
    perm[r, (r % CMP_STRIDE) * HALVES_PER_PAGE + r // CMP_STRIDE] = 1.0
    const2 = lambda b, pt: (0, 0)
    grid_spec = pltpu.PrefetchScalarGridSpec(
        num_scalar_prefetch=1,
        grid=(nb,),
        in_specs=[page_spec(j) for j in range(n_pages)]
                 + [pl.BlockSpec((1, A_HEADS, A_HD), lambda b, pt: (b, 0, 0)),
                    pl.BlockSpec((PAGE_ROWS, PAGE_ROWS), const2),
                    pl.BlockSpec((CMP_STRIDE * KVROW_W, HALF_W), const2),
                    pl.BlockSpec((1, KVROW_W), const2),
                    pl.BlockSpec((KVROW_W, KVROW_W), const2),
                    pl.BlockSpec((A_HEADS, n_half), const2),
                    pl.BlockSpec((n_half, SEL_LANES), const2)],
        out_specs=[pl.BlockSpec((1, A_HEADS, A_HD), lambda b, pt: (b, 0, 0)),
                   pl.BlockSpec((1, A_HEADS, LANE), lambda b, pt: (b, 0, 0))],
        scratch_shapes=[pltpu.VMEM((A_KV, CMP_STRIDE, n_half, KVROW_W), F32)],
    )
    return pl.pallas_call(
        functools.partial(_cmp_sample_kernel, n_pages=n_pages, p_len=p_len),
        grid_spec=grid_spec,
        out_shape=[jax.ShapeDtypeStruct((nb, A_HEADS, A_HD), F32),
                   jax.ShapeDtypeStruct((nb, A_HEADS, LANE), jnp.int32)],
        compiler_params=_cparams("arbitrary"),
        name="cmp_sample",
    )(page_table.reshape(-1), *([cache_pages] * n_pages), q3, jnp.asarray(perm, BF16), wcat, const_row, w2bd,
      bias_cs, cov)


PAGE_ROWS = 128
BLOCKS_PER_PAGE = PAGE_ROWS // SLC_BLOCK


def _slc_sample_kernel(idx_ref, pt_ref, *refs, p_len):
    del pt_ref
    n_blk = A_KV * N_SEL
    blocks = refs[:n_blk]
    q_ref, snew_ref, win_ref, wnew_ref, wcol_ref, oc_ref, g_ref, rbt_ref, bw_ref, ha_ref, wbuf_ref = refs[n_blk:]
    b = pl.program_id(0)
    past_blocks = p_len // SLC_BLOCK
    n_keys = N_SEL * PAGE_ROWS
    n_buf = win_ref.shape[-1]
    qf = q_ref[0]
    q8 = qf.astype(BF16)
    first_o = lax.broadcasted_iota(jnp.int32, (A_HEADS, A_HD), 0) < A_GROUP
    lane = lax.broadcasted_iota(jnp.int32, (1, n_keys), 1)
    slot = lane // PAGE_ROWS
    in_page = lane % PAGE_ROWS
    bucket_row = lax.broadcasted_iota(jnp.int32, (N_BUCKETS, n_keys), 0).astype(F32)
    bias_new = rbt_ref[:, 0:1]

    def new_key_logit(row_ref, kv):
        k_new = row_ref[0, :, kv * LANE:kv * LANE + A_HD]
        v_new = row_ref[0, :, kv * LANE + A_HD:(kv + 1) * LANE]
        return jnp.sum(qf * k_new, axis=1, keepdims=True) * ATT_SCALE + bias_new, v_new

    o_s, o_w = [], []
    for kv in range(A_KV):
        blk_of = jnp.zeros((1, n_keys), jnp.int32)
        has_new = False
        for j in range(N_SEL):
            blk = idx_ref[(b * A_KV + kv) * N_SEL + j]
            blk_of = jnp.where(slot == j, blk, blk_of)
            has_new = jnp.logical_or(has_new, blk == past_blocks)
        pos = jnp.minimum(blk_of, past_blocks - 1) // BLOCKS_PER_PAGE * PAGE_ROWS + in_page
        valid = (pos // SLC_BLOCK == blk_of) & (pos < p_len)
        onehot = (bucket_row == _bucket_dyn(p_len - pos)).astype(F32)
        bias = _dot(rbt_ref[...], onehot, HIGHEST)
        kt = jnp.concatenate([blocks[kv * N_SEL + j][0, 0, 0] for j in range(N_SEL)], axis=1).astype(BF16)
        vt = jnp.concatenate([blocks[kv * N_SEL + j][0, 0, 1] for j in range(N_SEL)], axis=1).astype(BF16)
        s = jnp.where(valid, _dot(q8, kt) * ATT_SCALE + bias, NEG)
        s_new, v_new = new_key_logit(snew_ref, kv)
        s_new = jnp.where(has_new, s_new, NEG)
        e, e_new, inv = _softmax_keys_on_lanes(s, s_new)
        o_s.append((_nt(e.astype(BF16), vt) + e_new * v_new) * inv)
        sw = _dot(q8, win_ref[0, kv, 0].astype(BF16)) * ATT_SCALE + bw_ref[...]
        sw_new, vw_new = new_key_logit(wnew_ref, kv)
        e, e_new, inv = _softmax_keys_on_lanes(sw, sw_new)
        o_w.append((_nt(e.astype(BF16), win_ref[0, kv, 1].astype(BF16)) + e_new * vw_new) * inv)
    g = jax.nn.sigmoid(g_ref[0])
    ha_ref[0] = (g[0] * oc_ref[0] + g[1] * jnp.where(first_o, o_s[0], o_s[1])
                 + g[2] * jnp.where(first_o, o_w[0], o_w[1]))
    last = lax.broadcasted_iota(jnp.int32, (A_HD, n_buf), 1) == n_buf - 1
    for kv in range(A_KV):
        for c in range(2):
            r0 = (kv * 2 + c) * A_HD
            wbuf_ref[0, kv, c] = jnp.where(last, wcol_ref[0, r0:r0 + A_HD, :],
                                           pltpu.roll(win_ref[0, kv, c], n_buf - 1, 1))


def _slc_sample(idx, page_table, cache_t, q3, slc_new, win_t, win_new, win_new_col, o_c, gates, rb_t, bias_ws, p_len):
    nb, n_pages = page_table.shape
    past_blocks = p_len // SLC_BLOCK
    n_buf = win_t.shape[-1]

    def block_spec(kv, j):
        def index_map(b, idx_ref, pt_ref):
            blk = jnp.minimum(idx_ref[(b * A_KV + kv) * N_SEL + j], past_blocks - 1)
            return (pt_ref[b * n_pages + blk // BLOCKS_PER_PAGE], kv, 0, 0, 0)
        return pl.BlockSpec((1, 1, 2, A_HD, PAGE_ROWS), index_map)

    per_seq3 = lambda b, i, p: (b, 0, 0)
    win_spec = pl.BlockSpec((1, A_KV, 2, A_HD, n_buf), lambda b, i, p: (b, 0, 0, 0, 0))
    grid_spec = pltpu.PrefetchScalarGridSpec(
        num_scalar_prefetch=2,
        grid=(nb,),
        in_specs=[block_spec(kv, j) for kv in range(A_KV) for j in range(N_SEL)]
                 + [pl.BlockSpec((1, A_HEADS, A_HD), per_seq3),
                    pl.BlockSpec((1, 1, ROW_W), per_seq3),
                    win_spec,
                    pl.BlockSpec((1, 1, ROW_W), per_seq3),
                    pl.BlockSpec((1, ROW_W, 1), per_seq3),
                    pl.BlockSpec((1, A_HEADS, A_HD), per_seq3),
                    pl.BlockSpec((1, 3, A_HEADS, 1), lambda b, i, p: (b, 0, 0, 0)),
                    pl.BlockSpec((A_HEADS, N_BUCKETS), lambda b, i, p: (0, 0)),
                    pl.BlockSpec((A_HEADS, n_buf), lambda b, i, p: (0, 0))],
        out_specs=[pl.BlockSpec((1, A_HEADS, A_HD), per_seq3), win_spec],
    )
    return pl.pallas_call(
        functools.partial(_slc_sample_kernel, p_len=p_len),
        grid_spec=grid_spec,
        out_shape=[jax.ShapeDtypeStruct((nb, A_HEADS, A_HD), F32),
                   jax.ShapeDtypeStruct(win_t.shape, F32)],
        compiler_params=_cparams("arbitrary"),
        name="slc_win_sample",
    )(idx.reshape(-1), page_table.reshape(-1), *([cache_t] * (A_KV * N_SEL)),
      q3, slc_new, win_t, win_new, win_new_col, o_c, gates, rb_t, bias_ws)


def _rows_last(a):
    n = a.ndim
    return a.transpose(*range(n - 4), n - 3, n - 2, n - 1, n - 4)


def _rows_first(a):
    n = a.ndim
    return a.transpose(*range(n - 4), n - 1, n - 4, n - 3, n - 2)


def _nsa_sample(rel_bias, pa, rs_new, rw_new, cache_cmp, cache_slc, win_cache, page_table,
                bias, offs, cmp_w, const_row, p_len):
    wcat, w2bd = cmp_w
    nb, n_pages = page_table.shape
    n_half = p_len // CMP_STRIDE
    n_slc = p_len // SLC_BLOCK + 1
    n_buf = win_cache.shape[1]
    q3 = pa[:, :A_WIDTH].reshape(nb, A_HEADS, A_HD)
    gates = pa[:, 2 * A_WIDTH:2 * A_WIDTH + 3 * A_HEADS].reshape(nb, 3, A_HEADS, 1)
    bias_cs = bias[:, offs[6]:offs[7]].reshape(A_HEADS, -1)[:, :n_half]
    bias_ws = bias[:, offs[7]:offs[8]].reshape(A_HEADS, -1)[:, :n_buf]
    cov = jnp.asarray(_cover_np(n_half, n_half - 1, SEL_LANES, n_slc))
    o_c, picks = _cmp_sample(page_table, _rows_last(cache_cmp), q3, wcat, const_row, w2bd, bias_cs, cov, p_len)
    idx = picks[:, :A_KV, :N_SEL]
    ha, wbuf = _slc_sample(idx, page_table, _rows_last(cache_slc), q3, rs_new.reshape(nb, 1, ROW_W),
                           _rows_last(win_cache), rw_new.reshape(nb, 1, ROW_W), rw_new.reshape(nb, ROW_W, 1),
                           o_c, gates, rel_bias.T, bias_ws, p_len)
    return ha.reshape(nb, A_WIDTH), idx, wbuf


def kernel(x_prompt, x_sample, cache_cmp_kv, cache_slc_kv, cache_win_kv, state_mlstm_C, state_mlstm_n, state_mlstm_m, page_table, c_prompt, c_sample, rel_bias, w_ada, b_ada, w_in, b_in, m_norm_g, cmp_pe, cmp_w1, cmp_b1, cmp_w2, w_out, b_out, ln_g, ln_b):
    B, T, _ = x_prompt.shape
    NB = x_sample.shape[0]
    n_pages = page_table.shape[1]
    p_len = n_pages * PAGE_ROWS
    depth = w_in.shape[0]
    assert depth == 1 and x_sample.shape[1] == 1 and cache_win_kv.shape[2] == WINDOW
    ids, offs = _static_ids(p_len)
    bias = _bias_tables(rel_bias, ids)
    x_p = x_prompt.reshape(B * T, D_MODEL)
    x_s = x_sample.reshape(NB, D_MODEL)
    l = 0
    n_mod = -(-(B + NB) // SUBLANE) * SUBLANE
    c_all = jnp.concatenate([c_prompt, c_sample, jnp.zeros((n_mod - B - NB, D_MODEL), F32)])
    shift, scale, gate = jnp.split(_adaln_mod(c_all, w_ada[l], b_ada[l]), 3, axis=-1)
    packed = _pack_in_proj(w_in[l], b_in[l], BF16)
    packed_f32 = _pack_in_proj(w_in[l], b_in[l], F32)
    cmp_w = _pack_compress(cmp_w1[l], cmp_w2[l])
    const_row = _compress_const(cmp_pe[l], cmp_w1[l], cmp_b1[l])
    w_out_b = w_out[l].astype(BF16)
    vecs = (b_out[l].reshape(1, -1), ln_g[l].reshape(1, -1), ln_b[l].reshape(1, -1))
    pm, pa, rc, rs, rw, ct, st, wt = _project(x_p, shift[:B, None], scale[:B, None], packed, B, 256)
    mix_m, c_p, n_p, m_p = _mlstm_prompt(pm, m_norm_g[l], B, T)
    o_c, o_s, o_w, _ = _nsa_prompt(rel_bias, pa, rc, rs, rw, st, wt, bias, offs, cmp_w, const_row, B, T)
    y_p = _out_prompt(x_p, mix_m, o_c, o_s, o_w, pa, gate[:B, None], w_out_b, *vecs, B, T)
    pm_s, pa_s, _, rs_s, rw_s, ct_s, st_s, wt_s = _project(x_s, shift[B:B + NB], scale[B:B + NB], packed_f32, 1, NB)
    mix_s, c_s, n_s, m_s = _mlstm_sample(pm_s, m_norm_g[l], state_mlstm_C[l], state_mlstm_n[l], state_mlstm_m[l])
    ha_s, _, wbuf_s = _nsa_sample(rel_bias, pa_s, rs_s, rw_s, cache_cmp_kv[l], cache_slc_kv[l],
                                  cache_win_kv[l], page_table, bias, offs, cmp_w, const_row, p_len)
    y_s = _out_sample(x_s, mix_s.reshape(NB, M_WIDTH), ha_s, pa_s, gate[B:B + NB], w_out_b, *vecs)

    def kv_prompt(a):
        return _rows_first(a.reshape(1, B, A_KV, 2, A_HD, a.shape[-1]))

    def kv_sample(a):
        return a.reshape(1, 1, A_KV, 2, A_HD, NB).transpose(0, 5, 1, 2, 3, 4)

    return (y_p.reshape(B, T, D_MODEL), y_s.reshape(NB, 1, D_MODEL),
            kv_prompt(ct), kv_sample(ct_s), kv_prompt(st), kv_sample(st_s),
            kv_prompt(wt[:, :, T - WINDOW:]), _rows_first(wbuf_s)[None],
            c_p[None], c_s[None], n_p[None], n_s[None], m_p[None, :, :, 0], m_s[None])
```

```python
import functools
import math

import numpy as np
import jax
import jax.numpy as jnp
from jax import lax
from jax.experimental import pallas as pl
from jax.experimental.pallas import tpu as pltpu

F32 = jnp.float32
BF16 = jnp.bfloat16
HIGHEST = lax.Precision.HIGHEST

D_MODEL = 1024
M_HEADS = 4
M_HD = 128
M_WIDTH = M_HEADS * M_HD
M_CHUNK = 128
A_HEADS = 8
A_HD = 64
A_KV = 2
A_GROUP = A_HEADS // A_KV
A_WIDTH = A_HEADS * A_HD
A_KVW = A_KV * A_HD
ROW_W = 2 * A_KVW
CMP_LEN = 32
CMP_STRIDE = 16
SLC_BLOCK = 64
N_SEL = 16
WINDOW = 512
N_BUCKETS = 32
MAX_EXACT = N_BUCKETS // 2
MAX_DIST = 128
FAR_BUCKET = N_BUCKETS - 1
LN_EPS = 1e-5
ATT_SCALE = A_HD ** -0.5
DEPTH = 1
DEEPNORM_ALPHA = (2.0 * DEPTH) ** 0.25
IN_SPLITS = (M_WIDTH,) * 5 + (M_HEADS, M_HEADS) + (A_WIDTH,) + (A_KVW,) * 6 + (3 * A_HEADS, A_WIDTH)

LANE = 128
SUBLANE = 8
TQ = 128
NEG = -1e30
LOG2E = math.log2(math.e)
MASKED_ID = N_BUCKETS
VMEM_LIMIT = 56 * 1024 * 1024

PM_W = 5 * M_WIDTH + LANE
PA_W = 2 * A_WIDTH + LANE
PW_TOTAL = PM_W + PA_W + 3 * ROW_W


def _cparams(*sem):
    return pltpu.CompilerParams(dimension_semantics=sem, vmem_limit_bytes=VMEM_LIMIT)


def _nt(a, b):
    return lax.dot_general(a, b, (((1,), (1,)), ((), ())), preferred_element_type=F32)


def _dot(a, b, precision=None):
    return jnp.dot(a, b, preferred_element_type=F32, precision=precision)


def _log_sigmoid(x):
    return jnp.minimum(x, 0.0) - jnp.log(1.0 + jnp.exp(-jnp.abs(x)))


def _silu(x):
    return x * jax.nn.sigmoid(x)


def _gelu_tanh(x):
    return 0.5 * x * (1.0 + jnp.tanh(math.sqrt(2.0 / math.pi) * (x + 0.044715 * (x * x * x))))


def _ln_rows(x):
    mu = jnp.mean(x, axis=-1, keepdims=True)
    xc = x - mu
    var = jnp.mean(xc * xc, axis=-1, keepdims=True)
    return xc * lax.rsqrt(var + LN_EPS)


def _bucket_np(dist):
    dist = np.asarray(dist, np.int64)
    n = np.maximum(dist, 0)
    nf = np.maximum(n, 1).astype(np.float32)
    large = MAX_EXACT + (np.log(nf / np.float32(MAX_EXACT)) / np.float32(math.log(MAX_DIST / MAX_EXACT))
                         * np.float32(N_BUCKETS - MAX_EXACT)).astype(np.int32)
    large = np.minimum(large, N_BUCKETS - 1)
    b = np.where(n < MAX_EXACT, n, large)
    return np.where(dist < 0, MASKED_ID, b).astype(np.int32)


def _bucket_dyn(dist):
    n = jnp.maximum(dist, 0)
    nf = jnp.maximum(n, 1).astype(F32)
    large = MAX_EXACT + jnp.floor(jnp.log(nf / MAX_EXACT) / math.log(MAX_DIST / MAX_EXACT)
                                  * (N_BUCKETS - MAX_EXACT))
    large = jnp.minimum(large, float(N_BUCKETS - 1))
    return jnp.where(n < MAX_EXACT, n.astype(F32), large)


def _mod_kernel(c_ref, w_ref, b_ref, o_ref):
    a = _silu(c_ref[...])
    o_ref[...] = _dot(a, w_ref[...]) + b_ref[...]


def _adaln_mod(c, w_ada, b_ada):
    rows = c.shape[0]
    n3 = w_ada.shape[1]
    tn = D_MODEL
    return pl.pallas_call(
        _mod_kernel,
        grid=(n3 // tn,),
        in_specs=[pl.BlockSpec((rows, D_MODEL), lambda j: (0, 0)),
                  pl.BlockSpec((D_MODEL, tn), lambda j: (0, j)),
                  pl.BlockSpec((1, tn), lambda j: (0, j))],
        out_specs=pl.BlockSpec((rows, tn), lambda j: (0, j)),
        out_shape=jax.ShapeDtypeStruct((rows, n3), F32),
        compiler_params=_cparams("arbitrary"),
        name="adaln_mod",
    )(c, w_ada, b_ada.reshape(1, n3))


def _bias_kernel(rb_ref, ids_ref, o_ref, *, n_groups):
    def body(i, carry):
        r0 = pl.multiple_of(i * SUBLANE, SUBLANE)
        ids = ids_ref[pl.ds(r0, SUBLANE), :]
        for h in range(A_HEADS):
            acc = jnp.full((SUBLANE, LANE), NEG, F32)
            for b in range(N_BUCKETS):
                acc = jnp.where(ids == b, rb_ref[b, h], acc)
            o_ref[h, pl.ds(r0, SUBLANE), :] = acc
        return carry

    lax.fori_loop(0, n_groups, body, 0)


def _bias_tables(rel_bias, ids):
    rows = ids.shape[0]
    return pl.pallas_call(
        functools.partial(_bias_kernel, n_groups=rows // SUBLANE),
        in_specs=[pl.BlockSpec(memory_space=pltpu.SMEM),
                  pl.BlockSpec((rows, LANE), lambda: (0, 0))],
        out_specs=pl.BlockSpec((A_HEADS, rows, LANE), lambda: (0, 0, 0)),
        out_shape=jax.ShapeDtypeStruct((A_HEADS, rows, LANE), F32),
        name="bias_tables",
    )(rel_bias, jnp.asarray(ids))


def _pack_in_proj(w_in, b_in, dtype):
    offs = np.cumsum((0,) + IN_SPLITS)
    names = ("mq", "mk", "mv", "mo", "mz", "mi", "mf", "aq", "ck", "cv", "sk", "sv", "wk", "wv", "ga", "za")
    sl = {n: (int(offs[i]), int(offs[i + 1])) for i, n in enumerate(names)}

    def cols(a, name, lo=None, hi=None):
        s, e = sl[name]
        if lo is not None:
            s, e = s + lo, s + hi
        return a[..., s:e]

    def rows_of(a, kn, vn):
        return [cols(a, kn, 0, A_HD), cols(a, vn, 0, A_HD), cols(a, kn, A_HD, 2 * A_HD), cols(a, vn, A_HD, 2 * A_HD)]

    def pack(a):
        def zeros(n):
            return jnp.zeros(a.shape[:-1] + (n,), a.dtype)
        parts = [cols(a, n) for n in ("mq", "mk", "mv", "mo", "mz")]
        parts += [cols(a, "mi"), cols(a, "mf"), zeros(LANE - 2 * M_HEADS)]
        parts += [cols(a, "aq"), cols(a, "za"), cols(a, "ga"), zeros(LANE - 3 * A_HEADS)]
        parts += rows_of(a, "ck", "cv") + rows_of(a, "sk", "sv") + rows_of(a, "wk", "wv")
        return jnp.concatenate(parts, axis=-1)

    w = pack(w_in)
    b = pack(b_in.reshape(1, -1))
    wt = w[:, PM_W + PA_W:].T
    bt = b[:, PM_W + PA_W:].reshape(-1, 1)
    return w.astype(dtype), b, wt.astype(dtype), bt


def _proj_kernel(x_ref, sh_ref, sc_ref, w_ref, b_ref, wt_ref, bt_ref,
                 om_ref, oa_ref, oc_ref, os_ref, ow_ref, oct_ref, ost_ref, owt_ref):
    h = _ln_rows(x_ref[...]) * (1.0 + sc_ref[...]) + sh_ref[...]
    hb = h.astype(w_ref.dtype)
    precision = HIGHEST if w_ref.dtype == F32 else None
    lo = 0
    for o_ref in (om_ref, oa_ref, oc_ref, os_ref, ow_ref):
        n = o_ref.shape[-1]
        o_ref[...] = _dot(hb, w_ref[:, lo:lo + n], precision) + b_ref[:, lo:lo + n]
        lo += n
    t = lax.dot_general(wt_ref[...], hb, (((1,), (1,)), ((), ())), preferred_element_type=F32,
                        precision=precision) + bt_ref[...]
    for i, o_ref in enumerate((oct_ref, ost_ref, owt_ref)):
        o_ref[0] = t[i * ROW_W:(i + 1) * ROW_W]


def _project(x, shift, scale, packed, groups, tm):
    w, b, wt, bt = packed
    rows = x.shape[0]
    per = rows // groups // tm
    if shift.ndim == 3:
        mod_spec = pl.BlockSpec((None, 1, D_MODEL), lambda i: (i // per, 0, 0))
    else:
        mod_spec = pl.BlockSpec((tm, D_MODEL), lambda i: (i, 0))
    widths = (PM_W, PA_W, ROW_W, ROW_W, ROW_W)
    return pl.pallas_call(
        _proj_kernel,
        grid=(rows // tm,),
        in_specs=[pl.BlockSpec((tm, D_MODEL), lambda i: (i, 0)), mod_spec, mod_spec,
                  pl.BlockSpec((D_MODEL, PW_TOTAL), lambda i: (0, 0)),
                  pl.BlockSpec((1, PW_TOTAL), lambda i: (0, 0)),
                  pl.BlockSpec((3 * ROW_W, D_MODEL), lambda i: (0, 0)),
                  pl.BlockSpec((3 * ROW_W, 1), lambda i: (0, 0))],
        out_specs=[pl.BlockSpec((tm, n), lambda i: (i, 0)) for n in widths]
                  + [pl.BlockSpec((1, ROW_W, tm), lambda i: (i // per, 0, i % per))] * 3,
        out_shape=[jax.ShapeDtypeStruct((rows, n), F32) for n in widths]
                  + [jax.ShapeDtypeStruct((groups, ROW_W, rows // groups), F32)] * 3,
        compiler_params=_cparams("arbitrary"),
        name="in_proj",
    )(x, shift, scale, w, b, wt, bt)


def _mlstm_head_out(h, o_pre, z_pre, g_row):
    return jax.nn.sigmoid(o_pre) * (_ln_rows(h) * g_row) * _silu(z_pre)


def _mlstm_prompt_kernel(q_ref, k_ref, v_ref, o_ref, z_ref, g_ref, ng_ref, mix_ref, c_ref, n_ref, m_ref):
    L = M_CHUNK

    @pl.when(pl.program_id(0) == 0)
    def _():
        c_ref[...] = jnp.zeros_like(c_ref)
        n_ref[...] = jnp.zeros_like(n_ref)
        m_ref[...] = jnp.zeros_like(m_ref)

    row = lax.broadcasted_iota(jnp.int32, (L, L), 0)
    col = lax.broadcasted_iota(jnp.int32, (L, L), 1)
    tril = col <= row
    lower = tril.astype(F32)
    upper = (row <= col).astype(F32)
    nb = q_ref.shape[0]
    units = [(b, h) for b in range(nb) for h in range(M_HEADS)]
    gates = [g_ref[b] for b in range(nb)]
    gates_t = [g.T for g in gates]
    cum = [_dot(lower, _log_sigmoid(g), HIGHEST) for g in gates]
    cum_t = [_dot(_log_sigmoid(g), upper, HIGHEST) for g in gates_t]
    st = {}
    for b, h in units:
        hs = slice(h * M_HD, (h + 1) * M_HD)
        b_col = cum[b][:, M_HEADS + h:M_HEADS + h + 1]
        b_row = cum_t[b][M_HEADS + h:M_HEADS + h + 1, :]
        m_prev = m_ref[b, h:h + 1, 0:1]
        d = jnp.where(tril, b_col - b_row + gates_t[b][h:h + 1, :], NEG)
        inter = b_col + m_prev
        m_t = jnp.maximum(inter, jnp.max(d, axis=1, keepdims=True))
        q = q_ref[b, :, hs]
        ks = k_ref[b, :, hs] * (M_HD ** -0.5)
        st[b, h] = dict(hs=hs, b_col=b_col, m_prev=m_prev, d=d, m_t=m_t, w_inter=jnp.exp(inter - m_t), q=q, ks=ks,
                        qb=q.astype(BF16), kb=ks.astype(BF16), vb=v_ref[b, :, hs].astype(BF16),
                        c_prev=c_ref[b, h], n_prev=n_ref[b, h:h + 1, :])
    for u in units:
        s = st[u]
        s["qk"] = _nt(s["qb"], s["kb"]) * jnp.exp(s["d"] - s["m_t"])
        s["qc"] = _dot(s["qb"], s["c_prev"].astype(BF16))
    for u in units:
        s = st[u]
        num = s["w_inter"] * s["qc"] + _dot(s["qk"].astype(BF16), s["vb"])
        den = (s["w_inter"] * jnp.sum(s["q"] * s["n_prev"], axis=1, keepdims=True)
               + jnp.sum(s["qk"], axis=1, keepdims=True))
        s["hh"] = num / jnp.maximum(jnp.abs(den), jnp.exp(-s["m_t"]))
    for (b, h) in units:
        s = st[b, h]
        m_new = s["m_t"][L - 1:L, :]
        b_last = s["b_col"][L - 1:L, :]
        w_c = jnp.exp(b_last + s["m_prev"] - m_new)
        w_s = jnp.exp(b_last - s["b_col"] + gates[b][:, h:h + 1] - m_new)
        kw = s["ks"] * w_s
        c_ref[b, h] = w_c * s["c_prev"] + _dot(kw.T.astype(BF16), s["vb"])
        n_ref[b, h:h + 1, :] = w_c * s["n_prev"] + jnp.sum(kw, axis=0, keepdims=True)
        m_ref[b, h:h + 1, :] = jnp.broadcast_to(m_new, (1, M_HD))
    for (b, h) in units:
        s = st[b, h]
        hs = s["hs"]
        mix_ref[b, :, hs] = _mlstm_head_out(s["hh"], o_ref[b, :, hs], z_ref[b, :, hs], ng_ref[:, hs])


def _mlstm_prompt(pm, norm_g, batch, seq):
    nc = seq // M_CHUNK
    pm3 = pm.reshape(batch, seq, PM_W)

    def col_spec(j, width=M_WIDTH):
        return pl.BlockSpec((batch, M_CHUNK, width), lambda c: (0, c, j))

    state = lambda c: (0, 0, 0)
    mix, c_p, n_p, m_p = pl.pallas_call(
        _mlstm_prompt_kernel,
        grid=(nc,),
        in_specs=[col_spec(0), col_spec(1), col_spec(2), col_spec(3), col_spec(4),
                  pl.BlockSpec((batch, M_CHUNK, LANE), lambda c: (0, c, 5 * M_WIDTH // LANE)),
                  pl.BlockSpec((1, M_WIDTH), lambda c: (0, 0))],
        out_specs=[pl.BlockSpec((batch, M_CHUNK, M_WIDTH), lambda c: (0, c, 0)),
                   pl.BlockSpec((batch, M_HEADS, M_HD, M_HD), lambda c: (0, 0, 0, 0)),
                   pl.BlockSpec((batch, M_HEADS, M_HD), state),
                   pl.BlockSpec((batch, M_HEADS, M_HD), state)],
        out_shape=[jax.ShapeDtypeStruct((batch, seq, M_WIDTH), F32),
                   jax.ShapeDtypeStruct((batch, M_HEADS, M_HD, M_HD), F32),
                   jax.ShapeDtypeStruct((batch, M_HEADS, M_HD), F32),
                   jax.ShapeDtypeStruct((batch, M_HEADS, M_HD), F32)],
        compiler_params=_cparams("arbitrary"),
        name="mlstm_prompt",
    )(pm3, pm3, pm3, pm3, pm3, pm3, norm_g.reshape(1, M_WIDTH))
    return mix.reshape(batch * seq, M_WIDTH), c_p, n_p, m_p


MS_G = 128


def _mlstm_sample_kernel(q_ref, k_ref, v_ref, o_ref, z_ref, ig_ref, fg_ref, m_ref, n_ref, c_ref, ng_ref,
                         mix_ref, co_ref, no_ref, mo_ref):
    q = q_ref[...]
    ks = k_ref[...] * (M_HD ** -0.5)
    v = v_ref[...]
    n_prev = n_ref[...]
    ig = ig_ref[...]
    inter = _log_sigmoid(fg_ref[...]) + m_ref[...]
    m_t = jnp.maximum(inter, ig)
    w_inter = jnp.exp(inter - m_t)
    w_s = jnp.exp(ig - m_t)
    qk = jnp.sum(q * ks, axis=1, keepdims=True) * w_s
    q_t = q.T
    kw_t = (ks * w_s).T
    rows = []
    for r in range(MS_G):
        c_prev = c_ref[r]
        rows.append(jnp.sum(q_t[:, r:r + 1] * c_prev, axis=0, keepdims=True))
        co_ref[r] = w_inter[r:r + 1, :] * c_prev + kw_t[:, r:r + 1] * v[r:r + 1, :]
    q_c = jnp.concatenate(rows, axis=0)
    num = w_inter * q_c + qk * v
    den = w_inter * jnp.sum(q * n_prev, axis=1, keepdims=True) + qk
    hh = num / jnp.maximum(jnp.abs(den), jnp.exp(-m_t))
    no_ref[...] = w_inter * n_prev + w_s * ks
    mo_ref[...] = m_t
    mix_ref[...] = _mlstm_head_out(hh, o_ref[...], z_ref[...], ng_ref[...])


def _mlstm_sample(pm, norm_g, c0, n0, m0):
    nb = pm.shape[0]
    rows = nb * M_HEADS

    def head_rows(j):
        return pm[:, j * M_WIDTH:(j + 1) * M_WIDTH].reshape(rows, M_HD)

    gates = pm[:, 5 * M_WIDTH:5 * M_WIDTH + 2 * M_HEADS]
    ig = gates[:, :M_HEADS].reshape(rows, 1)
    fg = gates[:, M_HEADS:].reshape(rows, 1)
    ng_rows = jnp.tile(norm_g.reshape(M_HEADS, M_HD), (MS_G // M_HEADS, 1))
    vec = pl.BlockSpec((MS_G, M_HD), lambda i: (i, 0))
    one = pl.BlockSpec((MS_G, 1), lambda i: (i, 0))
    mat = pl.BlockSpec((MS_G, M_HD, M_HD), lambda i: (i, 0, 0))
    mix, c1, n1, m1 = pl.pallas_call(
        _mlstm_sample_kernel,
        grid=(rows // MS_G,),
        in_specs=[vec] * 5 + [one] * 3 + [vec, mat, pl.BlockSpec((MS_G, M_HD), lambda i: (0, 0))],
        out_specs=[vec, mat, vec, one],
        out_shape=[jax.ShapeDtypeStruct((rows, M_HD), F32),
                   jax.ShapeDtypeStruct((rows, M_HD, M_HD), F32),
                   jax.ShapeDtypeStruct((rows, M_HD), F32),
                   jax.ShapeDtypeStruct((rows, 1), F32)],
        compiler_params=_cparams("arbitrary"),
        name="mlstm_sample",
    )(*[head_rows(j) for j in range(5)], ig, fg, m0.reshape(rows, 1), n0.reshape(rows, M_HD),
      c0.reshape(rows, M_HD, M_HD), ng_rows)
    return (mix.reshape(nb, M_WIDTH), c1.reshape(nb, M_HEADS, M_HD, M_HD), n1.reshape(nb, M_HEADS, M_HD),
            m1.reshape(nb, M_HEADS))


KVROW_W = 2 * A_HD
HALF_W = 2 * KVROW_W


def _pack_compress(w1, w2):
    def block_diag(k, v):
        z = jnp.zeros_like(k)
        return jnp.concatenate([jnp.concatenate([k, z], axis=-1), jnp.concatenate([z, v], axis=-1)], axis=-2)

    wbd = block_diag(w1[0], w1[1])
    wcat = jnp.concatenate([wbd[:CMP_STRIDE], wbd[CMP_STRIDE:]], axis=-1)
    return wcat.reshape(CMP_STRIDE * KVROW_W, HALF_W).astype(BF16), block_diag(w2[0], w2[1]).astype(BF16)


def _cmp_const_kernel(pe_ref, w_ref, b_ref, o_ref):
    for c in range(2):
        o_ref[c] = _dot(pe_ref[c], w_ref[c], HIGHEST) + b_ref[c]


def _compress_const(pe, w1, b1):
    k = CMP_LEN * A_HD
    pe8 = jnp.broadcast_to(pe.reshape(2, 1, k), (2, SUBLANE, k))
    out = pl.pallas_call(
        _cmp_const_kernel,
        out_shape=jax.ShapeDtypeStruct((2, SUBLANE, A_HD), F32),
        name="compress_const",
    )(pe8, w1.reshape(2, k, A_HD), b1.reshape(2, 1, A_HD))
    return jnp.concatenate([out[0, 0:1], out[1, 0:1]], axis=-1)


def _compress_halves(halves, wcat_ref, const_ref, w2_ref, n_half):
    acc = _dot(halves.astype(BF16), wcat_ref[...])
    pre = acc[:, :KVROW_W] + pltpu.roll(acc[:, KVROW_W:], A_KV * n_half - 1, 0) + const_ref[...]
    return _dot(_gelu_tanh(pre).astype(BF16), w2_ref[...])


def _compress_prompt_kernel(x0_ref, x1_ref, wcat_ref, const_ref, w2_ref, kk_ref, kvt_ref, *, n_half):
    x_refs = (x0_ref, x1_ref)
    halves = jnp.concatenate(
        [jnp.concatenate([x_refs[kv][pl.ds(p, n_half, stride=CMP_STRIDE), :] for kv in range(A_KV)], axis=0)
         for p in range(CMP_STRIDE)], axis=1)
    kc = _compress_halves(halves, wcat_ref, const_ref, w2_ref, n_half)
    kct = kc.T
    for kv in range(A_KV):
        kk_ref[0, kv] = kc[kv * n_half:(kv + 1) * n_half, 0:A_HD]
        kvt_ref[0, kv] = kct[A_HD:, kv * n_half:(kv + 1) * n_half]


def _compress_prompt(rows, wcat, const_row, w2bd, batch, seq):
    n_half = seq // CMP_STRIDE
    return pl.pallas_call(
        functools.partial(_compress_prompt_kernel, n_half=n_half),
        grid=(batch,),
        in_specs=[pl.BlockSpec((seq, KVROW_W), lambda b: (b, 0)),
                  pl.BlockSpec((seq, KVROW_W), lambda b: (b, 1)),
                  pl.BlockSpec((CMP_STRIDE * KVROW_W, HALF_W), lambda b: (0, 0)),
                  pl.BlockSpec((1, KVROW_W), lambda b: (0, 0)),
                  pl.BlockSpec((KVROW_W, KVROW_W), lambda b: (0, 0))],
        out_specs=[pl.BlockSpec((1, A_KV, n_half, A_HD), lambda b: (b, 0, 0, 0)),
                   pl.BlockSpec((1, A_KV, A_HD, n_half), lambda b: (b, 0, 0, 0))],
        out_shape=[jax.ShapeDtypeStruct((batch, A_KV, n_half, A_HD), F32),
                   jax.ShapeDtypeStruct((batch, A_KV, A_HD, n_half), F32)],
        compiler_params=_cparams("arbitrary"),
        name="compress_prompt",
    )(rows, rows, wcat, const_row, w2bd)


CMP_PAT = 16


def _static_ids(p_len):
    i = np.arange(TQ)[None, :]
    c = np.arange(CMP_PAT)[:, None]
    cmp_a = _bucket_np(i + (TQ - (CMP_LEN - 1)) - CMP_STRIDE * c)
    cmp_b = _bucket_np(i - CMP_STRIDE * c - (CMP_LEN - 1))
    r = np.arange(TQ)[:, None]
    slc_diag = _bucket_np(i - r)
    slc_sub = _bucket_np(TQ + i - r)
    slc_far = np.full((TQ, TQ), FAR_BUCKET, np.int32)
    rw = np.arange(WINDOW + TQ)[:, None]
    dw = WINDOW + i - rw
    win = np.where(dw > WINDOW, MASKED_ID, _bucket_np(dw))
    n_half = p_len // CMP_STRIDE
    n = np.arange(n_half)
    cs = _bucket_np(p_len - (CMP_STRIDE * n + CMP_LEN - 1))
    cs[n_half - 1] = MASKED_ID
    cs_rows = -(-n_half // LANE)
    cs_pad = np.full((cs_rows * LANE,), MASKED_ID, np.int32)
    cs_pad[:n_half] = cs
    ws = _bucket_np(WINDOW - np.arange(WINDOW))
    parts = [cmp_a, cmp_b, slc_diag, slc_sub, slc_far, win, cs_pad.reshape(cs_rows, LANE),
             ws.reshape(WINDOW // LANE, LANE)]
    offs = np.cumsum([0] + [p.shape[0] for p in parts])
    total = -(-int(offs[-1]) // SUBLANE) * SUBLANE
    ids = np.full((total, LANE), MASKED_ID, np.int32)
    ids[:offs[-1]] = np.concatenate(parts, axis=0)
    return ids, [int(o) for o in offs]


def _cover_np(n_cmp_rows, n_cmp, n_slc_rows, n_slc):
    cs = np.arange(n_cmp_rows)[:, None] * CMP_STRIDE
    ss = np.arange(n_slc_rows)[None, :] * SLC_BLOCK
    cov = (cs <= ss + SLC_BLOCK - 1) & (cs + CMP_LEN - 1 >= ss)
    cov &= (np.arange(n_cmp_rows)[:, None] < n_cmp) & (np.arange(n_slc_rows)[None, :] < n_slc)
    return cov.astype(np.float32)


def _softmax_keys_on_rows(s):
    m = jnp.max(s, axis=0, keepdims=True)
    m = jnp.where(m > 0.5 * NEG, m, 0.0)
    e = jnp.exp(s - m)
    tot = jnp.sum(e, axis=0, keepdims=True)
    return e / jnp.where(tot > 0.0, tot, 1.0)


def _softmax_keys_on_lanes(s, s_new=None):
    m = jnp.max(s, axis=1, keepdims=True)
    if s_new is not None:
        m = jnp.maximum(m, s_new)
    m = jnp.where(m > 0.5 * NEG, m, 0.0)
    e = jnp.exp(s - m)
    tot = jnp.sum(e, axis=1, keepdims=True)
    e_new = None
    if s_new is not None:
        e_new = jnp.exp(s_new - m)
        tot = tot + e_new
    return e, e_new, 1.0 / jnp.where(tot > 0.0, tot, 1.0)


def _cmp_attend_kernel(far_ref, q_ref, kk_ref, kvt_ref, pt_ref, cov_ref, o_ref, sel_ref, bscr, *, nc, ns):
    k = pl.program_id(1)
    start = pl.multiple_of(jnp.maximum(SUBLANE * k - SUBLANE, 0), SUBLANE)
    variant = jnp.where(k == 0, 1, 0)
    row = lax.broadcasted_iota(jnp.int32, (nc, GROUP_LANES), 0)
    t = k * TQ + lax.broadcasted_iota(jnp.int32, (ns, TQ), 1)
    blk = lax.broadcasted_iota(jnp.int32, (ns, TQ), 0)
    cur = t // SLC_BLOCK
    valid = blk * SLC_BLOCK <= t
    forced = (blk == 0) | (blk == cur) | (blk == cur - 1)
    kvs = range(A_KV)
    for kv in kvs:
        bscr[kv] = jnp.where(row < start, far_ref[kv], NEG)
        bscr[kv, pl.ds(start, CMP_PAT), :] = pt_ref[variant, kv]
    s = [_nt(kk_ref[0, kv].astype(BF16), _stacked_queries(q_ref, kv)) + bscr[kv] for kv in kvs]
    p = [_softmax_keys_on_rows(s[kv]) for kv in kvs]
    o = [_dot(kvt_ref[0, kv].astype(BF16), p[kv].astype(BF16)) for kv in kvs]
    for kv in kvs:
        for g in range(A_GROUP):
            h = kv * A_GROUP + g
            o_ref[0, h * A_HD:(h + 1) * A_HD, :] = o[kv][:, g * TQ:(g + 1) * TQ]
    imp = [sum(p[kv][:, g * TQ:(g + 1) * TQ] for g in range(A_GROUP)) for kv in kvs]
    sc = [jnp.where(forced, jnp.inf, jnp.where(valid, _dot(cov_ref[...], imp[kv], HIGHEST), -jnp.inf)) for kv in kvs]
    cnt = [jnp.zeros((ns, TQ), jnp.int32) for _ in kvs]
    for j in range(ns):
        for kv in kvs:
            r = sc[kv][j:j + 1, :]
            before = (r > sc[kv]) | ((r == sc[kv]) & (blk > j))
            cnt[kv] = cnt[kv] + before.astype(jnp.int32)
    for kv in kvs:
        sel_ref[0, kv] = jnp.where(cnt[kv] < N_SEL, 0.0, NEG)


def _cmp_attend(far, pa, kk, kvt, pat, cov_t, batch, seq):
    nq = seq // TQ
    nc = seq // CMP_STRIDE
    ns = seq // SLC_BLOCK
    return pl.pallas_call(
        functools.partial(_cmp_attend_kernel, nc=nc, ns=ns),
        grid=(batch, nq),
        in_specs=[pl.BlockSpec((A_KV, 1, GROUP_LANES), lambda b, k: (0, 0, 0)),
                  pl.BlockSpec((TQ, A_WIDTH), lambda b, k: (b * nq + k, 0)),
                  pl.BlockSpec((1, A_KV, nc, A_HD), lambda b, k: (b, 0, 0, 0)),
                  pl.BlockSpec((1, A_KV, A_HD, nc), lambda b, k: (b, 0, 0, 0)),
                  pl.BlockSpec((2, A_KV, CMP_PAT, GROUP_LANES), lambda b, k: (0, 0, 0, 0)),
                  pl.BlockSpec((ns, nc), lambda b, k: (0, 0))],
        out_specs=[pl.BlockSpec((1, A_WIDTH, TQ), lambda b, k: (b, 0, k)),
                   pl.BlockSpec((1, A_KV, ns, TQ), lambda b, k: (b, 0, 0, k))],
        out_shape=[jax.ShapeDtypeStruct((batch, A_WIDTH, seq), F32),
                   jax.ShapeDtypeStruct((batch, A_KV, ns, seq), F32)],
        scratch_shapes=[pltpu.VMEM((A_KV, nc, GROUP_LANES), F32)],
        compiler_params=_cparams("arbitrary", "arbitrary"),
        name="cmp_attend",
    )(far, pa, kk, kvt, pat, cov_t)


SLC_CK = 4 * TQ
GROUP_LANES = A_GROUP * TQ
SLC_CLASSES = 4


def _stacked_queries(q_ref, kv, scale=ATT_SCALE):
    heads = [q_ref[:, (kv * A_GROUP + g) * A_HD:(kv * A_GROUP + g + 1) * A_HD] for g in range(A_GROUP)]
    return (jnp.concatenate(heads, axis=0) * scale).astype(BF16)


def _slc_attend_kernel(q_ref, k_ref, vt_ref, sel_ref, tab_ref, o_ref):
    k = pl.program_id(1)
    sub = SLC_CK // TQ
    q4 = [_stacked_queries(q_ref, kv, ATT_SCALE * LOG2E) for kv in range(A_KV)]
    upper = lax.broadcasted_iota(jnp.int32, (TQ, GROUP_LANES), 0) < SLC_BLOCK

    def body(j, carry):
        j0 = pl.multiple_of(j * SLC_CK, SLC_CK)
        kvs = range(A_KV)
        s_all = [_nt(k_ref[pl.ds(j0, SLC_CK), kv * LANE:kv * LANE + A_HD].astype(BF16), q4[kv]) for kv in kvs]
        vt = [vt_ref[0, kv * LANE + A_HD:(kv + 1) * LANE, pl.ds(j0, SLC_CK)].astype(BF16) for kv in kvs]
        s = []
        for kv in kvs:
            parts = []
            for u in range(sub):
                jj = j * sub + u
                cls = jnp.where(jj > k, SLC_CLASSES - 1, jnp.minimum(k - jj, 2))
                sel0 = jnp.concatenate([sel_ref[0, kv, pl.ds(2 * jj, 1), :]] * A_GROUP, axis=1)
                sel1 = jnp.concatenate([sel_ref[0, kv, pl.ds(2 * jj + 1, 1), :]] * A_GROUP, axis=1)
                parts.append(s_all[kv][u * TQ:(u + 1) * TQ] + tab_ref[cls, kv] + jnp.where(upper, sel0, sel1))
            s.append(jnp.concatenate(parts, axis=0))
        m_new = [jnp.maximum(carry[kv][0], jnp.max(s[kv], axis=0, keepdims=True)) for kv in kvs]
        p = [jnp.exp2(s[kv] - m_new[kv]) for kv in kvs]
        pv = [_dot(vt[kv], p[kv].astype(BF16)) for kv in kvs]
        out = []
        for kv in kvs:
            m_run, l_run, acc = carry[kv]
            alpha = jnp.exp2(m_run - m_new[kv])
            l_new = alpha * l_run + jnp.sum(p[kv], axis=0, keepdims=True)
            out.append((m_new[kv], l_new, alpha * acc + pv[kv]))
        return tuple(out)

    init = tuple((jnp.full((1, GROUP_LANES), NEG, F32), jnp.zeros((1, GROUP_LANES), F32),
                  jnp.zeros((A_HD, GROUP_LANES), F32)) for _ in range(A_KV))
    res = lax.fori_loop(0, (k + sub) // sub, body, init)
    for kv in range(A_KV):
        _, l_run, acc = res[kv]
        o = acc / l_run
        for g in range(A_GROUP):
            h = kv * A_GROUP + g
            o_ref[0, h * A_HD:(h + 1) * A_HD, :] = o[:, g * TQ:(g + 1) * TQ]


def _slc_attend(pa, rows, rows_t, sel, tab, batch, seq):
    nq = seq // TQ
    ns = seq // SLC_BLOCK
    return pl.pallas_call(
        _slc_attend_kernel,
        grid=(batch, nq),
        in_specs=[pl.BlockSpec((TQ, A_WIDTH), lambda b, k: (b * nq + k, 0)),
                  pl.BlockSpec((seq, ROW_W), lambda b, k: (b, 0)),
                  pl.BlockSpec((1, ROW_W, seq), lambda b, k: (b, 0, 0)),
                  pl.BlockSpec((1, A_KV, ns, TQ), lambda b, k: (b, 0, 0, k)),
                  pl.BlockSpec((SLC_CLASSES, A_KV, TQ, GROUP_LANES), lambda b, k: (0, 0, 0, 0))],
        out_specs=pl.BlockSpec((1, A_WIDTH, TQ), lambda b, k: (b, 0, k)),
        out_shape=jax.ShapeDtypeStruct((batch, A_WIDTH, seq), F32),
        compiler_params=_cparams("arbitrary", "arbitrary"),
        name="slc_attend",
    )(pa, rows, rows_t, sel, tab)


WIN_SPAN = WINDOW + TQ


def _win_attend_kernel(q_ref, k_ref, vt_ref, bias_ref, o_ref):
    k = pl.program_id(1)
    r0 = pl.multiple_of(k * TQ, TQ)
    exists = lax.broadcasted_iota(jnp.int32, (WIN_SPAN, GROUP_LANES), 0) + k * TQ >= WINDOW
    kvs = range(A_KV)
    s = [_nt(k_ref[0, pl.ds(r0, WIN_SPAN), kv * LANE:kv * LANE + A_HD].astype(BF16), _stacked_queries(q_ref, kv))
         + bias_ref[kv] for kv in kvs]
    p = [_softmax_keys_on_rows(jnp.where(exists, s[kv], NEG)) for kv in kvs]
    o = [_dot(vt_ref[0, kv * LANE + A_HD:(kv + 1) * LANE, pl.ds(r0, WIN_SPAN)].astype(BF16), p[kv].astype(BF16))
         for kv in kvs]
    for kv in kvs:
        for g in range(A_GROUP):
            h = kv * A_GROUP + g
            o_ref[0, h * A_HD:(h + 1) * A_HD, :] = o[kv][:, g * TQ:(g + 1) * TQ]


def _win_attend(pa, rows_pad, rows_t_pad, bias_w, batch, seq):
    nq = seq // TQ
    return pl.pallas_call(
        _win_attend_kernel,
        grid=(batch, nq),
        in_specs=[pl.BlockSpec((TQ, A_WIDTH), lambda b, k: (b * nq + k, 0)),
                  pl.BlockSpec((1, seq + WINDOW, ROW_W), lambda b, k: (b, 0, 0)),
                  pl.BlockSpec((1, ROW_W, seq + WINDOW), lambda b, k: (b, 0, 0)),
                  pl.BlockSpec((A_KV, WIN_SPAN, GROUP_LANES), lambda b, k: (0, 0, 0))],
        out_specs=pl.BlockSpec((1, A_WIDTH, TQ), lambda b, k: (b, 0, k)),
        out_shape=jax.ShapeDtypeStruct((batch, A_WIDTH, seq), F32),
        compiler_params=_cparams("arbitrary", "arbitrary"),
        name="win_attend",
    )(pa, rows_pad, rows_t_pad, bias_w)


def _out_tail(x, mix_m, mix_a, gate, w_ref, b_ref, g_ref, beta_ref):
    y = (_dot(mix_m.astype(BF16), w_ref[:M_WIDTH]) + _dot(mix_a.astype(BF16), w_ref[M_WIDTH:]) + b_ref[...])
    return _ln_rows(DEEPNORM_ALPHA * x + gate * y) * g_ref[...] + beta_ref[...]


OUT_TM = 512


def _out_prompt_kernel(x_ref, mm_ref, oc_ref, os_ref, ow_ref, ga_ref, za_ref, gate_ref,
                       w_ref, b_ref, g_ref, beta_ref, y_ref):
    sig = jax.nn.sigmoid(ga_ref[...].T)
    parts = []
    for h in range(A_HEADS):
        hs = slice(h * A_HD, (h + 1) * A_HD)
        parts.append(sig[h:h + 1] * oc_ref[0, hs, :] + sig[A_HEADS + h:A_HEADS + h + 1] * os_ref[0, hs, :]
                     + sig[2 * A_HEADS + h:2 * A_HEADS + h + 1] * ow_ref[0, hs, :])
    ha = jnp.concatenate(parts, axis=0).T
    mix_a = ha * _silu(za_ref[...])
    y_ref[...] = _out_tail(x_ref[...], mm_ref[...], mix_a, gate_ref[...], w_ref, b_ref, g_ref, beta_ref)


def _out_prompt(x, mix_m, o_c, o_s, o_w, pa, gate, w_out, b_out, ln_g, ln_b, batch, seq):
    tm = OUT_TM
    nq = seq // tm
    rows = batch * seq
    branch = pl.BlockSpec((1, A_WIDTH, tm), lambda i: (i // nq, 0, i % nq))
    vec = pl.BlockSpec((1, D_MODEL), lambda i: (0, 0))
    return pl.pallas_call(
        _out_prompt_kernel,
        grid=(rows // tm,),
        in_specs=[pl.BlockSpec((tm, D_MODEL), lambda i: (i, 0)),
                  pl.BlockSpec((tm, M_WIDTH), lambda i: (i, 0)),
                  branch, branch, branch,
                  pl.BlockSpec((tm, LANE), lambda i: (i, 2 * A_WIDTH // LANE)),
                  pl.BlockSpec((tm, A_WIDTH), lambda i: (i, 1)),
                  pl.BlockSpec((None, 1, D_MODEL), lambda i: (i // nq, 0, 0)),
                  pl.BlockSpec((D_MODEL, D_MODEL), lambda i: (0, 0)),
                  vec, vec, vec],
        out_specs=pl.BlockSpec((tm, D_MODEL), lambda i: (i, 0)),
        out_shape=jax.ShapeDtypeStruct((rows, D_MODEL), F32),
        compiler_params=_cparams("arbitrary"),
        name="out_prompt",
    )(x, mix_m, o_c, o_s, o_w, pa, pa, gate, w_out, b_out, ln_g, ln_b)


def _out_sample_kernel(x_ref, mm_ref, ha_ref, za_ref, gate_ref, w_ref, b_ref, g_ref, beta_ref, y_ref):
    mix_a = ha_ref[...] * _silu(za_ref[...])
    y_ref[...] = _out_tail(x_ref[...], mm_ref[...], mix_a, gate_ref[...], w_ref, b_ref, g_ref, beta_ref)


def _out_sample(x, mix_m, ha, pa, gate, w_out, b_out, ln_g, ln_b):
    rows = x.shape[0]
    vec = pl.BlockSpec((1, D_MODEL), lambda i: (0, 0))
    return pl.pallas_call(
        _out_sample_kernel,
        grid=(1,),
        in_specs=[pl.BlockSpec((rows, D_MODEL), lambda i: (0, 0)),
                  pl.BlockSpec((rows, M_WIDTH), lambda i: (0, 0)),
                  pl.BlockSpec((rows, A_WIDTH), lambda i: (0, 0)),
                  pl.BlockSpec((rows, A_WIDTH), lambda i: (0, 1)),
                  pl.BlockSpec((rows, D_MODEL), lambda i: (0, 0)),
                  pl.BlockSpec((D_MODEL, D_MODEL), lambda i: (0, 0)),
                  vec, vec, vec],
        out_specs=pl.BlockSpec((rows, D_MODEL), lambda i: (0, 0)),
        out_shape=jax.ShapeDtypeStruct((rows, D_MODEL), F32),
        compiler_params=_cparams("arbitrary"),
        name="out_sample",
    )(x, mix_m, ha, pa, gate, w_out, b_out, ln_g, ln_b)


def _nsa_prompt(rel_bias, pa, rc, rs, rw, st, wt, bias, offs, cmp_w, const_row, batch, seq):
    wcat, w2bd = cmp_w
    nc = seq // CMP_STRIDE
    ns = seq // SLC_BLOCK
    kk, kvt = _compress_prompt(rc, wcat, const_row, w2bd, batch, seq)

    def group_lanes(tiles):
        rows = tiles.shape[1]
        return tiles.reshape(A_KV, A_GROUP, rows, TQ).transpose(0, 2, 1, 3).reshape(A_KV, rows, GROUP_LANES)

    pat = group_lanes(bias[:, offs[0]:offs[2]]).reshape(A_KV, 2, CMP_PAT, GROUP_LANES).transpose(1, 0, 2, 3)
    far = jnp.repeat(rel_bias[FAR_BUCKET].reshape(A_KV, A_GROUP), TQ, axis=1).reshape(A_KV, 1, GROUP_LANES)
    cov_t = jnp.asarray(_cover_np(nc, nc - 1, ns, ns).T)
    o_c, sel = _cmp_attend(far, pa, kk, kvt, pat, cov_t, batch, seq)

    tab = group_lanes(bias[:, offs[2]:offs[5]]).reshape(A_KV, 3, TQ, GROUP_LANES).transpose(1, 0, 2, 3)
    tab = jnp.concatenate([tab * LOG2E, jnp.full((1,) + tab.shape[1:], NEG, F32)], axis=0)
    o_s = _slc_attend(pa, rs, st, sel, tab, batch, seq)
    rows_pad = jnp.pad(rw.reshape(batch, seq, ROW_W), ((0, 0), (WINDOW, 0), (0, 0)))
    rows_t_pad = jnp.pad(wt, ((0, 0), (0, 0), (WINDOW, 0)))
    o_w = _win_attend(pa, rows_pad, rows_t_pad, group_lanes(bias[:, offs[5]:offs[6]]), batch, seq)
    return o_c, o_s, o_w, sel


HALVES_PER_PAGE = 8
SEL_LANES = 256


def _gather_pages(copy_of, n_copies):
    b = pl.program_id(0)
    slot = b % 2

    @pl.when(b == 0)
    def _():
        for i in range(n_copies):
            copy_of(0, i, 0).start()

    @pl.when(b + 1 < pl.num_programs(0))
    def _():
        for i in range(n_copies):
            copy_of(b + 1, i, 1 - slot).start()

    for i in range(n_copies):
        copy_of(b, i, slot).wait()
    return slot


def _cmp_sample_kernel(pt_ref, cache_ref, q_ref, perm_ref, wcat_ref, const_ref, w2_ref, bias_ref, cov_ref,
                       o_ref, idx_ref, buf, sem, *, n_pages, p_len):
    def page_copy(seq, j, slot):
        return pltpu.make_async_copy(cache_ref.at[pt_ref[seq * n_pages + j]], buf.at[slot, j], sem.at[slot])

    slot = _gather_pages(page_copy, n_pages)
    pages = [buf.at[slot, j] for j in range(n_pages)]
    n_half = n_pages * HALVES_PER_PAGE
    perm = perm_ref[...]
    group = 8

    def move_lanes(g):
        tiles = [pages[j][kv].reshape(KVROW_W, PAGE_ROWS).astype(BF16)
                 for j in range(g * group, (g + 1) * group) for kv in range(A_KV)]
        return _dot(jnp.concatenate(tiles, axis=0), perm)

    blocks = [[], []]
    n_groups = n_pages // group
    moved = move_lanes(0)
    for g in range(n_groups):
        moved_next = move_lanes(g + 1) if g + 1 < n_groups else None
        for i in range(group * A_KV):
            kv = i % A_KV
            rows = moved[i * KVROW_W:(i + 1) * KVROW_W].T
            blocks[kv].append(jnp.concatenate(
                [rows[p * HALVES_PER_PAGE:(p + 1) * HALVES_PER_PAGE] for p in range(CMP_STRIDE)], axis=1))
        moved = moved_next
    halves = jnp.concatenate(blocks[0] + blocks[1], axis=0)
    kc = _compress_halves(halves, wcat_ref, const_ref, w2_ref, n_half)
    kc0, kc1 = kc[:n_half], kc[n_half:]
    q8 = q_ref[0]
    first = lax.broadcasted_iota(jnp.int32, (A_HEADS, n_half), 0) < A_GROUP

    def logits(keys):
        return lax.dot_general(q8, keys, (((1,), (1,)), ((), ())), preferred_element_type=F32, precision=HIGHEST)

    s = jnp.where(first, logits(kc0[:, :A_HD]), logits(kc1[:, :A_HD])) * ATT_SCALE + bias_ref[...]
    e, _, inv = _softmax_keys_on_lanes(s)
    p = e * inv
    pb = p.astype(BF16)
    first_o = lax.broadcasted_iota(jnp.int32, (A_HEADS, A_HD), 0) < A_GROUP
    o_ref[0] = jnp.where(first_o, _dot(pb, kc0[:, A_HD:].astype(BF16)), _dot(pb, kc1[:, A_HD:].astype(BF16)))
    hrow = lax.broadcasted_iota(jnp.int32, (A_HEADS, n_half), 0)
    imp0 = jnp.sum(jnp.where(first, p, 0.0), axis=0, keepdims=True)
    imp1 = jnp.sum(jnp.where(first, 0.0, p), axis=0, keepdims=True)
    imp = jnp.where(hrow == 0, imp0, jnp.where(hrow == 1, imp1, 0.0))
    score = _dot(imp, cov_ref[...], HIGHEST)
    n_slc = p_len // SLC_BLOCK + 1
    cur = p_len // SLC_BLOCK
    lane = lax.broadcasted_iota(jnp.int32, (A_HEADS, SEL_LANES), 1)
    forced = (lane == 0) | (lane == cur) | (lane == cur - 1)
    valid = lane * SLC_BLOCK <= p_len
    sc = jnp.where(forced, jnp.inf, jnp.where(valid, score, -jnp.inf))
    k_sel = float(min(N_SEL, n_slc))
    sub = lax.broadcasted_iota(jnp.int32, (SEL_LANES, SEL_LANES), 0)
    lan = lax.broadcasted_iota(jnp.int32, (SEL_LANES, SEL_LANES), 1)
    slot_l = lax.broadcasted_iota(jnp.int32, (SEL_LANES, LANE), 1).astype(F32)
    blk_s = lax.broadcasted_iota(jnp.int32, (SEL_LANES, LANE), 0).astype(F32)
    out_row = lax.broadcasted_iota(jnp.int32, (A_HEADS, LANE), 0)
    picks = jnp.zeros((A_HEADS, LANE), F32)
    for kv in range(A_KV):
        row = sc[kv:kv + 1, :]
        col = jnp.sum(jnp.where(sub == lan, row, 0.0), axis=1, keepdims=True)
        before_c = (lan < n_slc) & ((row > col) | ((row == col) & (lan < sub)))
        sel_c = (jnp.sum(before_c.astype(F32), axis=1, keepdims=True) < k_sel) & (sub[:, :1] < n_slc)
        before_r = (sub < n_slc) & ((col > row) | ((col == row) & (sub < lan)))
        sel_r = (jnp.sum(before_r.astype(F32), axis=0, keepdims=True) < k_sel) & (lan[:1] < n_slc)
        slot_c = jnp.sum(jnp.where((lan < sub) & sel_r, 1.0, 0.0), axis=1, keepdims=True)
        hit = sel_c & (slot_c == slot_l)
        picks_kv = jnp.sum(jnp.where(hit, blk_s, 0.0), axis=0, keepdims=True)
        picks = jnp.where(out_row == kv, picks_kv, picks)
    idx_ref[0] = picks.astype(jnp.int32)


def _cmp_sample(page_table, cache_pages, q3, wcat, const_row, w2bd, bias_cs, cov, p_len):
    nb, n_pages = page_table.shape
    n_half = n_pages * HALVES_PER_PAGE

    r = np.arange(PAGE_ROWS)
    perm = np.zeros((PAGE_ROWS, PAGE_ROWS), np.float32)
    perm[r, (r % CMP_STRIDE) * HALVES_PER_PAGE + r // CMP_STRIDE] = 1.0
    const2 = lambda b, pt: (0, 0)
    grid_spec = pltpu.PrefetchScalarGridSpec(
        num_scalar_prefetch=1,
        grid=(nb,),
        in_specs=[pl.BlockSpec(memory_space=pl.ANY),
                  pl.BlockSpec((1, A_HEADS, A_HD), lambda b, pt: (b, 0, 0)),
                  pl.BlockSpec((PAGE_ROWS, PAGE_ROWS), const2),
                  pl.BlockSpec((CMP_STRIDE * KVROW_W, HALF_W), const2),
                  pl.BlockSpec((1, KVROW_W), const2),
                  pl.BlockSpec((KVROW_W, KVROW_W), const2),
                  pl.BlockSpec((A_HEADS, n_half), const2),
                  pl.BlockSpec((n_half, SEL_LANES), const2)],
        out_specs=[pl.BlockSpec((1, A_HEADS, A_HD), lambda b, pt: (b, 0, 0)),
                   pl.BlockSpec((1, A_HEADS, LANE), lambda b, pt: (b, 0, 0))],
        scratch_shapes=[pltpu.VMEM((2, n_pages) + cache_pages.shape[1:], F32),
                        pltpu.SemaphoreType.DMA((2,))],
    )
    return pl.pallas_call(
        functools.partial(_cmp_sample_kernel, n_pages=n_pages, p_len=p_len),
        grid_spec=grid_spec,
        out_shape=[jax.ShapeDtypeStruct((nb, A_HEADS, A_HD), F32),
                   jax.ShapeDtypeStruct((nb, A_HEADS, LANE), jnp.int32)],
        compiler_params=_cparams("arbitrary"),
        name="cmp_sample",
    )(page_table.reshape(-1), cache_pages, q3, jnp.asarray(perm, BF16), wcat, const_row, w2bd, bias_cs, cov)


PAGE_ROWS = 128
BLOCKS_PER_PAGE = PAGE_ROWS // SLC_BLOCK


def _slc_sample_kernel(idx_ref, pt_ref, cache_ref, q_ref, snew_ref, win_ref, wnew_ref, wcol_ref, oc_ref, g_ref,
                       rbt_ref, bw_ref, ha_ref, wbuf_ref, buf, sem, *, n_pages, p_len):
    n_blk = A_KV * N_SEL
    past_blocks = p_len // SLC_BLOCK

    def block_copy(seq, i, slot):
        kv = i // N_SEL
        blk = jnp.minimum(idx_ref[seq * n_blk + i], past_blocks - 1)
        page = pt_ref[seq * n_pages + blk // BLOCKS_PER_PAGE]
        return pltpu.make_async_copy(cache_ref.at[page, kv], buf.at[slot, i], sem.at[slot])

    slot = _gather_pages(block_copy, n_blk)
    blocks = [buf.at[slot, i] for i in range(n_blk)]
    b = pl.program_id(0)
    n_keys = N_SEL * PAGE_ROWS
    n_buf = win_ref.shape[-1]
    qf = q_ref[0]
    q8 = qf.astype(BF16)
    first_o = lax.broadcasted_iota(jnp.int32, (A_HEADS, A_HD), 0) < A_GROUP
    lane = lax.broadcasted_iota(jnp.int32, (1, n_keys), 1)
    slot = lane // PAGE_ROWS
    in_page = lane % PAGE_ROWS
    bucket_row = lax.broadcasted_iota(jnp.int32, (N_BUCKETS, n_keys), 0).astype(F32)
    bias_new = rbt_ref[:, 0:1]

    def new_key_logit(row_ref, kv):
        k_new = row_ref[0, :, kv * LANE:kv * LANE + A_HD]
        v_new = row_ref[0, :, kv * LANE + A_HD:(kv + 1) * LANE]
        return jnp.sum(qf * k_new, axis=1, keepdims=True) * ATT_SCALE + bias_new, v_new

    o_s, o_w = [], []
    for kv in range(A_KV):
        blk_of = jnp.zeros((1, n_keys), jnp.int32)
        has_new = False
        for j in range(N_SEL):
            blk = idx_ref[(b * A_KV + kv) * N_SEL + j]
            blk_of = jnp.where(slot == j, blk, blk_of)
            has_new = jnp.logical_or(has_new, blk == past_blocks)
        pos = jnp.minimum(blk_of, past_blocks - 1) // BLOCKS_PER_PAGE * PAGE_ROWS + in_page
        valid = (pos // SLC_BLOCK == blk_of) & (pos < p_len)
        onehot = (bucket_row == _bucket_dyn(p_len - pos)).astype(F32)
        bias = _dot(rbt_ref[...], onehot, HIGHEST)
        kt = jnp.concatenate([blocks[kv * N_SEL + j][0] for j in range(N_SEL)], axis=1).astype(BF16)
        vt = jnp.concatenate([blocks[kv * N_SEL + j][1] for j in range(N_SEL)], axis=1).astype(BF16)
        s = jnp.where(valid, _dot(q8, kt) * ATT_SCALE + bias, NEG)
        s_new, v_new = new_key_logit(snew_ref, kv)
        s_new = jnp.where(has_new, s_new, NEG)
        e, e_new, inv = _softmax_keys_on_lanes(s, s_new)
        o_s.append((_nt(e.astype(BF16), vt) + e_new * v_new) * inv)
        sw = _dot(q8, win_ref[0, kv, 0].astype(BF16)) * ATT_SCALE + bw_ref[...]
        sw_new, vw_new = new_key_logit(wnew_ref, kv)
        e, e_new, inv = _softmax_keys_on_lanes(sw, sw_new)
        o_w.append((_nt(e.astype(BF16), win_ref[0, kv, 1].astype(BF16)) + e_new * vw_new) * inv)
    g = jax.nn.sigmoid(g_ref[0])
    ha_ref[0] = (g[0] * oc_ref[0] + g[1] * jnp.where(first_o, o_s[0], o_s[1])
                 + g[2] * jnp.where(first_o, o_w[0], o_w[1]))
    last = lax.broadcasted_iota(jnp.int32, (A_HD, n_buf), 1) == n_buf - 1
    for kv in range(A_KV):
        for c in range(2):
            r0 = (kv * 2 + c) * A_HD
            wbuf_ref[0, kv, c] = jnp.where(last, wcol_ref[0, r0:r0 + A_HD, :],
                                           pltpu.roll(win_ref[0, kv, c], n_buf - 1, 1))


def _slc_sample(idx, page_table, cache_t, q3, slc_new, win_t, win_new, win_new_col, o_c, gates, rb_t, bias_ws, p_len):
    nb, n_pages = page_table.shape
    n_buf = win_t.shape[-1]
    per_seq3 = lambda b, i, p: (b, 0, 0)
    win_spec = pl.BlockSpec((1, A_KV, 2, A_HD, n_buf), lambda b, i, p: (b, 0, 0, 0, 0))
    grid_spec = pltpu.PrefetchScalarGridSpec(
        num_scalar_prefetch=2,
        grid=(nb,),
        in_specs=[pl.BlockSpec(memory_space=pl.ANY),
                  pl.BlockSpec((1, A_HEADS, A_HD), per_seq3),
                  pl.BlockSpec((1, 1, ROW_W), per_seq3),
                  win_spec,
                  pl.BlockSpec((1, 1, ROW_W), per_seq3),
                  pl.BlockSpec((1, ROW_W, 1), per_seq3),
                  pl.BlockSpec((1, A_HEADS, A_HD), per_seq3),
                  pl.BlockSpec((1, 3, A_HEADS, 1), lambda b, i, p: (b, 0, 0, 0)),
                  pl.BlockSpec((A_HEADS, N_BUCKETS), lambda b, i, p: (0, 0)),
                  pl.BlockSpec((A_HEADS, n_buf), lambda b, i, p: (0, 0))],
        out_specs=[pl.BlockSpec((1, A_HEADS, A_HD), per_seq3), win_spec],
        scratch_shapes=[pltpu.VMEM((2, A_KV * N_SEL) + cache_t.shape[2:], F32),
                        pltpu.SemaphoreType.DMA((2,))],
    )
    return pl.pallas_call(
        functools.partial(_slc_sample_kernel, n_pages=n_pages, p_len=p_len),
        grid_spec=grid_spec,
        out_shape=[jax.ShapeDtypeStruct((nb, A_HEADS, A_HD), F32),
                   jax.ShapeDtypeStruct(win_t.shape, F32)],
        compiler_params=_cparams("arbitrary"),
        name="slc_win_sample",
    )(idx.reshape(-1), page_table.reshape(-1), cache_t, q3, slc_new, win_t, win_new, win_new_col, o_c, gates,
      rb_t, bias_ws)


def _rows_last(a):
    n = a.ndim
    return a.transpose(*range(n - 4), n - 3, n - 2, n - 1, n - 4)


def _rows_first(a):
    n = a.ndim
    return a.transpose(*range(n - 4), n - 1, n - 4, n - 3, n - 2)


def _nsa_sample(rel_bias, pa, rs_new, rw_new, cache_cmp, cache_slc, win_cache, page_table,
                bias, offs, cmp_w, const_row, p_len):
    wcat, w2bd = cmp_w
    nb, n_pages = page_table.shape
    n_half = p_len // CMP_STRIDE
    n_slc = p_len // SLC_BLOCK + 1
    n_buf = win_cache.shape[1]
    q3 = pa[:, :A_WIDTH].reshape(nb, A_HEADS, A_HD)
    gates = pa[:, 2 * A_WIDTH:2 * A_WIDTH + 3 * A_HEADS].reshape(nb, 3, A_HEADS, 1)
    bias_cs = bias[:, offs[6]:offs[7]].reshape(A_HEADS, -1)[:, :n_half]
    bias_ws = bias[:, offs[7]:offs[8]].reshape(A_HEADS, -1)[:, :n_buf]
    cov = jnp.asarray(_cover_np(n_half, n_half - 1, SEL_LANES, n_slc))
    o_c, picks = _cmp_sample(page_table, _rows_last(cache_cmp), q3, wcat, const_row, w2bd, bias_cs, cov, p_len)
    idx = picks[:, :A_KV, :N_SEL]
    ha, wbuf = _slc_sample(idx, page_table, _rows_last(cache_slc), q3, rs_new.reshape(nb, 1, ROW_W),
                           _rows_last(win_cache), rw_new.reshape(nb, 1, ROW_W), rw_new.reshape(nb, ROW_W, 1),
                           o_c, gates, rel_bias.T, bias_ws, p_len)
    return ha.reshape(nb, A_WIDTH), idx, wbuf


def kernel(x_prompt, x_sample, cache_cmp_kv, cache_slc_kv, cache_win_kv, state_mlstm_C, state_mlstm_n, state_mlstm_m, page_table, c_prompt, c_sample, rel_bias, w_ada, b_ada, w_in, b_in, m_norm_g, cmp_pe, cmp_w1, cmp_b1, cmp_w2, w_out, b_out, ln_g, ln_b):
    B, T, _ = x_prompt.shape
    NB = x_sample.shape[0]
    n_pages = page_table.shape[1]
    p_len = n_pages * PAGE_ROWS
    depth = w_in.shape[0]
    assert depth == 1 and x_sample.shape[1] == 1 and cache_win_kv.shape[2] == WINDOW
    ids, offs = _static_ids(p_len)
    bias = _bias_tables(rel_bias, ids)
    x_p = x_prompt.reshape(B * T, D_MODEL)
    x_s = x_sample.reshape(NB, D_MODEL)
    l = 0
    n_mod = -(-(B + NB) // SUBLANE) * SUBLANE
    c_all = jnp.concatenate([c_prompt, c_sample, jnp.zeros((n_mod - B - NB, D_MODEL), F32)])
    shift, scale, gate = jnp.split(_adaln_mod(c_all, w_ada[l], b_ada[l]), 3, axis=-1)
    packed = _pack_in_proj(w_in[l], b_in[l], BF16)
    packed_f32 = _pack_in_proj(w_in[l], b_in[l], F32)
    cmp_w = _pack_compress(cmp_w1[l], cmp_w2[l])
    const_row = _compress_const(cmp_pe[l], cmp_w1[l], cmp_b1[l])
    w_out_b = w_out[l].astype(BF16)
    vecs = (b_out[l].reshape(1, -1), ln_g[l].reshape(1, -1), ln_b[l].reshape(1, -1))
    pm, pa, rc, rs, rw, ct, st, wt = _project(x_p, shift[:B, None], scale[:B, None], packed, B, 256)
    mix_m, c_p, n_p, m_p = _mlstm_prompt(pm, m_norm_g[l], B, T)
    o_c, o_s, o_w, _ = _nsa_prompt(rel_bias, pa, rc, rs, rw, st, wt, bias, offs, cmp_w, const_row, B, T)
    y_p = _out_prompt(x_p, mix_m, o_c, o_s, o_w, pa, gate[:B, None], w_out_b, *vecs, B, T)
    pm_s, pa_s, _, rs_s, rw_s, ct_s, st_s, wt_s = _project(x_s, shift[B:B + NB], scale[B:B + NB], packed_f32, 1, NB)
    mix_s, c_s, n_s, m_s = _mlstm_sample(pm_s, m_norm_g[l], state_mlstm_C[l], state_mlstm_n[l], state_mlstm_m[l])
    ha_s, _, wbuf_s = _nsa_sample(rel_bias, pa_s, rs_s, rw_s, cache_cmp_kv[l], cache_slc_kv[l],
                                  cache_win_kv[l], page_table, bias, offs, cmp_w, const_row, p_len)
    y_s = _out_sample(x_s, mix_s.reshape(NB, M_WIDTH), ha_s, pa_s, gate[B:B + NB], w_out_b, *vecs)

    def kv_prompt(a):
        return _rows_first(a.reshape(1, B, A_KV, 2, A_HD, a.shape[-1]))

    def kv_sample(a):
        return a.reshape(1, 1, A_KV, 2, A_HD, NB).transpose(0, 5, 1, 2, 3, 4)

    return (y_p.reshape(B, T, D_MODEL), y_s.reshape(NB, 1, D_MODEL),
            kv_prompt(ct), kv_sample(ct_s), kv_prompt(st), kv_sample(st_s),
            kv_prompt(wt[:, :, T - WINDOW:]), _rows_first(wbuf_s)[None],
            c_p[None], c_s[None], n_p[None], n_s[None], m_p[None, :, :, 0], m_s[None])
```

```python
import functools
import math

import numpy as np
import jax
import jax.numpy as jnp
from jax import lax
from jax.experimental import pallas as pl
from jax.experimental.pallas import tpu as pltpu

F32 = jnp.float32
BF16 = jnp.bfloat16
HIGHEST = lax.Precision.HIGHEST

D_MODEL = 1024
M_HEADS = 4
M_HD = 128
M_WIDTH = M_HEADS * M_HD
M_CHUNK = 128
A_HEADS = 8
A_HD = 64
A_KV = 2
A_GROUP = A_HEADS // A_KV
A_WIDTH = A_HEADS * A_HD
A_KVW = A_KV * A_HD
ROW_W = 2 * A_KVW
CMP_LEN = 32
CMP_STRIDE = 16
SLC_BLOCK = 64
N_SEL = 16
WINDOW = 512
N_BUCKETS = 32
MAX_EXACT = N_BUCKETS // 2
MAX_DIST = 128
FAR_BUCKET = N_BUCKETS - 1
LN_EPS = 1e-5
ATT_SCALE = A_HD ** -0.5
DEPTH = 1
DEEPNORM_ALPHA = (2.0 * DEPTH) ** 0.25
IN_SPLITS = (M_WIDTH,) * 5 + (M_HEADS, M_HEADS) + (A_WIDTH,) + (A_KVW,) * 6 + (3 * A_HEADS, A_WIDTH)

LANE = 128
SUBLANE = 8
TQ = 128
NEG = -1e30
LOG2E = math.log2(math.e)
MASKED_ID = N_BUCKETS
VMEM_LIMIT = 56 * 1024 * 1024

PM_W = 5 * M_WIDTH + LANE
PA_W = 2 * A_WIDTH + LANE
PW_TOTAL = PM_W + PA_W + 3 * ROW_W


def _cparams(*sem):
    return pltpu.CompilerParams(dimension_semantics=sem, vmem_limit_bytes=VMEM_LIMIT)


def _nt(a, b):
    return lax.dot_general(a, b, (((1,), (1,)), ((), ())), preferred_element_type=F32)


def _dot(a, b, precision=None):
    return jnp.dot(a, b, preferred_element_type=F32, precision=precision)


def _log_sigmoid(x):
    return jnp.minimum(x, 0.0) - jnp.log(1.0 + jnp.exp(-jnp.abs(x)))


def _silu(x):
    return x * jax.nn.sigmoid(x)


def _gelu_tanh(x):
    return 0.5 * x * (1.0 + jnp.tanh(math.sqrt(2.0 / math.pi) * (x + 0.044715 * (x * x * x))))


def _ln_rows(x):
    mu = jnp.mean(x, axis=-1, keepdims=True)
    xc = x - mu
    var = jnp.mean(xc * xc, axis=-1, keepdims=True)
    return xc * lax.rsqrt(var + LN_EPS)


def _bucket_np(dist):
    dist = np.asarray(dist, np.int64)
    n = np.maximum(dist, 0)
    nf = np.maximum(n, 1).astype(np.float32)
    large = MAX_EXACT + (np.log(nf / np.float32(MAX_EXACT)) / np.float32(math.log(MAX_DIST / MAX_EXACT))
                         * np.float32(N_BUCKETS - MAX_EXACT)).astype(np.int32)
    large = np.minimum(large, N_BUCKETS - 1)
    b = np.where(n < MAX_EXACT, n, large)
    return np.where(dist < 0, MASKED_ID, b).astype(np.int32)


def _mod_kernel(c_ref, w_ref, b_ref, o_ref):
    a = _silu(c_ref[...])
    o_ref[...] = _dot(a, w_ref[...]) + b_ref[...]


def _adaln_mod(c, w_ada, b_ada):
    rows = c.shape[0]
    n3 = w_ada.shape[1]
    tn = D_MODEL
    return pl.pallas_call(
        _mod_kernel,
        grid=(n3 // tn,),
        in_specs=[pl.BlockSpec((rows, D_MODEL), lambda j: (0, 0)),
                  pl.BlockSpec((D_MODEL, tn), lambda j: (0, j)),
                  pl.BlockSpec((1, tn), lambda j: (0, j))],
        out_specs=pl.BlockSpec((rows, tn), lambda j: (0, j)),
        out_shape=jax.ShapeDtypeStruct((rows, n3), F32),
        compiler_params=_cparams("arbitrary"),
        name="adaln_mod",
    )(c, w_ada, b_ada.reshape(1, n3))


def _bias_kernel(rb_ref, ids_ref, o_ref, *, n_groups):
    def body(i, carry):
        r0 = pl.multiple_of(i * SUBLANE, SUBLANE)
        ids = ids_ref[pl.ds(r0, SUBLANE), :]
        for h in range(A_HEADS):
            acc = jnp.full((SUBLANE, LANE), NEG, F32)
            for b in range(N_BUCKETS):
                acc = jnp.where(ids == b, rb_ref[b, h], acc)
            o_ref[h, pl.ds(r0, SUBLANE), :] = acc
        return carry

    lax.fori_loop(0, n_groups, body, 0)


def _bias_tables(rel_bias, ids):
    rows = ids.shape[0]
    return pl.pallas_call(
        functools.partial(_bias_kernel, n_groups=rows // SUBLANE),
        in_specs=[pl.BlockSpec(memory_space=pltpu.SMEM),
                  pl.BlockSpec((rows, LANE), lambda: (0, 0))],
        out_specs=pl.BlockSpec((A_HEADS, rows, LANE), lambda: (0, 0, 0)),
        out_shape=jax.ShapeDtypeStruct((A_HEADS, rows, LANE), F32),
        name="bias_tables",
    )(rel_bias, jnp.asarray(ids))


def _pack_in_proj(w_in, b_in, dtype):
    offs = np.cumsum((0,) + IN_SPLITS)
    names = ("mq", "mk", "mv", "mo", "mz", "mi", "mf", "aq", "ck", "cv", "sk", "sv", "wk", "wv", "ga", "za")
    sl = {n: (int(offs[i]), int(offs[i + 1])) for i, n in enumerate(names)}

    def cols(a, name, lo=None, hi=None):
        s, e = sl[name]
        if lo is not None:
            s, e = s + lo, s + hi
        return a[..., s:e]

    def rows_of(a, kn, vn):
        return [cols(a, kn, 0, A_HD), cols(a, vn, 0, A_HD), cols(a, kn, A_HD, 2 * A_HD), cols(a, vn, A_HD, 2 * A_HD)]

    def pack(a):
        def zeros(n):
            return jnp.zeros(a.shape[:-1] + (n,), a.dtype)
        parts = [cols(a, n) for n in ("mq", "mk", "mv", "mo", "mz")]
        parts += [cols(a, "mi"), cols(a, "mf"), zeros(LANE - 2 * M_HEADS)]
        parts += [cols(a, "aq"), cols(a, "za"), cols(a, "ga"), zeros(LANE - 3 * A_HEADS)]
        parts += rows_of(a, "ck", "cv") + rows_of(a, "sk", "sv") + rows_of(a, "wk", "wv")
        return jnp.concatenate(parts, axis=-1)

    w = pack(w_in)
    b = pack(b_in.reshape(1, -1))
    wt = w[:, PM_W + PA_W:].T
    bt = b[:, PM_W + PA_W:].reshape(-1, 1)
    return w.astype(dtype), b, wt.astype(dtype), bt


def _proj_kernel(x_ref, sh_ref, sc_ref, w_ref, b_ref, wt_ref, bt_ref,
                 om_ref, oa_ref, oc_ref, os_ref, ow_ref, oct_ref, ost_ref, owt_ref):
    h = _ln_rows(x_ref[...]) * (1.0 + sc_ref[...]) + sh_ref[...]
    hb = h.astype(w_ref.dtype)
    precision = HIGHEST if w_ref.dtype == F32 else None
    lo = 0
    for o_ref in (om_ref, oa_ref, oc_ref, os_ref, ow_ref):
        n = o_ref.shape[-1]
        o_ref[...] = _dot(hb, w_ref[:, lo:lo + n], precision) + b_ref[:, lo:lo + n]
        lo += n
    t = lax.dot_general(wt_ref[...], hb, (((1,), (1,)), ((), ())), preferred_element_type=F32,
                        precision=precision) + bt_ref[...]
    for i, o_ref in enumerate((oct_ref, ost_ref, owt_ref)):
        o_ref[0] = t[i * ROW_W:(i + 1) * ROW_W]


def _project(x, shift, scale, packed, groups, tm):
    w, b, wt, bt = packed
    rows = x.shape[0]
    per = rows // groups // tm
    if shift.ndim == 3:
        mod_spec = pl.BlockSpec((None, 1, D_MODEL), lambda i: (i // per, 0, 0))
    else:
        mod_spec = pl.BlockSpec((tm, D_MODEL), lambda i: (i, 0))
    widths = (PM_W, PA_W, ROW_W, ROW_W, ROW_W)
    return pl.pallas_call(
        _proj_kernel,
        grid=(rows // tm,),
        in_specs=[pl.BlockSpec((tm, D_MODEL), lambda i: (i, 0)), mod_spec, mod_spec,
                  pl.BlockSpec((D_MODEL, PW_TOTAL), lambda i: (0, 0)),
                  pl.BlockSpec((1, PW_TOTAL), lambda i: (0, 0)),
                  pl.BlockSpec((3 * ROW_W, D_MODEL), lambda i: (0, 0)),
                  pl.BlockSpec((3 * ROW_W, 1), lambda i: (0, 0))],
        out_specs=[pl.BlockSpec((tm, n), lambda i: (i, 0)) for n in widths]
                  + [pl.BlockSpec((1, ROW_W, tm), lambda i: (i // per, 0, i % per))] * 3,
        out_shape=[jax.ShapeDtypeStruct((rows, n), F32) for n in widths]
                  + [jax.ShapeDtypeStruct((groups, ROW_W, rows // groups), F32)] * 3,
        compiler_params=_cparams("arbitrary"),
        name="in_proj",
    )(x, shift, scale, w, b, wt, bt)


def _mlstm_head_out(h, o_pre, z_pre, g_row):
    return jax.nn.sigmoid(o_pre) * (_ln_rows(h) * g_row) * _silu(z_pre)


def _mlstm_prompt_kernel(q_ref, k_ref, v_ref, o_ref, z_ref, g_ref, ng_ref, mix_ref, c_ref, n_ref, m_ref):
    L = M_CHUNK

    @pl.when(pl.program_id(0) == 0)
    def _():
        c_ref[...] = jnp.zeros_like(c_ref)
        n_ref[...] = jnp.zeros_like(n_ref)
        m_ref[...] = jnp.zeros_like(m_ref)

    row = lax.broadcasted_iota(jnp.int32, (L, L), 0)
    col = lax.broadcasted_iota(jnp.int32, (L, L), 1)
    tril = col <= row
    lower = tril.astype(F32)
    upper = (row <= col).astype(F32)
    nb = q_ref.shape[0]
    units = [(b, h) for b in range(nb) for h in range(M_HEADS)]
    gates = [g_ref[b] for b in range(nb)]
    gates_t = [g.T for g in gates]
    cum = [_dot(lower, _log_sigmoid(g), HIGHEST) for g in gates]
    cum_t = [_dot(_log_sigmoid(g), upper, HIGHEST) for g in gates_t]
    st = {}
    for b, h in units:
        hs = slice(h * M_HD, (h + 1) * M_HD)
        b_col = cum[b][:, M_HEADS + h:M_HEADS + h + 1]
        b_row = cum_t[b][M_HEADS + h:M_HEADS + h + 1, :]
        m_prev = m_ref[b, h:h + 1, 0:1]
        d = jnp.where(tril, b_col - b_row + gates_t[b][h:h + 1, :], NEG)
        inter = b_col + m_prev
        m_t = jnp.maximum(inter, jnp.max(d, axis=1, keepdims=True))
        q = q_ref[b, :, hs]
        ks = k_ref[b, :, hs] * (M_HD ** -0.5)
        st[b, h] = dict(hs=hs, b_col=b_col, m_prev=m_prev, d=d, m_t=m_t, w_inter=jnp.exp(inter - m_t), q=q, ks=ks,
                        qb=q.astype(BF16), kb=ks.astype(BF16), vb=v_ref[b, :, hs].astype(BF16),
                        c_prev=c_ref[b, h], n_prev=n_ref[b, h:h + 1, :])
    for u in units:
        s = st[u]
        s["qk"] = _nt(s["qb"], s["kb"]) * jnp.exp(s["d"] - s["m_t"])
        s["qc"] = _dot(s["qb"], s["c_prev"].astype(BF16))
    for u in units:
        s = st[u]
        num = s["w_inter"] * s["qc"] + _dot(s["qk"].astype(BF16), s["vb"])
        den = (s["w_inter"] * jnp.sum(s["q"] * s["n_prev"], axis=1, keepdims=True)
               + jnp.sum(s["qk"], axis=1, keepdims=True))
        s["hh"] = num / jnp.maximum(jnp.abs(den), jnp.exp(-s["m_t"]))
    for (b, h) in units:
        s = st[b, h]
        m_new = s["m_t"][L - 1:L, :]
        b_last = s["b_col"][L - 1:L, :]
        w_c = jnp.exp(b_last + s["m_prev"] - m_new)
        w_s = jnp.exp(b_last - s["b_col"] + gates[b][:, h:h + 1] - m_new)
        kw = s["ks"] * w_s
        c_ref[b, h] = w_c * s["c_prev"] + _dot(kw.T.astype(BF16), s["vb"])
        n_ref[b, h:h + 1, :] = w_c * s["n_prev"] + jnp.sum(kw, axis=0, keepdims=True)
        m_ref[b, h:h + 1, :] = jnp.broadcast_to(m_new, (1, M_HD))
    for (b, h) in units:
        s = st[b, h]
        hs = s["hs"]
        mix_ref[b, :, hs] = _mlstm_head_out(s["hh"], o_ref[b, :, hs], z_ref[b, :, hs], ng_ref[:, hs])


def _mlstm_prompt(pm, norm_g, batch, seq):
    nc = seq // M_CHUNK
    pm3 = pm.reshape(batch, seq, PM_W)

    def col_spec(j, width=M_WIDTH):
        return pl.BlockSpec((batch, M_CHUNK, width), lambda c: (0, c, j))

    state = lambda c: (0, 0, 0)
    mix, c_p, n_p, m_p = pl.pallas_call(
        _mlstm_prompt_kernel,
        grid=(nc,),
        in_specs=[col_spec(0), col_spec(1), col_spec(2), col_spec(3), col_spec(4),
                  pl.BlockSpec((batch, M_CHUNK, LANE), lambda c: (0, c, 5 * M_WIDTH // LANE)),
                  pl.BlockSpec((1, M_WIDTH), lambda c: (0, 0))],
        out_specs=[pl.BlockSpec((batch, M_CHUNK, M_WIDTH), lambda c: (0, c, 0)),
                   pl.BlockSpec((batch, M_HEADS, M_HD, M_HD), lambda c: (0, 0, 0, 0)),
                   pl.BlockSpec((batch, M_HEADS, M_HD), state),
                   pl.BlockSpec((batch, M_HEADS, M_HD), state)],
        out_shape=[jax.ShapeDtypeStruct((batch, seq, M_WIDTH), F32),
                   jax.ShapeDtypeStruct((batch, M_HEADS, M_HD, M_HD), F32),
                   jax.ShapeDtypeStruct((batch, M_HEADS, M_HD), F32),
                   jax.ShapeDtypeStruct((batch, M_HEADS, M_HD), F32)],
        compiler_params=_cparams("arbitrary"),
        name="mlstm_prompt",
    )(pm3, pm3, pm3, pm3, pm3, pm3, norm_g.reshape(1, M_WIDTH))
    return mix.reshape(batch * seq, M_WIDTH), c_p, n_p, m_p


MS_G = 128


def _mlstm_sample_kernel(q_ref, k_ref, v_ref, o_ref, z_ref, ig_ref, fg_ref, m_ref, n_ref, c_ref, ng_ref,
                         mix_ref, co_ref, no_ref, mo_ref):
    q = q_ref[...]
    ks = k_ref[...] * (M_HD ** -0.5)
    v = v_ref[...]
    n_prev = n_ref[...]
    ig = ig_ref[...]
    inter = _log_sigmoid(fg_ref[...]) + m_ref[...]
    m_t = jnp.maximum(inter, ig)
    w_inter = jnp.exp(inter - m_t)
    w_s = jnp.exp(ig - m_t)
    qk = jnp.sum(q * ks, axis=1, keepdims=True) * w_s
    q_t = q.T
    kw_t = (ks * w_s).T
    rows = []
    for r in range(MS_G):
        c_prev = c_ref[r]
        rows.append(jnp.sum(q_t[:, r:r + 1] * c_prev, axis=0, keepdims=True))
        co_ref[r] = w_inter[r:r + 1, :] * c_prev + kw_t[:, r:r + 1] * v[r:r + 1, :]
    q_c = jnp.concatenate(rows, axis=0)
    num = w_inter * q_c + qk * v
    den = w_inter * jnp.sum(q * n_prev, axis=1, keepdims=True) + qk
    hh = num / jnp.maximum(jnp.abs(den), jnp.exp(-m_t))
    no_ref[...] = w_inter * n_prev + w_s * ks
    mo_ref[...] = m_t
    mix_ref[...] = _mlstm_head_out(hh, o_ref[...], z_ref[...], ng_ref[...])


def _mlstm_sample(pm, norm_g, c0, n0, m0):
    nb = pm.shape[0]
    rows = nb * M_HEADS

    def head_rows(j):
        return pm[:, j * M_WIDTH:(j + 1) * M_WIDTH].reshape(rows, M_HD)

    gates = pm[:, 5 * M_WIDTH:5 * M_WIDTH + 2 * M_HEADS]
    ig = gates[:, :M_HEADS].reshape(rows, 1)
    fg = gates[:, M_HEADS:].reshape(rows, 1)
    ng_rows = jnp.tile(norm_g.reshape(M_HEADS, M_HD), (MS_G // M_HEADS, 1))
    vec = pl.BlockSpec((MS_G, M_HD), lambda i: (i, 0))
    one = pl.BlockSpec((MS_G, 1), lambda i: (i, 0))
    mat = pl.BlockSpec((MS_G, M_HD, M_HD), lambda i: (i, 0, 0))
    mix, c1, n1, m1 = pl.pallas_call(
        _mlstm_sample_kernel,
        grid=(rows // MS_G,),
        in_specs=[vec] * 5 + [one] * 3 + [vec, mat, pl.BlockSpec((MS_G, M_HD), lambda i: (0, 0))],
        out_specs=[vec, mat, vec, one],
        out_shape=[jax.ShapeDtypeStruct((rows, M_HD), F32),
                   jax.ShapeDtypeStruct((rows, M_HD, M_HD), F32),
                   jax.ShapeDtypeStruct((rows, M_HD), F32),
                   jax.ShapeDtypeStruct((rows, 1), F32)],
        compiler_params=_cparams("arbitrary"),
        name="mlstm_sample",
    )(*[head_rows(j) for j in range(5)], ig, fg, m0.reshape(rows, 1), n0.reshape(rows, M_HD),
      c0.reshape(rows, M_HD, M_HD), ng_rows)
    return (mix.reshape(nb, M_WIDTH), c1.reshape(nb, M_HEADS, M_HD, M_HD), n1.reshape(nb, M_HEADS, M_HD),
            m1.reshape(nb, M_HEADS))


KVROW_W = 2 * A_HD
HALF_W = 2 * KVROW_W


def _pack_compress(w1, w2):
    def block_diag(k, v):
        z = jnp.zeros_like(k)
        return jnp.concatenate([jnp.concatenate([k, z], axis=-1), jnp.concatenate([z, v], axis=-1)], axis=-2)

    wbd = block_diag(w1[0], w1[1])
    wcat = jnp.concatenate([wbd[:CMP_STRIDE], wbd[CMP_STRIDE:]], axis=-1)
    return wcat.reshape(CMP_STRIDE * KVROW_W, HALF_W).astype(BF16), block_diag(w2[0], w2[1]).astype(BF16)


def _cmp_const_kernel(pe_ref, w_ref, b_ref, o_ref):
    for c in range(2):
        o_ref[c] = _dot(pe_ref[c], w_ref[c], HIGHEST) + b_ref[c]


def _compress_const(pe, w1, b1):
    k = CMP_LEN * A_HD
    pe8 = jnp.broadcast_to(pe.reshape(2, 1, k), (2, SUBLANE, k))
    out = pl.pallas_call(
        _cmp_const_kernel,
        out_shape=jax.ShapeDtypeStruct((2, SUBLANE, A_HD), F32),
        name="compress_const",
    )(pe8, w1.reshape(2, k, A_HD), b1.reshape(2, 1, A_HD))
    return jnp.concatenate([out[0, 0:1], out[1, 0:1]], axis=-1)


def _compress_halves(halves, wcat_ref, const_ref, w2_ref, n_half):
    acc = _dot(halves.astype(BF16), wcat_ref[...])
    pre = acc[:, :KVROW_W] + pltpu.roll(acc[:, KVROW_W:], A_KV * n_half - 1, 0) + const_ref[...]
    return _dot(_gelu_tanh(pre).astype(BF16), w2_ref[...])


def _compress_prompt_kernel(x0_ref, x1_ref, wcat_ref, const_ref, w2_ref, kk_ref, kvt_ref, *, n_half):
    x_refs = (x0_ref, x1_ref)
    halves = jnp.concatenate(
        [jnp.concatenate([x_refs[kv][pl.ds(p, n_half, stride=CMP_STRIDE), :] for kv in range(A_KV)], axis=0)
         for p in range(CMP_STRIDE)], axis=1)
    kc = _compress_halves(halves, wcat_ref, const_ref, w2_ref, n_half)
    kct = kc.T
    for kv in range(A_KV):
        kk_ref[0, kv] = kc[kv * n_half:(kv + 1) * n_half, 0:A_HD]
        kvt_ref[0, kv] = kct[A_HD:, kv * n_half:(kv + 1) * n_half]


def _compress_prompt(rows, wcat, const_row, w2bd, batch, seq):
    n_half = seq // CMP_STRIDE
    return pl.pallas_call(
        functools.partial(_compress_prompt_kernel, n_half=n_half),
        grid=(batch,),
        in_specs=[pl.BlockSpec((seq, KVROW_W), lambda b: (b, 0)),
                  pl.BlockSpec((seq, KVROW_W), lambda b: (b, 1)),
                  pl.BlockSpec((CMP_STRIDE * KVROW_W, HALF_W), lambda b: (0, 0)),
                  pl.BlockSpec((1, KVROW_W), lambda b: (0, 0)),
                  pl.BlockSpec((KVROW_W, KVROW_W), lambda b: (0, 0))],
        out_specs=[pl.BlockSpec((1, A_KV, n_half, A_HD), lambda b: (b, 0, 0, 0)),
                   pl.BlockSpec((1, A_KV, A_HD, n_half), lambda b: (b, 0, 0, 0))],
        out_shape=[jax.ShapeDtypeStruct((batch, A_KV, n_half, A_HD), F32),
                   jax.ShapeDtypeStruct((batch, A_KV, A_HD, n_half), F32)],
        compiler_params=_cparams("arbitrary"),
        name="compress_prompt",
    )(rows, rows, wcat, const_row, w2bd)


CMP_PAT = 16


def _static_ids(p_len):
    i = np.arange(TQ)[None, :]
    c = np.arange(CMP_PAT)[:, None]
    cmp_a = _bucket_np(i + (TQ - (CMP_LEN - 1)) - CMP_STRIDE * c)
    cmp_b = _bucket_np(i - CMP_STRIDE * c - (CMP_LEN - 1))
    r = np.arange(TQ)[:, None]
    slc_diag = _bucket_np(i - r)
    slc_sub = _bucket_np(TQ + i - r)
    slc_far = np.full((TQ, TQ), FAR_BUCKET, np.int32)
    rw = np.arange(2 * WINDOW + TQ)[:, None]
    dw = WINDOW + i - rw
    win = np.where(dw > WINDOW, MASKED_ID, _bucket_np(dw))
    n_half = p_len // CMP_STRIDE
    n = np.arange(n_half)
    cs = _bucket_np(p_len - (CMP_STRIDE * n + CMP_LEN - 1))
    cs[n_half - 1] = MASKED_ID
    cs_rows = -(-n_half // LANE)
    cs_pad = np.full((cs_rows * LANE,), MASKED_ID, np.int32)
    cs_pad[:n_half] = cs
    ws = _bucket_np(WINDOW - np.arange(WINDOW))
    last_page = _bucket_np(PAGE_ROWS - np.arange(PAGE_ROWS))
    parts = [cmp_a, cmp_b, slc_diag, slc_sub, slc_far, win, cs_pad.reshape(cs_rows, LANE),
             ws.reshape(WINDOW // LANE, LANE), last_page.reshape(1, LANE)]
    offs = np.cumsum([0] + [p.shape[0] for p in parts])
    total = -(-int(offs[-1]) // SUBLANE) * SUBLANE
    ids = np.full((total, LANE), MASKED_ID, np.int32)
    ids[:offs[-1]] = np.concatenate(parts, axis=0)
    return ids, [int(o) for o in offs]


def _cover_np(n_cmp_rows, n_cmp, n_slc_rows, n_slc):
    cs = np.arange(n_cmp_rows)[:, None] * CMP_STRIDE
    ss = np.arange(n_slc_rows)[None, :] * SLC_BLOCK
    cov = (cs <= ss + SLC_BLOCK - 1) & (cs + CMP_LEN - 1 >= ss)
    cov &= (np.arange(n_cmp_rows)[:, None] < n_cmp) & (np.arange(n_slc_rows)[None, :] < n_slc)
    return cov.astype(np.float32)


def _softmax_keys_on_rows(s):
    m = jnp.max(s, axis=0, keepdims=True)
    m = jnp.where(m > 0.5 * NEG, m, 0.0)
    e = jnp.exp(s - m)
    tot = jnp.sum(e, axis=0, keepdims=True)
    return e / jnp.where(tot > 0.0, tot, 1.0)


def _softmax_keys_on_lanes(s, s_new=None):
    m = jnp.max(s, axis=1, keepdims=True)
    if s_new is not None:
        m = jnp.maximum(m, s_new)
    m = jnp.where(m > 0.5 * NEG, m, 0.0)
    e = jnp.exp(s - m)
    tot = jnp.sum(e, axis=1, keepdims=True)
    e_new = None
    if s_new is not None:
        e_new = jnp.exp(s_new - m)
        tot = tot + e_new
    return e, e_new, 1.0 / jnp.where(tot > 0.0, tot, 1.0)


def _cmp_attend_kernel(far_ref, q_ref, kk_ref, kvt_ref, pt_ref, cov_ref, o_ref, sel_ref, bscr, sc_scr, *, nc, ns):
    k = pl.program_id(1)
    start = pl.multiple_of(jnp.maximum(SUBLANE * k - SUBLANE, 0), SUBLANE)
    variant = jnp.where(k == 0, 1, 0)
    row = lax.broadcasted_iota(jnp.int32, (nc, GROUP_LANES), 0)
    t = k * TQ + lax.broadcasted_iota(jnp.int32, (ns, TQ), 1)
    blk = lax.broadcasted_iota(jnp.int32, (ns, TQ), 0)
    cur = t // SLC_BLOCK
    valid = blk * SLC_BLOCK <= t
    forced = (blk == 0) | (blk == cur) | (blk == cur - 1)
    kvs = range(A_KV)
    for kv in kvs:
        bscr[kv] = jnp.where(row < start, far_ref[kv], NEG)
        bscr[kv, pl.ds(start, CMP_PAT), :] = pt_ref[variant, kv]
    s = [_nt(kk_ref[0, kv].astype(BF16), _stacked_queries(q_ref, kv)) + bscr[kv] for kv in kvs]
    p = [_softmax_keys_on_rows(s[kv]) for kv in kvs]
    o = [_dot(kvt_ref[0, kv].astype(BF16), p[kv].astype(BF16)) for kv in kvs]
    for kv in kvs:
        for g in range(A_GROUP):
            h = kv * A_GROUP + g
            o_ref[0, h * A_HD:(h + 1) * A_HD, :] = o[kv][:, g * TQ:(g + 1) * TQ]
    imp = [sum(p[kv][:, g * TQ:(g + 1) * TQ] for g in range(A_GROUP)) for kv in kvs]
    sc = [jnp.where(forced, jnp.inf, jnp.where(valid, _dot(cov_ref[...], imp[kv], HIGHEST), -jnp.inf)) for kv in kvs]
    for kv in kvs:
        sc_scr[kv] = sc[kv]

    def count(j, cnt):
        out = []
        for kv in kvs:
            r = sc_scr[kv, pl.ds(j, 1), :]
            before = (r > sc[kv]) | ((r == sc[kv]) & (blk > j))
            out.append(cnt[kv] + before.astype(jnp.int32))
        return tuple(out)

    n_visible = jnp.minimum((k + 1) * (TQ // SLC_BLOCK), ns)
    cnt = lax.fori_loop(0, n_visible, count, tuple(jnp.zeros((ns, TQ), jnp.int32) for _ in kvs))
    for kv in kvs:
        sel_ref[0, kv] = jnp.where(cnt[kv] < N_SEL, 0.0, NEG)


def _cmp_attend(far, pa, kk, kvt, pat, cov_t, batch, seq):
    nq = seq // TQ
    nc = seq // CMP_STRIDE
    ns = seq // SLC_BLOCK
    return pl.pallas_call(
        functools.partial(_cmp_attend_kernel, nc=nc, ns=ns),
        grid=(batch, nq),
        in_specs=[pl.BlockSpec((A_KV, 1, GROUP_LANES), lambda b, k: (0, 0, 0)),
                  pl.BlockSpec((TQ, A_WIDTH), lambda b, k: (b * nq + k, 0)),
                  pl.BlockSpec((1, A_KV, nc, A_HD), lambda b, k: (b, 0, 0, 0)),
                  pl.BlockSpec((1, A_KV, A_HD, nc), lambda b, k: (b, 0, 0, 0)),
                  pl.BlockSpec((2, A_KV, CMP_PAT, GROUP_LANES), lambda b, k: (0, 0, 0, 0)),
                  pl.BlockSpec((ns, nc), lambda b, k: (0, 0))],
        out_specs=[pl.BlockSpec((1, A_WIDTH, TQ), lambda b, k: (b, 0, k)),
                   pl.BlockSpec((1, A_KV, ns, TQ), lambda b, k: (b, 0, 0, k))],
        out_shape=[jax.ShapeDtypeStruct((batch, A_WIDTH, seq), F32),
                   jax.ShapeDtypeStruct((batch, A_KV, ns, seq), F32)],
        scratch_shapes=[pltpu.VMEM((A_KV, nc, GROUP_LANES), F32), pltpu.VMEM((A_KV, ns, TQ), F32)],
        compiler_params=_cparams("arbitrary", "arbitrary"),
        name="cmp_attend",
    )(far, pa, kk, kvt, pat, cov_t)


SLC_CK = 4 * TQ
GROUP_LANES = A_GROUP * TQ
SLC_CLASSES = 4


def _stacked_queries(q_ref, kv, scale=ATT_SCALE):
    heads = [q_ref[:, (kv * A_GROUP + g) * A_HD:(kv * A_GROUP + g + 1) * A_HD] for g in range(A_GROUP)]
    return (jnp.concatenate(heads, axis=0) * scale).astype(BF16)


def _slc_attend_kernel(q_ref, k_ref, vt_ref, sel_ref, tab_ref, o_ref):
    k = pl.program_id(1)
    sub = SLC_CK // TQ
    q4 = [_stacked_queries(q_ref, kv, ATT_SCALE * LOG2E) for kv in range(A_KV)]
    upper = lax.broadcasted_iota(jnp.int32, (TQ, GROUP_LANES), 0) < SLC_BLOCK

    def body(j, carry):
        j0 = pl.multiple_of(j * SLC_CK, SLC_CK)
        kvs = range(A_KV)
        s_all = [_nt(k_ref[pl.ds(j0, SLC_CK), kv * LANE:kv * LANE + A_HD].astype(BF16), q4[kv]) for kv in kvs]
        vt = [vt_ref[0, kv * LANE + A_HD:(kv + 1) * LANE, pl.ds(j0, SLC_CK)].astype(BF16) for kv in kvs]
        s = []
        for kv in kvs:
            parts = []
            for u in range(sub):
                jj = j * sub + u
                sel0 = jnp.concatenate([sel_ref[0, kv, pl.ds(2 * jj, 1), :]] * A_GROUP, axis=1)
                sel1 = jnp.concatenate([sel_ref[0, kv, pl.ds(2 * jj + 1, 1), :]] * A_GROUP, axis=1)
                cls = jnp.where(jj > k, SLC_CLASSES - 1, jnp.minimum(k - jj, 2))
                parts.append(s_all[kv][u * TQ:(u + 1) * TQ] + tab_ref[cls, kv] + jnp.where(upper, sel0, sel1))
            s.append(jnp.concatenate(parts, axis=0))
        m_new = [jnp.maximum(carry[kv][0], jnp.max(s[kv], axis=0, keepdims=True)) for kv in kvs]
        p = [jnp.exp2(s[kv] - m_new[kv]) for kv in kvs]
        pv = [_dot(vt[kv], p[kv].astype(BF16)) for kv in kvs]
        out = []
        for kv in kvs:
            m_run, l_run, acc = carry[kv]
            alpha = jnp.exp2(m_run - m_new[kv])
            l_new = alpha * l_run + jnp.sum(p[kv], axis=0, keepdims=True)
            out.append((m_new[kv], l_new, alpha * acc + pv[kv]))
        return tuple(out)

    init = tuple((jnp.full((1, GROUP_LANES), NEG, F32), jnp.zeros((1, GROUP_LANES), F32),
                  jnp.zeros((A_HD, GROUP_LANES), F32)) for _ in range(A_KV))
    res = lax.fori_loop(0, (k + sub) // sub, body, init)
    for kv in range(A_KV):
        _, l_run, acc = res[kv]
        o = acc / l_run
        for g in range(A_GROUP):
            h = kv * A_GROUP + g
            o_ref[0, h * A_HD:(h + 1) * A_HD, :] = o[:, g * TQ:(g + 1) * TQ]


def _slc_attend(pa, rows, rows_t, sel, tab, batch, seq):
    nq = seq // TQ
    ns = seq // SLC_BLOCK
    return pl.pallas_call(
        _slc_attend_kernel,
        grid=(batch, nq),
        in_specs=[pl.BlockSpec((TQ, A_WIDTH), lambda b, k: (b * nq + k, 0)),
                  pl.BlockSpec((seq, ROW_W), lambda b, k: (b, 0)),
                  pl.BlockSpec((1, ROW_W, seq), lambda b, k: (b, 0, 0)),
                  pl.BlockSpec((1, A_KV, ns, TQ), lambda b, k: (b, 0, 0, k)),
                  pl.BlockSpec((SLC_CLASSES, A_KV, TQ, GROUP_LANES), lambda b, k: (0, 0, 0, 0))],
        out_specs=pl.BlockSpec((1, A_WIDTH, TQ), lambda b, k: (b, 0, k)),
        out_shape=jax.ShapeDtypeStruct((batch, A_WIDTH, seq), F32),
        compiler_params=_cparams("arbitrary", "arbitrary"),
        name="slc_attend",
    )(pa, rows, rows_t, sel, tab)


WIN_SPAN = WINDOW + TQ


def _win_attend_kernel(q_ref, k_ref, vt_ref, bias_ref, o_ref):
    k = pl.program_id(1)
    key0 = pl.multiple_of(jnp.maximum(k * TQ - WINDOW, 0), TQ)
    row0 = pl.multiple_of(jnp.maximum(WINDOW - k * TQ, 0), TQ)
    kvs = range(A_KV)
    s = [_nt(k_ref[0, pl.ds(key0, WIN_SPAN), kv * LANE:kv * LANE + A_HD].astype(BF16), _stacked_queries(q_ref, kv))
         + bias_ref[kv, pl.ds(row0, WIN_SPAN), :] for kv in kvs]
    p = [_softmax_keys_on_rows(s[kv]) for kv in kvs]
    o = [_dot(vt_ref[0, kv * LANE + A_HD:(kv + 1) * LANE, pl.ds(key0, WIN_SPAN)].astype(BF16), p[kv].astype(BF16))
         for kv in kvs]
    for kv in kvs:
        for g in range(A_GROUP):
            h = kv * A_GROUP + g
            o_ref[0, h * A_HD:(h + 1) * A_HD, :] = o[kv][:, g * TQ:(g + 1) * TQ]


def _win_attend(pa, rows, rows_t, bias_w, batch, seq):
    nq = seq // TQ
    assert seq >= WIN_SPAN
    return pl.pallas_call(
        _win_attend_kernel,
        grid=(batch, nq),
        in_specs=[pl.BlockSpec((TQ, A_WIDTH), lambda b, k: (b * nq + k, 0)),
                  pl.BlockSpec((1, seq, ROW_W), lambda b, k: (b, 0, 0)),
                  pl.BlockSpec((1, ROW_W, seq), lambda b, k: (b, 0, 0)),
                  pl.BlockSpec((A_KV, WIN_SPAN + WINDOW, GROUP_LANES), lambda b, k: (0, 0, 0))],
        out_specs=pl.BlockSpec((1, A_WIDTH, TQ), lambda b, k: (b, 0, k)),
        out_shape=jax.ShapeDtypeStruct((batch, A_WIDTH, seq), F32),
        compiler_params=_cparams("arbitrary", "arbitrary"),
        name="win_attend",
    )(pa, rows, rows_t, bias_w)


def _out_tail(x, mix_m, mix_a, gate, w_ref, b_ref, g_ref, beta_ref):
    y = (_dot(mix_m.astype(BF16), w_ref[:M_WIDTH]) + _dot(mix_a.astype(BF16), w_ref[M_WIDTH:]) + b_ref[...])
    return _ln_rows(DEEPNORM_ALPHA * x + gate * y) * g_ref[...] + beta_ref[...]


OUT_TM = 512


def _out_prompt_kernel(x_ref, mm_ref, oc_ref, os_ref, ow_ref, ga_ref, za_ref, gate_ref,
                       w_ref, b_ref, g_ref, beta_ref, y_ref):
    sig = jax.nn.sigmoid(ga_ref[...].T)
    parts = []
    for h in range(A_HEADS):
        hs = slice(h * A_HD, (h + 1) * A_HD)
        parts.append(sig[h:h + 1] * oc_ref[0, hs, :] + sig[A_HEADS + h:A_HEADS + h + 1] * os_ref[0, hs, :]
                     + sig[2 * A_HEADS + h:2 * A_HEADS + h + 1] * ow_ref[0, hs, :])
    ha = jnp.concatenate(parts, axis=0).T
    mix_a = ha * _silu(za_ref[...])
    y_ref[...] = _out_tail(x_ref[...], mm_ref[...], mix_a, gate_ref[...], w_ref, b_ref, g_ref, beta_ref)


def _out_prompt(x, mix_m, o_c, o_s, o_w, pa, gate, w_out, b_out, ln_g, ln_b, batch, seq):
    tm = OUT_TM
    nq = seq // tm
    rows = batch * seq
    branch = pl.BlockSpec((1, A_WIDTH, tm), lambda i: (i // nq, 0, i % nq))
    vec = pl.BlockSpec((1, D_MODEL), lambda i: (0, 0))
    return pl.pallas_call(
        _out_prompt_kernel,
        grid=(rows // tm,),
        in_specs=[pl.BlockSpec((tm, D_MODEL), lambda i: (i, 0)),
                  pl.BlockSpec((tm, M_WIDTH), lambda i: (i, 0)),
                  branch, branch, branch,
                  pl.BlockSpec((tm, LANE), lambda i: (i, 2 * A_WIDTH // LANE)),
                  pl.BlockSpec((tm, A_WIDTH), lambda i: (i, 1)),
                  pl.BlockSpec((None, 1, D_MODEL), lambda i: (i // nq, 0, 0)),
                  pl.BlockSpec((D_MODEL, D_MODEL), lambda i: (0, 0)),
                  vec, vec, vec],
        out_specs=pl.BlockSpec((tm, D_MODEL), lambda i: (i, 0)),
        out_shape=jax.ShapeDtypeStruct((rows, D_MODEL), F32),
        compiler_params=_cparams("arbitrary"),
        name="out_prompt",
    )(x, mix_m, o_c, o_s, o_w, pa, pa, gate, w_out, b_out, ln_g, ln_b)


def _out_sample_kernel(x_ref, mm_ref, ha_ref, za_ref, gate_ref, w_ref, b_ref, g_ref, beta_ref, y_ref):
    mix_a = ha_ref[...] * _silu(za_ref[...])
    y_ref[...] = _out_tail(x_ref[...], mm_ref[...], mix_a, gate_ref[...], w_ref, b_ref, g_ref, beta_ref)


def _out_sample(x, mix_m, ha, pa, gate, w_out, b_out, ln_g, ln_b):
    rows = x.shape[0]
    vec = pl.BlockSpec((1, D_MODEL), lambda i: (0, 0))
    return pl.pallas_call(
        _out_sample_kernel,
        grid=(1,),
        in_specs=[pl.BlockSpec((rows, D_MODEL), lambda i: (0, 0)),
                  pl.BlockSpec((rows, M_WIDTH), lambda i: (0, 0)),
                  pl.BlockSpec((rows, A_WIDTH), lambda i: (0, 0)),
                  pl.BlockSpec((rows, A_WIDTH), lambda i: (0, 1)),
                  pl.BlockSpec((rows, D_MODEL), lambda i: (0, 0)),
                  pl.BlockSpec((D_MODEL, D_MODEL), lambda i: (0, 0)),
                  vec, vec, vec],
        out_specs=pl.BlockSpec((rows, D_MODEL), lambda i: (0, 0)),
        out_shape=jax.ShapeDtypeStruct((rows, D_MODEL), F32),
        compiler_params=_cparams("arbitrary"),
        name="out_sample",
    )(x, mix_m, ha, pa, gate, w_out, b_out, ln_g, ln_b)


def _nsa_prompt(rel_bias, pa, rc, rs, rw, st, wt, bias, offs, cmp_w, const_row, batch, seq):
    wcat, w2bd = cmp_w
    nc = seq // CMP_STRIDE
    ns = seq // SLC_BLOCK
    kk, kvt = _compress_prompt(rc, wcat, const_row, w2bd, batch, seq)

    def group_lanes(tiles):
        rows = tiles.shape[1]
        return tiles.reshape(A_KV, A_GROUP, rows, TQ).transpose(0, 2, 1, 3).reshape(A_KV, rows, GROUP_LANES)

    pat = group_lanes(bias[:, offs[0]:offs[2]]).reshape(A_KV, 2, CMP_PAT, GROUP_LANES).transpose(1, 0, 2, 3)
    far = jnp.repeat(rel_bias[FAR_BUCKET].reshape(A_KV, A_GROUP), TQ, axis=1).reshape(A_KV, 1, GROUP_LANES)
    cov_t = jnp.asarray(_cover_np(nc, nc - 1, ns, ns).T)
    o_c, sel = _cmp_attend(far, pa, kk, kvt, pat, cov_t, batch, seq)

    tab = group_lanes(bias[:, offs[2]:offs[5]]).reshape(A_KV, 3, TQ, GROUP_LANES).transpose(1, 0, 2, 3)
    tab = jnp.concatenate([tab * LOG2E, jnp.full((1,) + tab.shape[1:], NEG, F32)], axis=0)
    o_s = _slc_attend(pa, rs, st, sel, tab, batch, seq)
    o_w = _win_attend(pa, rw.reshape(batch, seq, ROW_W), wt, group_lanes(bias[:, offs[5]:offs[6]]), batch, seq)
    return o_c, o_s, o_w, sel


HALVES_PER_PAGE = 8
SEL_LANES = 256


def _gather_pages(copy_of, n_copies):
    b = pl.program_id(0)
    slot = b % 2

    @pl.when(b == 0)
    def _():
        for i in range(n_copies):
            copy_of(0, i, 0).start()

    @pl.when(b + 1 < pl.num_programs(0))
    def _():
        for i in range(n_copies):
            copy_of(b + 1, i, 1 - slot).start()

    for i in range(n_copies):
        copy_of(b, i, slot).wait()
    return slot


def _cmp_sample_kernel(pt_ref, cache_ref, q_ref, perm_ref, wcat_ref, const_ref, w2_ref, bias_ref, cov_ref,
                       o_ref, idx_ref, buf, sem, *, n_pages, p_len):
    def page_copy(seq, j, slot):
        return pltpu.make_async_copy(cache_ref.at[pt_ref[seq * n_pages + j]], buf.at[slot, j], sem.at[slot])

    slot = _gather_pages(page_copy, n_pages)
    pages = [buf.at[slot, j] for j in range(n_pages)]
    n_half = n_pages * HALVES_PER_PAGE
    perm = perm_ref[...]
    group = 8

    def move_lanes(g):
        tiles = [pages[j][kv].reshape(KVROW_W, PAGE_ROWS).astype(BF16)
                 for j in range(g * group, (g + 1) * group) for kv in range(A_KV)]
        return _dot(jnp.concatenate(tiles, axis=0), perm)

    blocks = [[], []]
    n_groups = n_pages // group
    moved = move_lanes(0)
    for g in range(n_groups):
        moved_next = move_lanes(g + 1) if g + 1 < n_groups else None
        for i in range(group * A_KV):
            kv = i % A_KV
            rows = moved[i * KVROW_W:(i + 1) * KVROW_W].T
            blocks[kv].append(jnp.concatenate(
                [rows[p * HALVES_PER_PAGE:(p + 1) * HALVES_PER_PAGE] for p in range(CMP_STRIDE)], axis=1))
        moved = moved_next
    halves = jnp.concatenate(blocks[0] + blocks[1], axis=0)
    kc = _compress_halves(halves, wcat_ref, const_ref, w2_ref, n_half)
    kc0, kc1 = kc[:n_half], kc[n_half:]
    q8 = q_ref[0]
    first = lax.broadcasted_iota(jnp.int32, (A_HEADS, n_half), 0) < A_GROUP

    def logits(keys):
        return lax.dot_general(q8, keys, (((1,), (1,)), ((), ())), preferred_element_type=F32, precision=HIGHEST)

    s = jnp.where(first, logits(kc0[:, :A_HD]), logits(kc1[:, :A_HD])) * ATT_SCALE + bias_ref[...]
    e, _, inv = _softmax_keys_on_lanes(s)
    p = e * inv
    pb = p.astype(BF16)
    first_o = lax.broadcasted_iota(jnp.int32, (A_HEADS, A_HD), 0) < A_GROUP
    o_ref[0] = jnp.where(first_o, _dot(pb, kc0[:, A_HD:].astype(BF16)), _dot(pb, kc1[:, A_HD:].astype(BF16)))
    hrow = lax.broadcasted_iota(jnp.int32, (A_HEADS, n_half), 0)
    imp0 = jnp.sum(jnp.where(first, p, 0.0), axis=0, keepdims=True)
    imp1 = jnp.sum(jnp.where(first, 0.0, p), axis=0, keepdims=True)
    imp = jnp.where(hrow == 0, imp0, jnp.where(hrow == 1, imp1, 0.0))
    score = _dot(imp, cov_ref[...], HIGHEST)
    n_slc = p_len // SLC_BLOCK + 1
    cur = p_len // SLC_BLOCK
    lane = lax.broadcasted_iota(jnp.int32, (A_HEADS, SEL_LANES), 1)
    forced = (lane == 0) | (lane == cur) | (lane == cur - 1)
    valid = lane * SLC_BLOCK <= p_len
    sc = jnp.where(forced, jnp.inf, jnp.where(valid, score, -jnp.inf))
    k_sel = float(min(N_SEL, n_slc))
    sub = lax.broadcasted_iota(jnp.int32, (SEL_LANES, SEL_LANES), 0)
    lan = lax.broadcasted_iota(jnp.int32, (SEL_LANES, SEL_LANES), 1)
    slot_l = lax.broadcasted_iota(jnp.int32, (SEL_LANES, LANE), 1).astype(F32)
    blk_s = lax.broadcasted_iota(jnp.int32, (SEL_LANES, LANE), 0).astype(F32)
    out_row = lax.broadcasted_iota(jnp.int32, (A_HEADS, LANE), 0)
    picks = jnp.zeros((A_HEADS, LANE), F32)
    for kv in range(A_KV):
        row = sc[kv:kv + 1, :]
        col = jnp.sum(jnp.where(sub == lan, row, 0.0), axis=1, keepdims=True)
        before_c = (lan < n_slc) & ((row > col) | ((row == col) & (lan < sub)))
        sel_c = (jnp.sum(before_c.astype(F32), axis=1, keepdims=True) < k_sel) & (sub[:, :1] < n_slc)
        before_r = (sub < n_slc) & ((col > row) | ((col == row) & (sub < lan)))
        sel_r = (jnp.sum(before_r.astype(F32), axis=0, keepdims=True) < k_sel) & (lan[:1] < n_slc)
        slot_c = jnp.sum(jnp.where((lan < sub) & sel_r, 1.0, 0.0), axis=1, keepdims=True)
        hit = sel_c & (slot_c == slot_l)
        picks_kv = jnp.sum(jnp.where(hit, blk_s, 0.0), axis=0, keepdims=True)
        picks = jnp.where(out_row == kv, picks_kv, picks)
    idx_ref[0] = picks.astype(jnp.int32)


def _cmp_sample(page_table, cache_pages, q3, wcat, const_row, w2bd, bias_cs, cov, p_len):
    nb, n_pages = page_table.shape
    n_half = n_pages * HALVES_PER_PAGE

    r = np.arange(PAGE_ROWS)
    perm = np.zeros((PAGE_ROWS, PAGE_ROWS), np.float32)
    perm[r, (r % CMP_STRIDE) * HALVES_PER_PAGE + r // CMP_STRIDE] = 1.0
    const2 = lambda b, pt: (0, 0)
    grid_spec = pltpu.PrefetchScalarGridSpec(
        num_scalar_prefetch=1,
        grid=(nb,),
        in_specs=[pl.BlockSpec(memory_space=pl.ANY),
                  pl.BlockSpec((1, A_HEADS, A_HD), lambda b, pt: (b, 0, 0)),
                  pl.BlockSpec((PAGE_ROWS, PAGE_ROWS), const2),
                  pl.BlockSpec((CMP_STRIDE * KVROW_W, HALF_W), const2),
                  pl.BlockSpec((1, KVROW_W), const2),
                  pl.BlockSpec((KVROW_W, KVROW_W), const2),
                  pl.BlockSpec((A_HEADS, n_half), const2),
                  pl.BlockSpec((n_half, SEL_LANES), const2)],
        out_specs=[pl.BlockSpec((1, A_HEADS, A_HD), lambda b, pt: (b, 0, 0)),
                   pl.BlockSpec((1, A_HEADS, LANE), lambda b, pt: (b, 0, 0))],
        scratch_shapes=[pltpu.VMEM((2, n_pages) + cache_pages.shape[1:], F32),
                        pltpu.SemaphoreType.DMA((2,))],
    )
    return pl.pallas_call(
        functools.partial(_cmp_sample_kernel, n_pages=n_pages, p_len=p_len),
        grid_spec=grid_spec,
        out_shape=[jax.ShapeDtypeStruct((nb, A_HEADS, A_HD), F32),
                   jax.ShapeDtypeStruct((nb, A_HEADS, LANE), jnp.int32)],
        compiler_params=_cparams("arbitrary"),
        name="cmp_sample",
    )(page_table.reshape(-1), cache_pages, q3, jnp.asarray(perm, BF16), wcat, const_row, w2bd, bias_cs, cov)


PAGE_ROWS = 128
BLOCKS_PER_PAGE = PAGE_ROWS // SLC_BLOCK


def _slc_sample_kernel(idx_ref, pt_ref, cache_ref, q_ref, snew_ref, win_ref, wnew_ref, wcol_ref, oc_ref, g_ref,
                       rbt_ref, bw_ref, near_ref, ha_ref, wbuf_ref, buf, sem, *, n_pages, p_len):
    n_blk = A_KV * N_SEL
    past_blocks = p_len // SLC_BLOCK

    def block_copy(seq, i, slot):
        kv = i // N_SEL
        blk = jnp.minimum(idx_ref[seq * n_blk + i], past_blocks - 1)
        page = pt_ref[seq * n_pages + blk // BLOCKS_PER_PAGE]
        return pltpu.make_async_copy(cache_ref.at[page, kv], buf.at[slot, i], sem.at[slot])

    slot = _gather_pages(block_copy, n_blk)
    blocks = [buf.at[slot, i] for i in range(n_blk)]
    b = pl.program_id(0)
    n_keys = N_SEL * PAGE_ROWS
    n_buf = win_ref.shape[-1]
    qf = q_ref[0]
    q8 = qf.astype(BF16)
    first_o = lax.broadcasted_iota(jnp.int32, (A_HEADS, A_HD), 0) < A_GROUP
    lane = lax.broadcasted_iota(jnp.int32, (1, n_keys), 1)
    slot = lane // PAGE_ROWS
    in_page = lane % PAGE_ROWS
    near = jnp.concatenate([near_ref[...]] * N_SEL, axis=1)
    far = rbt_ref[:, FAR_BUCKET:FAR_BUCKET + 1]
    bias_new = rbt_ref[:, 0:1]

    def new_key_logit(row_ref, kv):
        k_new = row_ref[0, :, kv * LANE:kv * LANE + A_HD]
        v_new = row_ref[0, :, kv * LANE + A_HD:(kv + 1) * LANE]
        return jnp.sum(qf * k_new, axis=1, keepdims=True) * ATT_SCALE + bias_new, v_new

    o_s, o_w = [], []
    for kv in range(A_KV):
        blk_of = jnp.zeros((1, n_keys), jnp.int32)
        has_new = False
        for j in range(N_SEL):
            blk = idx_ref[(b * A_KV + kv) * N_SEL + j]
            blk_of = jnp.where(slot == j, blk, blk_of)
            has_new = jnp.logical_or(has_new, blk == past_blocks)
        page_of = jnp.minimum(blk_of, past_blocks - 1) // BLOCKS_PER_PAGE
        pos = page_of * PAGE_ROWS + in_page
        valid = (pos // SLC_BLOCK == blk_of) & (pos < p_len)
        bias = jnp.where(page_of == n_pages - 1, near, far)
        kt = jnp.concatenate([blocks[kv * N_SEL + j][0] for j in range(N_SEL)], axis=1).astype(BF16)
        vt = jnp.concatenate([blocks[kv * N_SEL + j][1] for j in range(N_SEL)], axis=1).astype(BF16)
        s = jnp.where(valid, _dot(q8, kt) * ATT_SCALE + bias, NEG)
        s_new, v_new = new_key_logit(snew_ref, kv)
        s_new = jnp.where(has_new, s_new, NEG)
        e, e_new, inv = _softmax_keys_on_lanes(s, s_new)
        o_s.append((_nt(e.astype(BF16), vt) + e_new * v_new) * inv)
        sw = _dot(q8, win_ref[0, kv, 0].astype(BF16)) * ATT_SCALE + bw_ref[...]
        sw_new, vw_new = new_key_logit(wnew_ref, kv)
        e, e_new, inv = _softmax_keys_on_lanes(sw, sw_new)
        o_w.append((_nt(e.astype(BF16), win_ref[0, kv, 1].astype(BF16)) + e_new * vw_new) * inv)
    g = jax.nn.sigmoid(g_ref[0])
    ha_ref[0] = (g[0] * oc_ref[0] + g[1] * jnp.where(first_o, o_s[0], o_s[1])
                 + g[2] * jnp.where(first_o, o_w[0], o_w[1]))
    last = lax.broadcasted_iota(jnp.int32, (A_HD, n_buf), 1) == n_buf - 1
    for kv in range(A_KV):
        for c in range(2):
            r0 = (kv * 2 + c) * A_HD
            wbuf_ref[0, kv, c] = jnp.where(last, wcol_ref[0, r0:r0 + A_HD, :],
                                           pltpu.roll(win_ref[0, kv, c], n_buf - 1, 1))


def _slc_sample(idx, page_table, cache_t, q3, slc_new, win_t, win_new, win_new_col, o_c, gates, rb_t, bias_ws,
                bias_near, p_len):
    nb, n_pages = page_table.shape
    n_buf = win_t.shape[-1]
    per_seq3 = lambda b, i, p: (b, 0, 0)
    win_spec = pl.BlockSpec((1, A_KV, 2, A_HD, n_buf), lambda b, i, p: (b, 0, 0, 0, 0))
    grid_spec = pltpu.PrefetchScalarGridSpec(
        num_scalar_prefetch=2,
        grid=(nb,),
        in_specs=[pl.BlockSpec(memory_space=pl.ANY),
                  pl.BlockSpec((1, A_HEADS, A_HD), per_seq3),
                  pl.BlockSpec((1, 1, ROW_W), per_seq3),
                  win_spec,
                  pl.BlockSpec((1, 1, ROW_W), per_seq3),
                  pl.BlockSpec((1, ROW_W, 1), per_seq3),
                  pl.BlockSpec((1, A_HEADS, A_HD), per_seq3),
                  pl.BlockSpec((1, 3, A_HEADS, 1), lambda b, i, p: (b, 0, 0, 0)),
                  pl.BlockSpec((A_HEADS, N_BUCKETS), lambda b, i, p: (0, 0)),
                  pl.BlockSpec((A_HEADS, n_buf), lambda b, i, p: (0, 0)),
                  pl.BlockSpec((A_HEADS, PAGE_ROWS), lambda b, i, p: (0, 0))],
        out_specs=[pl.BlockSpec((1, A_HEADS, A_HD), per_seq3), win_spec],
        scratch_shapes=[pltpu.VMEM((2, A_KV * N_SEL) + cache_t.shape[2:], F32),
                        pltpu.SemaphoreType.DMA((2,))],
    )
    return pl.pallas_call(
        functools.partial(_slc_sample_kernel, n_pages=n_pages, p_len=p_len),
        grid_spec=grid_spec,
        out_shape=[jax.ShapeDtypeStruct((nb, A_HEADS, A_HD), F32),
                   jax.ShapeDtypeStruct(win_t.shape, F32)],
        compiler_params=_cparams("arbitrary"),
        name="slc_win_sample",
    )(idx.reshape(-1), page_table.reshape(-1), cache_t, q3, slc_new, win_t, win_new, win_new_col, o_c, gates,
      rb_t, bias_ws, bias_near)


def _rows_last(a):
    n = a.ndim
    return a.transpose(*range(n - 4), n - 3, n - 2, n - 1, n - 4)


def _rows_first(a):
    n = a.ndim
    return a.transpose(*range(n - 4), n - 1, n - 4, n - 3, n - 2)


def _nsa_sample(rel_bias, pa, rs_new, rw_new, cache_cmp, cache_slc, win_cache, page_table,
                bias, offs, cmp_w, const_row, p_len):
    wcat, w2bd = cmp_w
    nb, n_pages = page_table.shape
    n_half = p_len // CMP_STRIDE
    n_slc = p_len // SLC_BLOCK + 1
    n_buf = win_cache.shape[1]
    q3 = pa[:, :A_WIDTH].reshape(nb, A_HEADS, A_HD)
    gates = pa[:, 2 * A_WIDTH:2 * A_WIDTH + 3 * A_HEADS].reshape(nb, 3, A_HEADS, 1)
    bias_cs = bias[:, offs[6]:offs[7]].reshape(A_HEADS, -1)[:, :n_half]
    bias_ws = bias[:, offs[7]:offs[8]].reshape(A_HEADS, -1)[:, :n_buf]
    cov = jnp.asarray(_cover_np(n_half, n_half - 1, SEL_LANES, n_slc))
    o_c, picks = _cmp_sample(page_table, _rows_last(cache_cmp), q3, wcat, const_row, w2bd, bias_cs, cov, p_len)
    idx = picks[:, :A_KV, :N_SEL]
    ha, wbuf = _slc_sample(idx, page_table, _rows_last(cache_slc), q3, rs_new.reshape(nb, 1, ROW_W),
                           _rows_last(win_cache), rw_new.reshape(nb, 1, ROW_W), rw_new.reshape(nb, ROW_W, 1),
                           o_c, gates, rel_bias.T, bias_ws, bias[:, offs[8]], p_len)
    return ha.reshape(nb, A_WIDTH), idx, wbuf


def kernel(x_prompt, x_sample, cache_cmp_kv, cache_slc_kv, cache_win_kv, state_mlstm_C, state_mlstm_n, state_mlstm_m, page_table, c_prompt, c_sample, rel_bias, w_ada, b_ada, w_in, b_in, m_norm_g, cmp_pe, cmp_w1, cmp_b1, cmp_w2, w_out, b_out, ln_g, ln_b):
    B, T, _ = x_prompt.shape
    NB = x_sample.shape[0]
    n_pages = page_table.shape[1]
    p_len = n_pages * PAGE_ROWS
    depth = w_in.shape[0]
    assert depth == 1 and x_sample.shape[1] == 1 and cache_win_kv.shape[2] == WINDOW
    ids, offs = _static_ids(p_len)
    bias = _bias_tables(rel_bias, ids)
    x_p = x_prompt.reshape(B * T, D_MODEL)
    x_s = x_sample.reshape(NB, D_MODEL)
    l = 0
    n_mod = -(-(B + NB) // SUBLANE) * SUBLANE
    c_all = jnp.concatenate([c_prompt, c_sample, jnp.zeros((n_mod - B - NB, D_MODEL), F32)])
    shift, scale, gate = jnp.split(_adaln_mod(c_all, w_ada[l], b_ada[l]), 3, axis=-1)
    packed = _pack_in_proj(w_in[l], b_in[l], BF16)
    packed_f32 = _pack_in_proj(w_in[l], b_in[l], F32)
    cmp_w = _pack_compress(cmp_w1[l], cmp_w2[l])
    const_row = _compress_const(cmp_pe[l], cmp_w1[l], cmp_b1[l])
    w_out_b = w_out[l].astype(BF16)
    vecs = (b_out[l].reshape(1, -1), ln_g[l].reshape(1, -1), ln_b[l].reshape(1, -1))
    pm, pa, rc, rs, rw, ct, st, wt = _project(x_p, shift[:B, None], scale[:B, None], packed, B, 256)
    mix_m, c_p, n_p, m_p = _mlstm_prompt(pm, m_norm_g[l], B, T)
    o_c, o_s, o_w, _ = _nsa_prompt(rel_bias, pa, rc, rs, rw, st, wt, bias, offs, cmp_w, const_row, B, T)
    y_p = _out_prompt(x_p, mix_m, o_c, o_s, o_w, pa, gate[:B, None], w_out_b, *vecs, B, T)
    pm_s, pa_s, _, rs_s, rw_s, ct_s, st_s, wt_s = _project(x_s, shift[B:B + NB], scale[B:B + NB], packed_f32, 1, NB)
    mix_s, c_s, n_s, m_s = _mlstm_sample(pm_s, m_norm_g[l], state_mlstm_C[l], state_mlstm_n[l], state_mlstm_m[l])
    ha_s, _, wbuf_s = _nsa_sample(rel_bias, pa_s, rs_s, rw_s, cache_cmp_kv[l], cache_slc_kv[l],
                                  cache_win_kv[l], page_table, bias, offs, cmp_w, const_row, p_len)
    y_s = _out_sample(x_s, mix_s.reshape(NB, M_WIDTH), ha_s, pa_s, gate[B:B + NB], w_out_b, *vecs)

    def kv_prompt(a):
        return _rows_first(a.reshape(1, B, A_KV, 2, A_HD, a.shape[-1]))

    def kv_sample(a):
        return a.reshape(1, 1, A_KV, 2, A_HD, NB).transpose(0, 5, 1, 2, 3, 4)

    return (y_p.reshape(B, T, D_MODEL), y_s.reshape(NB, 1, D_MODEL),
            kv_prompt(ct), kv_sample(ct_s), kv_prompt(st), kv_sample(st_s),
            kv_prompt(wt[:, :, T - WINDOW:]), _rows_first(wbuf_s)[None],
            c_p[None], c_s[None], n_p[None], n_s[None], m_p[None, :, :, 0], m_s[None])
```

```python
import functools
import math

import numpy as np
import jax
import jax.numpy as jnp
from jax import lax
from jax.experimental import pallas as pl
from jax.experimental.pallas import tpu as pltpu

F32 = jnp.float32
BF16 = jnp.bfloat16
HIGHEST = lax.Precision.HIGHEST

D_MODEL = 1024
M_HEADS = 4
M_HD = 128
M_WIDTH = M_HEADS * M_HD
M_CHUNK = 128
A_HEADS = 8
A_HD = 64
A_KV = 2
A_GROUP = A_HEADS // A_KV
A_WIDTH = A_HEADS * A_HD
A_KVW = A_KV * A_HD
ROW_W = 2 * A_KVW
CMP_LEN = 32
CMP_STRIDE = 16
SLC_BLOCK = 64
N_SEL = 16
WINDOW = 512
N_BUCKETS = 32
MAX_EXACT = N_BUCKETS // 2
MAX_DIST = 128
FAR_BUCKET = N_BUCKETS - 1
LN_EPS = 1e-5
ATT_SCALE = A_HD ** -0.5
DEPTH = 1
DEEPNORM_ALPHA = (2.0 * DEPTH) ** 0.25
IN_SPLITS = (M_WIDTH,) * 5 + (M_HEADS, M_HEADS) + (A_WIDTH,) + (A_KVW,) * 6 + (3 * A_HEADS, A_WIDTH)

LANE = 128
SUBLANE = 8
TQ = 128
NEG = -1e30
LOG2E = math.log2(math.e)
MASKED_ID = N_BUCKETS
VMEM_LIMIT = 56 * 1024 * 1024

PM_W = 5 * M_WIDTH + LANE
PA_W = 2 * A_WIDTH + LANE
PW_TOTAL = PM_W + PA_W + 3 * ROW_W


def _cparams(*sem):
    return pltpu.CompilerParams(dimension_semantics=sem, vmem_limit_bytes=VMEM_LIMIT)


def _nt(a, b):
    return lax.dot_general(a, b, (((1,), (1,)), ((), ())), preferred_element_type=F32)


def _dot(a, b, precision=None):
    return jnp.dot(a, b, preferred_element_type=F32, precision=precision)


def _log_sigmoid(x):
    return jnp.minimum(x, 0.0) - jnp.log(1.0 + jnp.exp(-jnp.abs(x)))


def _silu(x):
    return x * jax.nn.sigmoid(x)


def _gelu_tanh(x):
    return 0.5 * x * (1.0 + jnp.tanh(math.sqrt(2.0 / math.pi) * (x + 0.044715 * (x * x * x))))


def _ln_rows(x):
    mu = jnp.mean(x, axis=-1, keepdims=True)
    xc = x - mu
    var = jnp.mean(xc * xc, axis=-1, keepdims=True)
    return xc * lax.rsqrt(var + LN_EPS)


def _bucket_np(dist):
    dist = np.asarray(dist, np.int64)
    n = np.maximum(dist, 0)
    nf = np.maximum(n, 1).astype(np.float32)
    large = MAX_EXACT + (np.log(nf / np.float32(MAX_EXACT)) / np.float32(math.log(MAX_DIST / MAX_EXACT))
                         * np.float32(N_BUCKETS - MAX_EXACT)).astype(np.int32)
    large = np.minimum(large, N_BUCKETS - 1)
    b = np.where(n < MAX_EXACT, n, large)
    return np.where(dist < 0, MASKED_ID, b).astype(np.int32)


def _mod_kernel(c_ref, w_ref, b_ref, o_ref):
    a = _silu(c_ref[...])
    o_ref[...] = _dot(a, w_ref[...]) + b_ref[...]


def _adaln_mod(c, w_ada, b_ada):
    rows = c.shape[0]
    n3 = w_ada.shape[1]
    tn = D_MODEL
    return pl.pallas_call(
        _mod_kernel,
        grid=(n3 // tn,),
        in_specs=[pl.BlockSpec((rows, D_MODEL), lambda j: (0, 0)),
                  pl.BlockSpec((D_MODEL, tn), lambda j: (0, j)),
                  pl.BlockSpec((1, tn), lambda j: (0, j))],
        out_specs=pl.BlockSpec((rows, tn), lambda j: (0, j)),
        out_shape=jax.ShapeDtypeStruct((rows, n3), F32),
        compiler_params=_cparams("arbitrary"),
        name="adaln_mod",
    )(c, w_ada, b_ada.reshape(1, n3))


def _bias_kernel(rb_ref, ids_ref, o_ref, *, n_groups):
    def body(i, carry):
        r0 = pl.multiple_of(i * SUBLANE, SUBLANE)
        ids = ids_ref[pl.ds(r0, SUBLANE), :]
        for h in range(A_HEADS):
            acc = jnp.full((SUBLANE, LANE), NEG, F32)
            for b in range(N_BUCKETS):
                acc = jnp.where(ids == b, rb_ref[b, h], acc)
            o_ref[h, pl.ds(r0, SUBLANE), :] = acc
        return carry

    lax.fori_loop(0, n_groups, body, 0)


def _bias_tables(rel_bias, ids):
    rows = ids.shape[0]
    return pl.pallas_call(
        functools.partial(_bias_kernel, n_groups=rows // SUBLANE),
        in_specs=[pl.BlockSpec(memory_space=pltpu.SMEM),
                  pl.BlockSpec((rows, LANE), lambda: (0, 0))],
        out_specs=pl.BlockSpec((A_HEADS, rows, LANE), lambda: (0, 0, 0)),
        out_shape=jax.ShapeDtypeStruct((A_HEADS, rows, LANE), F32),
        name="bias_tables",
    )(rel_bias, jnp.asarray(ids))


def _pack_in_proj(w_in, b_in, dtype):
    offs = np.cumsum((0,) + IN_SPLITS)
    names = ("mq", "mk", "mv", "mo", "mz", "mi", "mf", "aq", "ck", "cv", "sk", "sv", "wk", "wv", "ga", "za")
    sl = {n: (int(offs[i]), int(offs[i + 1])) for i, n in enumerate(names)}

    def cols(a, name, lo=None, hi=None):
        s, e = sl[name]
        if lo is not None:
            s, e = s + lo, s + hi
        return a[..., s:e]

    def rows_of(a, kn, vn):
        return [cols(a, kn, 0, A_HD), cols(a, vn, 0, A_HD), cols(a, kn, A_HD, 2 * A_HD), cols(a, vn, A_HD, 2 * A_HD)]

    def pack(a):
        def zeros(n):
            return jnp.zeros(a.shape[:-1] + (n,), a.dtype)
        parts = [cols(a, n) for n in ("mq", "mk", "mv", "mo", "mz")]
        parts += [cols(a, "mi"), cols(a, "mf"), zeros(LANE - 2 * M_HEADS)]
        parts += [cols(a, "aq"), cols(a, "za"), cols(a, "ga"), zeros(LANE - 3 * A_HEADS)]
        parts += rows_of(a, "ck", "cv") + rows_of(a, "sk", "sv") + rows_of(a, "wk", "wv")
        return jnp.concatenate(parts, axis=-1)

    w = pack(w_in)
    b = pack(b_in.reshape(1, -1))
    wt = w[:, PM_W + PA_W:].T
    bt = b[:, PM_W + PA_W:].reshape(-1, 1)
    return w.astype(dtype), b, wt.astype(dtype), bt


def _proj_kernel(x_ref, sh_ref, sc_ref, w_ref, b_ref, wt_ref, bt_ref,
                 om_ref, oa_ref, oc_ref, os_ref, ow_ref, oct_ref, ost_ref, owt_ref):
    h = _ln_rows(x_ref[...]) * (1.0 + sc_ref[...]) + sh_ref[...]
    hb = h.astype(w_ref.dtype)
    precision = HIGHEST if w_ref.dtype == F32 else None
    lo = 0
    for o_ref in (om_ref, oa_ref, oc_ref, os_ref, ow_ref):
        n = o_ref.shape[-1]
        o_ref[...] = _dot(hb, w_ref[:, lo:lo + n], precision) + b_ref[:, lo:lo + n]
        lo += n
    t = lax.dot_general(wt_ref[...], hb, (((1,), (1,)), ((), ())), preferred_element_type=F32,
                        precision=precision) + bt_ref[...]
    for i, o_ref in enumerate((oct_ref, ost_ref, owt_ref)):
        o_ref[0] = t[i * ROW_W:(i + 1) * ROW_W]


def _project(x, shift, scale, packed, groups, tm):
    w, b, wt, bt = packed
    rows = x.shape[0]
    per = rows // groups // tm
    if shift.ndim == 3:
        mod_spec = pl.BlockSpec((None, 1, D_MODEL), lambda i: (i // per, 0, 0))
    else:
        mod_spec = pl.BlockSpec((tm, D_MODEL), lambda i: (i, 0))
    widths = (PM_W, PA_W, ROW_W, ROW_W, ROW_W)
    return pl.pallas_call(
        _proj_kernel,
        grid=(rows // tm,),
        in_specs=[pl.BlockSpec((tm, D_MODEL), lambda i: (i, 0)), mod_spec, mod_spec,
                  pl.BlockSpec((D_MODEL, PW_TOTAL), lambda i: (0, 0)),
                  pl.BlockSpec((1, PW_TOTAL), lambda i: (0, 0)),
                  pl.BlockSpec((3 * ROW_W, D_MODEL), lambda i: (0, 0)),
                  pl.BlockSpec((3 * ROW_W, 1), lambda i: (0, 0))],
        out_specs=[pl.BlockSpec((tm, n), lambda i: (i, 0)) for n in widths]
                  + [pl.BlockSpec((1, ROW_W, tm), lambda i: (i // per, 0, i % per))] * 3,
        out_shape=[jax.ShapeDtypeStruct((rows, n), F32) for n in widths]
                  + [jax.ShapeDtypeStruct((groups, ROW_W, rows // groups), F32)] * 3,
        compiler_params=_cparams("arbitrary"),
        name="in_proj",
    )(x, shift, scale, w, b, wt, bt)


def _mlstm_head_out(h, o_pre, z_pre, g_row):
    return jax.nn.sigmoid(o_pre) * (_ln_rows(h) * g_row) * _silu(z_pre)


def _mlstm_prompt_kernel(q_ref, k_ref, v_ref, o_ref, z_ref, g_ref, ng_ref, mix_ref, c_ref, n_ref, m_ref):
    L = M_CHUNK

    @pl.when(pl.program_id(0) == 0)
    def _():
        c_ref[...] = jnp.zeros_like(c_ref)
        n_ref[...] = jnp.zeros_like(n_ref)
        m_ref[...] = jnp.zeros_like(m_ref)

    row = lax.broadcasted_iota(jnp.int32, (L, L), 0)
    col = lax.broadcasted_iota(jnp.int32, (L, L), 1)
    tril = col <= row
    lower = tril.astype(F32)
    upper = (row <= col).astype(F32)
    nb = q_ref.shape[0]
    units = [(b, h) for b in range(nb) for h in range(M_HEADS)]
    gates = [g_ref[b] for b in range(nb)]
    gates_t = [g.T for g in gates]
    cum = [_dot(lower, _log_sigmoid(g), HIGHEST) for g in gates]
    cum_t = [_dot(_log_sigmoid(g), upper, HIGHEST) for g in gates_t]
    st = {}
    for b, h in units:
        hs = slice(h * M_HD, (h + 1) * M_HD)
        b_col = cum[b][:, M_HEADS + h:M_HEADS + h + 1]
        b_row = cum_t[b][M_HEADS + h:M_HEADS + h + 1, :]
        m_prev = m_ref[b, h:h + 1, 0:1]
        d = jnp.where(tril, b_col - b_row + gates_t[b][h:h + 1, :], NEG)
        inter = b_col + m_prev
        m_t = jnp.maximum(inter, jnp.max(d, axis=1, keepdims=True))
        q = q_ref[b, :, hs]
        ks = k_ref[b, :, hs] * (M_HD ** -0.5)
        st[b, h] = dict(hs=hs, b_col=b_col, m_prev=m_prev, d=d, m_t=m_t, w_inter=jnp.exp(inter - m_t), q=q, ks=ks,
                        qb=q.astype(BF16), kb=ks.astype(BF16), vb=v_ref[b, :, hs].astype(BF16),
                        c_prev=c_ref[b, h], n_prev=n_ref[b, h:h + 1, :])
    for u in units:
        s = st[u]
        s["qk"] = _nt(s["qb"], s["kb"]) * jnp.exp(s["d"] - s["m_t"])
        s["qc"] = _dot(s["qb"], s["c_prev"].astype(BF16))
    for u in units:
        s = st[u]
        num = s["w_inter"] * s["qc"] + _dot(s["qk"].astype(BF16), s["vb"])
        den = (s["w_inter"] * jnp.sum(s["q"] * s["n_prev"], axis=1, keepdims=True)
               + jnp.sum(s["qk"], axis=1, keepdims=True))
        s["hh"] = num / jnp.maximum(jnp.abs(den), jnp.exp(-s["m_t"]))
    for (b, h) in units:
        s = st[b, h]
        m_new = s["m_t"][L - 1:L, :]
        b_last = s["b_col"][L - 1:L, :]
        w_c = jnp.exp(b_last + s["m_prev"] - m_new)
        w_s = jnp.exp(b_last - s["b_col"] + gates[b][:, h:h + 1] - m_new)
        kw = s["ks"] * w_s
        c_ref[b, h] = w_c * s["c_prev"] + _dot(kw.T.astype(BF16), s["vb"])
        n_ref[b, h:h + 1, :] = w_c * s["n_prev"] + jnp.sum(kw, axis=0, keepdims=True)
        m_ref[b, h:h + 1, :] = jnp.broadcast_to(m_new, (1, M_HD))
    for (b, h) in units:
        s = st[b, h]
        hs = s["hs"]
        mix_ref[b, :, hs] = _mlstm_head_out(s["hh"], o_ref[b, :, hs], z_ref[b, :, hs], ng_ref[:, hs])


def _mlstm_prompt(pm, norm_g, batch, seq):
    nc = seq // M_CHUNK
    pm3 = pm.reshape(batch, seq, PM_W)

    def col_spec(j, width=M_WIDTH):
        return pl.BlockSpec((batch, M_CHUNK, width), lambda c: (0, c, j))

    state = lambda c: (0, 0, 0)
    mix, c_p, n_p, m_p = pl.pallas_call(
        _mlstm_prompt_kernel,
        grid=(nc,),
        in_specs=[col_spec(0), col_spec(1), col_spec(2), col_spec(3), col_spec(4),
                  pl.BlockSpec((batch, M_CHUNK, LANE), lambda c: (0, c, 5 * M_WIDTH // LANE)),
                  pl.BlockSpec((1, M_WIDTH), lambda c: (0, 0))],
        out_specs=[pl.BlockSpec((batch, M_CHUNK, M_WIDTH), lambda c: (0, c, 0)),
                   pl.BlockSpec((batch, M_HEADS, M_HD, M_HD), lambda c: (0, 0, 0, 0)),
                   pl.BlockSpec((batch, M_HEADS, M_HD), state),
                   pl.BlockSpec((batch, M_HEADS, M_HD), state)],
        out_shape=[jax.ShapeDtypeStruct((batch, seq, M_WIDTH), F32),
                   jax.ShapeDtypeStruct((batch, M_HEADS, M_HD, M_HD), F32),
                   jax.ShapeDtypeStruct((batch, M_HEADS, M_HD), F32),
                   jax.ShapeDtypeStruct((batch, M_HEADS, M_HD), F32)],
        compiler_params=_cparams("arbitrary"),
        name="mlstm_prompt",
    )(pm3, pm3, pm3, pm3, pm3, pm3, norm_g.reshape(1, M_WIDTH))
    return mix.reshape(batch * seq, M_WIDTH), c_p, n_p, m_p


MS_G = 128


def _mlstm_sample_kernel(q_ref, k_ref, v_ref, o_ref, z_ref, ig_ref, fg_ref, m_ref, n_ref, c_ref, ng_ref,
                         mix_ref, co_ref, no_ref, mo_ref):
    q = q_ref[...]
    ks = k_ref[...] * (M_HD ** -0.5)
    v = v_ref[...]
    n_prev = n_ref[...]
    ig = ig_ref[...]
    inter = _log_sigmoid(fg_ref[...]) + m_ref[...]
    m_t = jnp.maximum(inter, ig)
    w_inter = jnp.exp(inter - m_t)
    w_s = jnp.exp(ig - m_t)
    qk = jnp.sum(q * ks, axis=1, keepdims=True) * w_s
    q_t = q.T
    kw_t = (ks * w_s).T
    rows = []
    for r in range(MS_G):
        c_prev = c_ref[r]
        rows.append(jnp.sum(q_t[:, r:r + 1] * c_prev, axis=0, keepdims=True))
        co_ref[r] = w_inter[r:r + 1, :] * c_prev + kw_t[:, r:r + 1] * v[r:r + 1, :]
    q_c = jnp.concatenate(rows, axis=0)
    num = w_inter * q_c + qk * v
    den = w_inter * jnp.sum(q * n_prev, axis=1, keepdims=True) + qk
    hh = num / jnp.maximum(jnp.abs(den), jnp.exp(-m_t))
    no_ref[...] = w_inter * n_prev + w_s * ks
    mo_ref[...] = m_t
    mix_ref[...] = _mlstm_head_out(hh, o_ref[...], z_ref[...], ng_ref[...])


def _mlstm_sample(pm, norm_g, c0, n0, m0):
    nb = pm.shape[0]
    rows = nb * M_HEADS

    def head_rows(j):
        return pm[:, j * M_WIDTH:(j + 1) * M_WIDTH].reshape(rows, M_HD)

    gates = pm[:, 5 * M_WIDTH:5 * M_WIDTH + 2 * M_HEADS]
    ig = gates[:, :M_HEADS].reshape(rows, 1)
    fg = gates[:, M_HEADS:].reshape(rows, 1)
    ng_rows = jnp.tile(norm_g.reshape(M_HEADS, M_HD), (MS_G // M_HEADS, 1))
    vec = pl.BlockSpec((MS_G, M_HD), lambda i: (i, 0))
    one = pl.BlockSpec((MS_G, 1), lambda i: (i, 0))
    mat = pl.BlockSpec((MS_G, M_HD, M_HD), lambda i: (i, 0, 0))
    mix, c1, n1, m1 = pl.pallas_call(
        _mlstm_sample_kernel,
        grid=(rows // MS_G,),
        in_specs=[vec] * 5 + [one] * 3 + [vec, mat, pl.BlockSpec((MS_G, M_HD), lambda i: (0, 0))],
        out_specs=[vec, mat, vec, one],
        out_shape=[jax.ShapeDtypeStruct((rows, M_HD), F32),
                   jax.ShapeDtypeStruct((rows, M_HD, M_HD), F32),
                   jax.ShapeDtypeStruct((rows, M_HD), F32),
                   jax.ShapeDtypeStruct((rows, 1), F32)],
        compiler_params=_cparams("arbitrary"),
        name="mlstm_sample",
    )(*[head_rows(j) for j in range(5)], ig, fg, m0.reshape(rows, 1), n0.reshape(rows, M_HD),
      c0.reshape(rows, M_HD, M_HD), ng_rows)
    return (mix.reshape(nb, M_WIDTH), c1.reshape(nb, M_HEADS, M_HD, M_HD), n1.reshape(nb, M_HEADS, M_HD),
            m1.reshape(nb, M_HEADS))


KVROW_W = 2 * A_HD
HALF_W = 2 * KVROW_W


def _pack_compress(w1, w2):
    def block_diag(k, v):
        z = jnp.zeros_like(k)
        return jnp.concatenate([jnp.concatenate([k, z], axis=-1), jnp.concatenate([z, v], axis=-1)], axis=-2)

    wbd = block_diag(w1[0], w1[1])
    wcat = jnp.concatenate([wbd[:CMP_STRIDE], wbd[CMP_STRIDE:]], axis=-1)
    return wcat.reshape(CMP_STRIDE * KVROW_W, HALF_W).astype(BF16), block_diag(w2[0], w2[1]).astype(BF16)


def _cmp_const_kernel(pe_ref, w_ref, b_ref, o_ref):
    for c in range(2):
        o_ref[c] = _dot(pe_ref[c], w_ref[c], HIGHEST) + b_ref[c]


def _compress_const(pe, w1, b1):
    k = CMP_LEN * A_HD
    pe8 = jnp.broadcast_to(pe.reshape(2, 1, k), (2, SUBLANE, k))
    out = pl.pallas_call(
        _cmp_const_kernel,
        out_shape=jax.ShapeDtypeStruct((2, SUBLANE, A_HD), F32),
        name="compress_const",
    )(pe8, w1.reshape(2, k, A_HD), b1.reshape(2, 1, A_HD))
    return jnp.concatenate([out[0, 0:1], out[1, 0:1]], axis=-1)


def _compress_halves(halves, wcat_ref, const_ref, w2_ref, n_half):
    acc = _dot(halves.astype(BF16), wcat_ref[...])
    pre = acc[:, :KVROW_W] + pltpu.roll(acc[:, KVROW_W:], A_KV * n_half - 1, 0) + const_ref[...]
    return _dot(_gelu_tanh(pre).astype(BF16), w2_ref[...])


def _compress_prompt_kernel(x0_ref, x1_ref, wcat_ref, const_ref, w2_ref, kk_ref, kvt_ref, *, n_half):
    x_refs = (x0_ref, x1_ref)
    halves = jnp.concatenate(
        [jnp.concatenate([x_refs[kv][pl.ds(p, n_half, stride=CMP_STRIDE), :] for kv in range(A_KV)], axis=0)
         for p in range(CMP_STRIDE)], axis=1)
    kc = _compress_halves(halves, wcat_ref, const_ref, w2_ref, n_half)
    kct = kc.T
    for kv in range(A_KV):
        kk_ref[0, kv] = kc[kv * n_half:(kv + 1) * n_half, 0:A_HD]
        kvt_ref[0, kv] = kct[A_HD:, kv * n_half:(kv + 1) * n_half]


def _compress_prompt(rows, wcat, const_row, w2bd, batch, seq):
    n_half = seq // CMP_STRIDE
    return pl.pallas_call(
        functools.partial(_compress_prompt_kernel, n_half=n_half),
        grid=(batch,),
        in_specs=[pl.BlockSpec((seq, KVROW_W), lambda b: (b, 0)),
                  pl.BlockSpec((seq, KVROW_W), lambda b: (b, 1)),
                  pl.BlockSpec((CMP_STRIDE * KVROW_W, HALF_W), lambda b: (0, 0)),
                  pl.BlockSpec((1, KVROW_W), lambda b: (0, 0)),
                  pl.BlockSpec((KVROW_W, KVROW_W), lambda b: (0, 0))],
        out_specs=[pl.BlockSpec((1, A_KV, n_half, A_HD), lambda b: (b, 0, 0, 0)),
                   pl.BlockSpec((1, A_KV, A_HD, n_half), lambda b: (b, 0, 0, 0))],
        out_shape=[jax.ShapeDtypeStruct((batch, A_KV, n_half, A_HD), F32),
                   jax.ShapeDtypeStruct((batch, A_KV, A_HD, n_half), F32)],
        compiler_params=_cparams("arbitrary"),
        name="compress_prompt",
    )(rows, rows, wcat, const_row, w2bd)


CMP_PAT = 16


def _static_ids(p_len):
    i = np.arange(TQ)[None, :]
    c = np.arange(CMP_PAT)[:, None]
    cmp_a = _bucket_np(i + (TQ - (CMP_LEN - 1)) - CMP_STRIDE * c)
    cmp_b = _bucket_np(i - CMP_STRIDE * c - (CMP_LEN - 1))
    r = np.arange(TQ)[:, None]
    slc_diag = _bucket_np(i - r)
    slc_sub = _bucket_np(TQ + i - r)
    slc_far = np.full((TQ, TQ), FAR_BUCKET, np.int32)
    rw = np.arange(2 * WINDOW + TQ)[:, None]
    dw = WINDOW + i - rw
    win = np.where(dw > WINDOW, MASKED_ID, _bucket_np(dw))
    n_half = p_len // CMP_STRIDE
    n = np.arange(n_half)
    cs = _bucket_np(p_len - (CMP_STRIDE * n + CMP_LEN - 1))
    cs[n_half - 1] = MASKED_ID
    cs_rows = -(-n_half // LANE)
    cs_pad = np.full((cs_rows * LANE,), MASKED_ID, np.int32)
    cs_pad[:n_half] = cs
    ws = _bucket_np(WINDOW - np.arange(WINDOW))
    last_page = _bucket_np(PAGE_ROWS - np.arange(PAGE_ROWS))
    parts = [cmp_a, cmp_b, slc_diag, slc_sub, slc_far, win, cs_pad.reshape(cs_rows, LANE),
             ws.reshape(WINDOW // LANE, LANE), last_page.reshape(1, LANE)]
    offs = np.cumsum([0] + [p.shape[0] for p in parts])
    total = -(-int(offs[-1]) // SUBLANE) * SUBLANE
    ids = np.full((total, LANE), MASKED_ID, np.int32)
    ids[:offs[-1]] = np.concatenate(parts, axis=0)
    return ids, [int(o) for o in offs]


def _cover_np(n_cmp_rows, n_cmp, n_slc_rows, n_slc):
    cs = np.arange(n_cmp_rows)[:, None] * CMP_STRIDE
    ss = np.arange(n_slc_rows)[None, :] * SLC_BLOCK
    cov = (cs <= ss + SLC_BLOCK - 1) & (cs + CMP_LEN - 1 >= ss)
    cov &= (np.arange(n_cmp_rows)[:, None] < n_cmp) & (np.arange(n_slc_rows)[None, :] < n_slc)
    return cov.astype(np.float32)


def _softmax_keys_on_rows(s):
    m = jnp.max(s, axis=0, keepdims=True)
    m = jnp.where(m > 0.5 * NEG, m, 0.0)
    e = jnp.exp(s - m)
    tot = jnp.sum(e, axis=0, keepdims=True)
    return e / jnp.where(tot > 0.0, tot, 1.0)


def _softmax_keys_on_lanes(s, s_new=None):
    m = jnp.max(s, axis=1, keepdims=True)
    if s_new is not None:
        m = jnp.maximum(m, s_new)
    m = jnp.where(m > 0.5 * NEG, m, 0.0)
    e = jnp.exp(s - m)
    tot = jnp.sum(e, axis=1, keepdims=True)
    e_new = None
    if s_new is not None:
        e_new = jnp.exp(s_new - m)
        tot = tot + e_new
    return e, e_new, 1.0 / jnp.where(tot > 0.0, tot, 1.0)


def _cmp_attend_kernel(far_ref, q_ref, kk_ref, kvt_ref, pt_ref, cov_ref, o_ref, sel_ref, bscr, *, nc, ns):
    k = pl.program_id(1)
    start = pl.multiple_of(jnp.maximum(SUBLANE * k - SUBLANE, 0), SUBLANE)
    variant = jnp.where(k == 0, 1, 0)
    row = lax.broadcasted_iota(jnp.int32, (nc, GROUP_LANES), 0)
    t = k * TQ + lax.broadcasted_iota(jnp.int32, (ns, TQ), 1)
    blk = lax.broadcasted_iota(jnp.int32, (ns, TQ), 0)
    cur = t // SLC_BLOCK
    valid = blk * SLC_BLOCK <= t
    forced = (blk == 0) | (blk == cur) | (blk == cur - 1)
    kvs = range(A_KV)
    for kv in kvs:
        bscr[kv] = jnp.where(row < start, far_ref[kv], NEG)
        bscr[kv, pl.ds(start, CMP_PAT), :] = pt_ref[variant, kv]
    s = [_nt(kk_ref[0, kv].astype(BF16), _stacked_queries(q_ref, kv)) + bscr[kv] for kv in kvs]
    p = [_softmax_keys_on_rows(s[kv]) for kv in kvs]
    o = [_dot(kvt_ref[0, kv].astype(BF16), p[kv].astype(BF16)) for kv in kvs]
    for kv in kvs:
        for g in range(A_GROUP):
            h = kv * A_GROUP + g
            o_ref[0, h * A_HD:(h + 1) * A_HD, :] = o[kv][:, g * TQ:(g + 1) * TQ]
    imp = [sum(p[kv][:, g * TQ:(g + 1) * TQ] for g in range(A_GROUP)) for kv in kvs]
    sc = [jnp.where(forced, jnp.inf, jnp.where(valid, _dot(cov_ref[...], imp[kv], HIGHEST), -jnp.inf)) for kv in kvs]
    cnt = [jnp.zeros((ns, TQ), jnp.int32) for _ in kvs]
    for j in range(ns):
        for kv in kvs:
            r = sc[kv][j:j + 1, :]
            before = (r > sc[kv]) | ((r == sc[kv]) & (blk > j))
            cnt[kv] = cnt[kv] + before.astype(jnp.int32)
    for kv in kvs:
        sel_ref[0, kv] = jnp.where(cnt[kv] < N_SEL, 0.0, NEG)


def _cmp_attend(far, pa, kk, kvt, pat, cov_t, batch, seq):
    nq = seq // TQ
    nc = seq // CMP_STRIDE
    ns = seq // SLC_BLOCK
    return pl.pallas_call(
        functools.partial(_cmp_attend_kernel, nc=nc, ns=ns),
        grid=(batch, nq),
        in_specs=[pl.BlockSpec((A_KV, 1, GROUP_LANES), lambda b, k: (0, 0, 0)),
                  pl.BlockSpec((TQ, A_WIDTH), lambda b, k: (b * nq + k, 0)),
                  pl.BlockSpec((1, A_KV, nc, A_HD), lambda b, k: (b, 0, 0, 0)),
                  pl.BlockSpec((1, A_KV, A_HD, nc), lambda b, k: (b, 0, 0, 0)),
                  pl.BlockSpec((2, A_KV, CMP_PAT, GROUP_LANES), lambda b, k: (0, 0, 0, 0)),
                  pl.BlockSpec((ns, nc), lambda b, k: (0, 0))],
        out_specs=[pl.BlockSpec((1, A_WIDTH, TQ), lambda b, k: (b, 0, k)),
                   pl.BlockSpec((1, A_KV, ns, TQ), lambda b, k: (b, 0, 0, k))],
        out_shape=[jax.ShapeDtypeStruct((batch, A_WIDTH, seq), F32),
                   jax.ShapeDtypeStruct((batch, A_KV, ns, seq), F32)],
        scratch_shapes=[pltpu.VMEM((A_KV, nc, GROUP_LANES), F32)],
        compiler_params=_cparams("arbitrary", "arbitrary"),
        name="cmp_attend",
    )(far, pa, kk, kvt, pat, cov_t)


SLC_CK = 4 * TQ
GROUP_LANES = A_GROUP * TQ
SLC_CLASSES = 4


def _stacked_queries(q_ref, kv, scale=ATT_SCALE):
    heads = [q_ref[:, (kv * A_GROUP + g) * A_HD:(kv * A_GROUP + g + 1) * A_HD] for g in range(A_GROUP)]
    return (jnp.concatenate(heads, axis=0) * scale).astype(BF16)


def _slc_attend_kernel(q_ref, k_ref, vt_ref, sel_ref, tab_ref, o_ref):
    k = pl.program_id(1)
    sub = SLC_CK // TQ
    q4 = [_stacked_queries(q_ref, kv, ATT_SCALE * LOG2E) for kv in range(A_KV)]
    upper = lax.broadcasted_iota(jnp.int32, (TQ, GROUP_LANES), 0) < SLC_BLOCK

    def body(j, carry):
        j0 = pl.multiple_of(j * SLC_CK, SLC_CK)
        kvs = range(A_KV)
        s_all = [_nt(k_ref[pl.ds(j0, SLC_CK), kv * LANE:kv * LANE + A_HD].astype(BF16), q4[kv]) for kv in kvs]
        vt = [vt_ref[0, kv * LANE + A_HD:(kv + 1) * LANE, pl.ds(j0, SLC_CK)].astype(BF16) for kv in kvs]
        s = []
        for kv in kvs:
            parts = []
            for u in range(sub):
                jj = j * sub + u
                sel0 = jnp.concatenate([sel_ref[0, kv, pl.ds(2 * jj, 1), :]] * A_GROUP, axis=1)
                sel1 = jnp.concatenate([sel_ref[0, kv, pl.ds(2 * jj + 1, 1), :]] * A_GROUP, axis=1)
                cls = jnp.where(jj > k, SLC_CLASSES - 1, jnp.minimum(k - jj, 2))
                parts.append(s_all[kv][u * TQ:(u + 1) * TQ] + tab_ref[cls, kv] + jnp.where(upper, sel0, sel1))
            s.append(jnp.concatenate(parts, axis=0))
        m_new = [jnp.maximum(carry[kv][0], jnp.max(s[kv], axis=0, keepdims=True)) for kv in kvs]
        p = [jnp.exp2(s[kv] - m_new[kv]) for kv in kvs]
        pv = [_dot(vt[kv], p[kv].astype(BF16)) for kv in kvs]
        out = []
        for kv in kvs:
            m_run, l_run, acc = carry[kv]
            alpha = jnp.exp2(m_run - m_new[kv])
            l_new = alpha * l_run + jnp.sum(p[kv], axis=0, keepdims=True)
            out.append((m_new[kv], l_new, alpha * acc + pv[kv]))
        return tuple(out)

    init = tuple((jnp.full((1, GROUP_LANES), NEG, F32), jnp.zeros((1, GROUP_LANES), F32),
                  jnp.zeros((A_HD, GROUP_LANES), F32)) for _ in range(A_KV))
    res = lax.fori_loop(0, (k + sub) // sub, body, init)
    for kv in range(A_KV):
        _, l_run, acc = res[kv]
        o = acc / l_run
        for g in range(A_GROUP):
            h = kv * A_GROUP + g
            o_ref[0, h * A_HD:(h + 1) * A_HD, :] = o[:, g * TQ:(g + 1) * TQ]


def _slc_attend(pa, rows, rows_t, sel, tab, batch, seq):
    nq = seq // TQ
    ns = seq // SLC_BLOCK
    return pl.pallas_call(
        _slc_attend_kernel,
        grid=(batch, nq),
        in_specs=[pl.BlockSpec((TQ, A_WIDTH), lambda b, k: (b * nq + k, 0)),
                  pl.BlockSpec((seq, ROW_W), lambda b, k: (b, 0)),
                  pl.BlockSpec((1, ROW_W, seq), lambda b, k: (b, 0, 0)),
                  pl.BlockSpec((1, A_KV, ns, TQ), lambda b, k: (b, 0, 0, k)),
                  pl.BlockSpec((SLC_CLASSES, A_KV, TQ, GROUP_LANES), lambda b, k: (0, 0, 0, 0))],
        out_specs=pl.BlockSpec((1, A_WIDTH, TQ), lambda b, k: (b, 0, k)),
        out_shape=jax.ShapeDtypeStruct((batch, A_WIDTH, seq), F32),
        compiler_params=_cparams("arbitrary", "arbitrary"),
        name="slc_attend",
    )(pa, rows, rows_t, sel, tab)


WIN_SPAN = WINDOW + TQ


def _win_attend_kernel(q_ref, k_ref, vt_ref, bias_ref, o_ref):
    k = pl.program_id(1)
    key0 = pl.multiple_of(jnp.maximum(k * TQ - WINDOW, 0), TQ)
    row0 = pl.multiple_of(jnp.maximum(WINDOW - k * TQ, 0), TQ)
    kvs = range(A_KV)
    s = [_nt(k_ref[0, pl.ds(key0, WIN_SPAN), kv * LANE:kv * LANE + A_HD].astype(BF16), _stacked_queries(q_ref, kv))
         + bias_ref[kv, pl.ds(row0, WIN_SPAN), :] for kv in kvs]
    p = [_softmax_keys_on_rows(s[kv]) for kv in kvs]
    o = [_dot(vt_ref[0, kv * LANE + A_HD:(kv + 1) * LANE, pl.ds(key0, WIN_SPAN)].astype(BF16), p[kv].astype(BF16))
         for kv in kvs]
    for kv in kvs:
        for g in range(A_GROUP):
            h = kv * A_GROUP + g
            o_ref[0, h * A_HD:(h + 1) * A_HD, :] = o[kv][:, g * TQ:(g + 1) * TQ]


def _win_attend(pa, rows, rows_t, bias_w, batch, seq):
    nq = seq // TQ
    assert seq >= WIN_SPAN
    return pl.pallas_call(
        _win_attend_kernel,
        grid=(batch, nq),
        in_specs=[pl.BlockSpec((TQ, A_WIDTH), lambda b, k: (b * nq + k, 0)),
                  pl.BlockSpec((1, seq, ROW_W), lambda b, k: (b, 0, 0)),
                  pl.BlockSpec((1, ROW_W, seq), lambda b, k: (b, 0, 0)),
                  pl.BlockSpec((A_KV, WIN_SPAN + WINDOW, GROUP_LANES), lambda b, k: (0, 0, 0))],
        out_specs=pl.BlockSpec((1, A_WIDTH, TQ), lambda b, k: (b, 0, k)),
        out_shape=jax.ShapeDtypeStruct((batch, A_WIDTH, seq), F32),
        compiler_params=_cparams("arbitrary", "arbitrary"),
        name="win_attend",
    )(pa, rows, rows_t, bias_w)


def _out_tail(x, mix_m, mix_a, gate, w_ref, b_ref, g_ref, beta_ref):
    y = (_dot(mix_m.astype(BF16), w_ref[:M_WIDTH]) + _dot(mix_a.astype(BF16), w_ref[M_WIDTH:]) + b_ref[...])
    return _ln_rows(DEEPNORM_ALPHA * x + gate * y) * g_ref[...] + beta_ref[...]


OUT_TM = 512


def _out_prompt_kernel(x_ref, mm_ref, oc_ref, os_ref, ow_ref, ga_ref, za_ref, gate_ref,
                       w_ref, b_ref, g_ref, beta_ref, y_ref):
    sig = jax.nn.sigmoid(ga_ref[...].T)
    parts = []
    for h in range(A_HEADS):
        hs = slice(h * A_HD, (h + 1) * A_HD)
        parts.append(sig[h:h + 1] * oc_ref[0, hs, :] + sig[A_HEADS + h:A_HEADS + h + 1] * os_ref[0, hs, :]
                     + sig[2 * A_HEADS + h:2 * A_HEADS + h + 1] * ow_ref[0, hs, :])
    ha = jnp.concatenate(parts, axis=0).T
    mix_a = ha * _silu(za_ref[...])
    y_ref[...] = _out_tail(x_ref[...], mm_ref[...], mix_a, gate_ref[...], w_ref, b_ref, g_ref, beta_ref)


def _out_prompt(x, mix_m, o_c, o_s, o_w, pa, gate, w_out, b_out, ln_g, ln_b, batch, seq):
    tm = OUT_TM
    nq = seq // tm
    rows = batch * seq
    branch = pl.BlockSpec((1, A_WIDTH, tm), lambda i: (i // nq, 0, i % nq))
    vec = pl.BlockSpec((1, D_MODEL), lambda i: (0, 0))
    return pl.pallas_call(
        _out_prompt_kernel,
        grid=(rows // tm,),
        in_specs=[pl.BlockSpec((tm, D_MODEL), lambda i: (i, 0)),
                  pl.BlockSpec((tm, M_WIDTH), lambda i: (i, 0)),
                  branch, branch, branch,
                  pl.BlockSpec((tm, LANE), lambda i: (i, 2 * A_WIDTH // LANE)),
                  pl.BlockSpec((tm, A_WIDTH), lambda i: (i, 1)),
                  pl.BlockSpec((None, 1, D_MODEL), lambda i: (i // nq, 0, 0)),
                  pl.BlockSpec((D_MODEL, D_MODEL), lambda i: (0, 0)),
                  vec, vec, vec],
        out_specs=pl.BlockSpec((tm, D_MODEL), lambda i: (i, 0)),
        out_shape=jax.ShapeDtypeStruct((rows, D_MODEL), F32),
        compiler_params=_cparams("arbitrary"),
        name="out_prompt",
    )(x, mix_m, o_c, o_s, o_w, pa, pa, gate, w_out, b_out, ln_g, ln_b)


def _out_sample_kernel(x_ref, mm_ref, ha_ref, za_ref, gate_ref, w_ref, b_ref, g_ref, beta_ref, y_ref):
    mix_a = ha_ref[...] * _silu(za_ref[...])
    y_ref[...] = _out_tail(x_ref[...], mm_ref[...], mix_a, gate_ref[...], w_ref, b_ref, g_ref, beta_ref)


def _out_sample(x, mix_m, ha, pa, gate, w_out, b_out, ln_g, ln_b):
    rows = x.shape[0]
    vec = pl.BlockSpec((1, D_MODEL), lambda i: (0, 0))
    return pl.pallas_call(
        _out_sample_kernel,
        grid=(1,),
        in_specs=[pl.BlockSpec((rows, D_MODEL), lambda i: (0, 0)),
                  pl.BlockSpec((rows, M_WIDTH), lambda i: (0, 0)),
                  pl.BlockSpec((rows, A_WIDTH), lambda i: (0, 0)),
                  pl.BlockSpec((rows, A_WIDTH), lambda i: (0, 1)),
                  pl.BlockSpec((rows, D_MODEL), lambda i: (0, 0)),
                  pl.BlockSpec((D_MODEL, D_MODEL), lambda i: (0, 0)),
                  vec, vec, vec],
        out_specs=pl.BlockSpec((rows, D_MODEL), lambda i: (0, 0)),
        out_shape=jax.ShapeDtypeStruct((rows, D_MODEL), F32),
        compiler_params=_cparams("arbitrary"),
        name="out_sample",
    )(x, mix_m, ha, pa, gate, w_out, b_out, ln_g, ln_b)


def _nsa_prompt(rel_bias, pa, rc, rs, rw, st, wt, bias, offs, cmp_w, const_row, batch, seq):
    wcat, w2bd = cmp_w
    nc = seq // CMP_STRIDE
    ns = seq // SLC_BLOCK
    kk, kvt = _compress_prompt(rc, wcat, const_row, w2bd, batch, seq)

    def group_lanes(tiles):
        rows = tiles.shape[1]
        return tiles.reshape(A_KV, A_GROUP, rows, TQ).transpose(0, 2, 1, 3).reshape(A_KV, rows, GROUP_LANES)

    pat = group_lanes(bias[:, offs[0]:offs[2]]).reshape(A_KV, 2, CMP_PAT, GROUP_LANES).transpose(1, 0, 2, 3)
    far = jnp.repeat(rel_bias[FAR_BUCKET].reshape(A_KV, A_GROUP), TQ, axis=1).reshape(A_KV, 1, GROUP_LANES)
    cov_t = jnp.asarray(_cover_np(nc, nc - 1, ns, ns).T)
    o_c, sel = _cmp_attend(far, pa, kk, kvt, pat, cov_t, batch, seq)

    tab = group_lanes(bias[:, offs[2]:offs[5]]).reshape(A_KV, 3, TQ, GROUP_LANES).transpose(1, 0, 2, 3)
    tab = jnp.concatenate([tab * LOG2E, jnp.full((1,) + tab.shape[1:], NEG, F32)], axis=0)
    o_s = _slc_attend(pa, rs, st, sel, tab, batch, seq)
    o_w = _win_attend(pa, rw.reshape(batch, seq, ROW_W), wt, group_lanes(bias[:, offs[5]:offs[6]]), batch, seq)
    return o_c, o_s, o_w, sel


HALVES_PER_PAGE = 8
SEL_LANES = 256


def _gather_pages(copy_of, n_copies):
    b = pl.program_id(0)
    slot = b % 2

    @pl.when(b == 0)
    def _():
        for i in range(n_copies):
            copy_of(0, i, 0).start()

    @pl.when(b + 1 < pl.num_programs(0))
    def _():
        for i in range(n_copies):
            copy_of(b + 1, i, 1 - slot).start()

    for i in range(n_copies):
        copy_of(b, i, slot).wait()
    return slot


def _cmp_sample_kernel(pt_ref, cache_ref, q_ref, perm_ref, wcat_ref, const_ref, w2_ref, bias_ref, cov_ref,
                       o_ref, idx_ref, buf, sem, *, n_pages, p_len):
    def page_copy(seq, j, slot):
        return pltpu.make_async_copy(cache_ref.at[pt_ref[seq * n_pages + j]], buf.at[slot, j], sem.at[slot])

    slot = _gather_pages(page_copy, n_pages)
    pages = [buf.at[slot, j] for j in range(n_pages)]
    n_half = n_pages * HALVES_PER_PAGE
    perm = perm_ref[...]
    group = 8

    def move_lanes(g):
        tiles = [pages[j][kv].reshape(KVROW_W, PAGE_ROWS).astype(BF16)
                 for j in range(g * group, (g + 1) * group) for kv in range(A_KV)]
        return _dot(jnp.concatenate(tiles, axis=0), perm)

    blocks = [[], []]
    n_groups = n_pages // group
    moved = move_lanes(0)
    for g in range(n_groups):
        moved_next = move_lanes(g + 1) if g + 1 < n_groups else None
        for i in range(group * A_KV):
            kv = i % A_KV
            rows = moved[i * KVROW_W:(i + 1) * KVROW_W].T
            blocks[kv].append(jnp.concatenate(
                [rows[p * HALVES_PER_PAGE:(p + 1) * HALVES_PER_PAGE] for p in range(CMP_STRIDE)], axis=1))
        moved = moved_next
    halves = jnp.concatenate(blocks[0] + blocks[1], axis=0)
    kc = _compress_halves(halves, wcat_ref, const_ref, w2_ref, n_half)
    kc0, kc1 = kc[:n_half], kc[n_half:]
    q8 = q_ref[0]
    first = lax.broadcasted_iota(jnp.int32, (A_HEADS, n_half), 0) < A_GROUP

    def logits(keys):
        return lax.dot_general(q8, keys, (((1,), (1,)), ((), ())), preferred_element_type=F32, precision=HIGHEST)

    s = jnp.where(first, logits(kc0[:, :A_HD]), logits(kc1[:, :A_HD])) * ATT_SCALE + bias_ref[...]
    e, _, inv = _softmax_keys_on_lanes(s)
    p = e * inv
    pb = p.astype(BF16)
    first_o = lax.broadcasted_iota(jnp.int32, (A_HEADS, A_HD), 0) < A_GROUP
    o_ref[0] = jnp.where(first_o, _dot(pb, kc0[:, A_HD:].astype(BF16)), _dot(pb, kc1[:, A_HD:].astype(BF16)))
    hrow = lax.broadcasted_iota(jnp.int32, (A_HEADS, n_half), 0)
    imp0 = jnp.sum(jnp.where(first, p, 0.0), axis=0, keepdims=True)
    imp1 = jnp.sum(jnp.where(first, 0.0, p), axis=0, keepdims=True)
    imp = jnp.where(hrow == 0, imp0, jnp.where(hrow == 1, imp1, 0.0))
    score = _dot(imp, cov_ref[...], HIGHEST)
    n_slc = p_len // SLC_BLOCK + 1
    cur = p_len // SLC_BLOCK
    lane = lax.broadcasted_iota(jnp.int32, (A_HEADS, SEL_LANES), 1)
    forced = (lane == 0) | (lane == cur) | (lane == cur - 1)
    valid = lane * SLC_BLOCK <= p_len
    sc = jnp.where(forced, jnp.inf, jnp.where(valid, score, -jnp.inf))
    k_sel = float(min(N_SEL, n_slc))
    sub = lax.broadcasted_iota(jnp.int32, (SEL_LANES, SEL_LANES), 0)
    lan = lax.broadcasted_iota(jnp.int32, (SEL_LANES, SEL_LANES), 1)
    slot_l = lax.broadcasted_iota(jnp.int32, (SEL_LANES, LANE), 1).astype(F32)
    blk_s = lax.broadcasted_iota(jnp.int32, (SEL_LANES, LANE), 0).astype(F32)
    out_row = lax.broadcasted_iota(jnp.int32, (A_HEADS, LANE), 0)
    picks = jnp.zeros((A_HEADS, LANE), F32)
    for kv in range(A_KV):
        row = sc[kv:kv + 1, :]
        col = jnp.sum(jnp.where(sub == lan, row, 0.0), axis=1, keepdims=True)
        before_c = (lan < n_slc) & ((row > col) | ((row == col) & (lan < sub)))
        sel_c = (jnp.sum(before_c.astype(F32), axis=1, keepdims=True) < k_sel) & (sub[:, :1] < n_slc)
        before_r = (sub < n_slc) & ((col > row) | ((col == row) & (sub < lan)))
        sel_r = (jnp.sum(before_r.astype(F32), axis=0, keepdims=True) < k_sel) & (lan[:1] < n_slc)
        slot_c = jnp.sum(jnp.where((lan < sub) & sel_r, 1.0, 0.0), axis=1, keepdims=True)
        hit = sel_c & (slot_c == slot_l)
        picks_kv = jnp.sum(jnp.where(hit, blk_s, 0.0), axis=0, keepdims=True)
        picks = jnp.where(out_row == kv, picks_kv, picks)
    idx_ref[0] = picks.astype(jnp.int32)


def _cmp_sample(page_table, cache_pages, q3, wcat, const_row, w2bd, bias_cs, cov, p_len):
    nb, n_pages = page_table.shape
    n_half = n_pages * HALVES_PER_PAGE

    r = np.arange(PAGE_ROWS)
    perm = np.zeros((PAGE_ROWS, PAGE_ROWS), np.float32)
    perm[r, (r % CMP_STRIDE) * HALVES_PER_PAGE + r // CMP_STRIDE] = 1.0
    const2 = lambda b, pt: (0, 0)
    grid_spec = pltpu.PrefetchScalarGridSpec(
        num_scalar_prefetch=1,
        grid=(nb,),
        in_specs=[pl.BlockSpec(memory_space=pl.ANY),
                  pl.BlockSpec((1, A_HEADS, A_HD), lambda b, pt: (b, 0, 0)),
                  pl.BlockSpec((PAGE_ROWS, PAGE_ROWS), const2),
                  pl.BlockSpec((CMP_STRIDE * KVROW_W, HALF_W), const2),
                  pl.BlockSpec((1, KVROW_W), const2),
                  pl.BlockSpec((KVROW_W, KVROW_W), const2),
                  pl.BlockSpec((A_HEADS, n_half), const2),
                  pl.BlockSpec((n_half, SEL_LANES), const2)],
        out_specs=[pl.BlockSpec((1, A_HEADS, A_HD), lambda b, pt: (b, 0, 0)),
                   pl.BlockSpec((1, A_HEADS, LANE), lambda b, pt: (b, 0, 0))],
        scratch_shapes=[pltpu.VMEM((2, n_pages) + cache_pages.shape[1:], F32),
                        pltpu.SemaphoreType.DMA((2,))],
    )
    return pl.pallas_call(
        functools.partial(_cmp_sample_kernel, n_pages=n_pages, p_len=p_len),
        grid_spec=grid_spec,
        out_shape=[jax.ShapeDtypeStruct((nb, A_HEADS, A_HD), F32),
                   jax.ShapeDtypeStruct((nb, A_HEADS, LANE), jnp.int32)],
        compiler_params=_cparams("arbitrary"),
        name="cmp_sample",
    )(page_table.reshape(-1), cache_pages, q3, jnp.asarray(perm, BF16), wcat, const_row, w2bd, bias_cs, cov)


PAGE_ROWS = 128
BLOCKS_PER_PAGE = PAGE_ROWS // SLC_BLOCK


def _slc_sample_kernel(idx_ref, pt_ref, cache_ref, q_ref, snew_ref, win_ref, wnew_ref, wcol_ref, oc_ref, g_ref,
                       rbt_ref, bw_ref, near_ref, ha_ref, wbuf_ref, buf, sem, *, n_pages, p_len):
    n_blk = A_KV * N_SEL
    past_blocks = p_len // SLC_BLOCK

    def block_copy(seq, i, slot):
        kv = i // N_SEL
        blk = jnp.minimum(idx_ref[seq * n_blk + i], past_blocks - 1)
        page = pt_ref[seq * n_pages + blk // BLOCKS_PER_PAGE]
        return pltpu.make_async_copy(cache_ref.at[page, kv], buf.at[slot, i], sem.at[slot])

    slot = _gather_pages(block_copy, n_blk)
    blocks = [buf.at[slot, i] for i in range(n_blk)]
    b = pl.program_id(0)
    n_keys = N_SEL * PAGE_ROWS
    n_buf = win_ref.shape[-1]
    qf = q_ref[0]
    q8 = qf.astype(BF16)
    first_o = lax.broadcasted_iota(jnp.int32, (A_HEADS, A_HD), 0) < A_GROUP
    lane = lax.broadcasted_iota(jnp.int32, (1, n_keys), 1)
    slot = lane // PAGE_ROWS
    in_page = lane % PAGE_ROWS
    near = jnp.concatenate([near_ref[...]] * N_SEL, axis=1)
    far = rbt_ref[:, FAR_BUCKET:FAR_BUCKET + 1]
    bias_new = rbt_ref[:, 0:1]

    def new_key_logit(row_ref, kv):
        k_new = row_ref[0, :, kv * LANE:kv * LANE + A_HD]
        v_new = row_ref[0, :, kv * LANE + A_HD:(kv + 1) * LANE]
        return jnp.sum(qf * k_new, axis=1, keepdims=True) * ATT_SCALE + bias_new, v_new

    o_s, o_w = [], []
    for kv in range(A_KV):
        blk_of = jnp.zeros((1, n_keys), jnp.int32)
        has_new = False
        for j in range(N_SEL):
            blk = idx_ref[(b * A_KV + kv) * N_SEL + j]
            blk_of = jnp.where(slot == j, blk, blk_of)
            has_new = jnp.logical_or(has_new, blk == past_blocks)
        page_of = jnp.minimum(blk_of, past_blocks - 1) // BLOCKS_PER_PAGE
        pos = page_of * PAGE_ROWS + in_page
        valid = (pos // SLC_BLOCK == blk_of) & (pos < p_len)
        bias = jnp.where(page_of == n_pages - 1, near, far)
        kt = jnp.concatenate([blocks[kv * N_SEL + j][0] for j in range(N_SEL)], axis=1).astype(BF16)
        vt = jnp.concatenate([blocks[kv * N_SEL + j][1] for j in range(N_SEL)], axis=1).astype(BF16)
        s = jnp.where(valid, _dot(q8, kt) * ATT_SCALE + bias, NEG)
        s_new, v_new = new_key_logit(snew_ref, kv)
        s_new = jnp.where(has_new, s_new, NEG)
        e, e_new, inv = _softmax_keys_on_lanes(s, s_new)
        o_s.append((_nt(e.astype(BF16), vt) + e_new * v_new) * inv)
        sw = _dot(q8, win_ref[0, kv, 0].astype(BF16)) * ATT_SCALE + bw_ref[...]
        sw_new, vw_new = new_key_logit(wnew_ref, kv)
        e, e_new, inv = _softmax_keys_on_lanes(sw, sw_new)
        o_w.append((_nt(e.astype(BF16), win_ref[0, kv, 1].astype(BF16)) + e_new * vw_new) * inv)
    g = jax.nn.sigmoid(g_ref[0])
    ha_ref[0] = (g[0] * oc_ref[0] + g[1] * jnp.where(first_o, o_s[0], o_s[1])
                 + g[2] * jnp.where(first_o, o_w[0], o_w[1]))
    last = lax.broadcasted_iota(jnp.int32, (A_HD, n_buf), 1) == n_buf - 1
    for kv in range(A_KV):
        for c in range(2):
            r0 = (kv * 2 + c) * A_HD
            wbuf_ref[0, kv, c] = jnp.where(last, wcol_ref[0, r0:r0 + A_HD, :],
                                           pltpu.roll(win_ref[0, kv, c], n_buf - 1, 1))


def _slc_sample(idx, page_table, cache_t, q3, slc_new, win_t, win_new, win_new_col, o_c, gates, rb_t, bias_ws,
                bias_near, p_len):
    nb, n_pages = page_table.shape
    n_buf = win_t.shape[-1]
    per_seq3 = lambda b, i, p: (b, 0, 0)
    win_spec = pl.BlockSpec((1, A_KV, 2, A_HD, n_buf), lambda b, i, p: (b, 0, 0, 0, 0))
    grid_spec = pltpu.PrefetchScalarGridSpec(
        num_scalar_prefetch=2,
        grid=(nb,),
        in_specs=[pl.BlockSpec(memory_space=pl.ANY),
                  pl.BlockSpec((1, A_HEADS, A_HD), per_seq3),
                  pl.BlockSpec((1, 1, ROW_W), per_seq3),
                  win_spec,
                  pl.BlockSpec((1, 1, ROW_W), per_seq3),
                  pl.BlockSpec((1, ROW_W, 1), per_seq3),
                  pl.BlockSpec((1, A_HEADS, A_HD), per_seq3),
                  pl.BlockSpec((1, 3, A_HEADS, 1), lambda b, i, p: (b, 0, 0, 0)),
                  pl.BlockSpec((A_HEADS, N_BUCKETS), lambda b, i, p: (0, 0)),
                  pl.BlockSpec((A_HEADS, n_buf), lambda b, i, p: (0, 0)),
                  pl.BlockSpec((A_HEADS, PAGE_ROWS), lambda b, i, p: (0, 0))],
        out_specs=[pl.BlockSpec((1, A_HEADS, A_HD), per_seq3), win_spec],
        scratch_shapes=[pltpu.VMEM((2, A_KV * N_SEL) + cache_t.shape[2:], F32),
                        pltpu.SemaphoreType.DMA((2,))],
    )
    return pl.pallas_call(
        functools.partial(_slc_sample_kernel, n_pages=n_pages, p_len=p_len),
        grid_spec=grid_spec,
        out_shape=[jax.ShapeDtypeStruct((nb, A_HEADS, A_HD), F32),
                   jax.ShapeDtypeStruct(win_t.shape, F32)],
        compiler_params=_cparams("arbitrary"),
        name="slc_win_sample",
    )(idx.reshape(-1), page_table.reshape(-1), cache_t, q3, slc_new, win_t, win_new, win_new_col, o_c, gates,
      rb_t, bias_ws, bias_near)


def _rows_last(a):
    n = a.ndim
    return a.transpose(*range(n - 4), n - 3, n - 2, n - 1, n - 4)


def _rows_first(a):
    n = a.ndim
    return a.transpose(*range(n - 4), n - 1, n - 4, n - 3, n - 2)


def _nsa_sample(rel_bias, pa, rs_new, rw_new, cache_cmp, cache_slc, win_cache, page_table,
                bias, offs, cmp_w, const_row, p_len):
    wcat, w2bd = cmp_w
    nb, n_pages = page_table.shape
    n_half = p_len // CMP_STRIDE
    n_slc = p_len // SLC_BLOCK + 1
    n_buf = win_cache.shape[1]
    q3 = pa[:, :A_WIDTH].reshape(nb, A_HEADS, A_HD)
    gates = pa[:, 2 * A_WIDTH:2 * A_WIDTH + 3 * A_HEADS].reshape(nb, 3, A_HEADS, 1)
    bias_cs = bias[:, offs[6]:offs[7]].reshape(A_HEADS, -1)[:, :n_half]
    bias_ws = bias[:, offs[7]:offs[8]].reshape(A_HEADS, -1)[:, :n_buf]
    cov = jnp.asarray(_cover_np(n_half, n_half - 1, SEL_LANES, n_slc))
    o_c, picks = _cmp_sample(page_table, _rows_last(cache_cmp), q3, wcat, const_row, w2bd, bias_cs, cov, p_len)
    idx = picks[:, :A_KV, :N_SEL]
    ha, wbuf = _slc_sample(idx, page_table, _rows_last(cache_slc), q3, rs_new.reshape(nb, 1, ROW_W),
                           _rows_last(win_cache), rw_new.reshape(nb, 1, ROW_W), rw_new.reshape(nb, ROW_W, 1),
                           o_c, gates, rel_bias.T, bias_ws, bias[:, offs[8]], p_len)
    return ha.reshape(nb, A_WIDTH), idx, wbuf


def kernel(x_prompt, x_sample, cache_cmp_kv, cache_slc_kv, cache_win_kv, state_mlstm_C, state_mlstm_n, state_mlstm_m, page_table, c_prompt, c_sample, rel_bias, w_ada, b_ada, w_in, b_in, m_norm_g, cmp_pe, cmp_w1, cmp_b1, cmp_w2, w_out, b_out, ln_g, ln_b):
    B, T, _ = x_prompt.shape
    NB = x_sample.shape[0]
    n_pages = page_table.shape[1]
    p_len = n_pages * PAGE_ROWS
    depth = w_in.shape[0]
    assert depth == 1 and x_sample.shape[1] == 1 and cache_win_kv.shape[2] == WINDOW
    ids, offs = _static_ids(p_len)
    bias = _bias_tables(rel_bias, ids)
    x_p = x_prompt.reshape(B * T, D_MODEL)
    x_s = x_sample.reshape(NB, D_MODEL)
    l = 0
    n_mod = -(-(B + NB) // SUBLANE) * SUBLANE
    c_all = jnp.concatenate([c_prompt, c_sample, jnp.zeros((n_mod - B - NB, D_MODEL), F32)])
    shift, scale, gate = jnp.split(_adaln_mod(c_all, w_ada[l], b_ada[l]), 3, axis=-1)
    packed = _pack_in_proj(w_in[l], b_in[l], BF16)
    packed_f32 = _pack_in_proj(w_in[l], b_in[l], F32)
    cmp_w = _pack_compress(cmp_w1[l], cmp_w2[l])
    const_row = _compress_const(cmp_pe[l], cmp_w1[l], cmp_b1[l])
    w_out_b = w_out[l].astype(BF16)
    vecs = (b_out[l].reshape(1, -1), ln_g[l].reshape(1, -1), ln_b[l].reshape(1, -1))
    pm, pa, rc, rs, rw, ct, st, wt = _project(x_p, shift[:B, None], scale[:B, None], packed, B, 256)
    mix_m, c_p, n_p, m_p = _mlstm_prompt(pm, m_norm_g[l], B, T)
    o_c, o_s, o_w, _ = _nsa_prompt(rel_bias, pa, rc, rs, rw, st, wt, bias, offs, cmp_w, const_row, B, T)
    y_p = _out_prompt(x_p, mix_m, o_c, o_s, o_w, pa, gate[:B, None], w_out_b, *vecs, B, T)
    pm_s, pa_s, _, rs_s, rw_s, ct_s, st_s, wt_s = _project(x_s, shift[B:B + NB], scale[B:B + NB], packed_f32, 1, NB)
    mix_s, c_s, n_s, m_s = _mlstm_sample(pm_s, m_norm_g[l], state_mlstm_C[l], state_mlstm_n[l], state_mlstm_m[l])
    ha_s, _, wbuf_s = _nsa_sample(rel_bias, pa_s, rs_s, rw_s, cache_cmp_kv[l], cache_slc_kv[l],
                                  cache_win_kv[l], page_table, bias, offs, cmp_w, const_row, p_len)
    y_s = _out_sample(x_s, mix_s.reshape(NB, M_WIDTH), ha_s, pa_s, gate[B:B + NB], w_out_b, *vecs)

    def kv_prompt(a):
        return _rows_first(a.reshape(1, B, A_KV, 2, A_HD, a.shape[-1]))

    def kv_sample(a):
        return a.reshape(1, 1, A_KV, 2, A_HD, NB).transpose(0, 5, 1, 2, 3, 4)

    return (y_p.reshape(B, T, D_MODEL), y_s.reshape(NB, 1, D_MODEL),
            kv_prompt(ct), kv_sample(ct_s), kv_prompt(st), kv_sample(st_s),
            kv_prompt(wt[:, :, T - WINDOW:]), _rows_first(wbuf_s)[None],
            c_p[None], c_s[None], n_p[None], n_s[None], m_p[None, :, :, 0], m_s[None])
```

```python
import functools
import math

import numpy as np
import jax
import jax.numpy as jnp
from jax import lax
from jax.experimental import pallas as pl
from jax.experimental.pallas import tpu as pltpu

F32 = jnp.float32
BF16 = jnp.bfloat16
HIGHEST = lax.Precision.HIGHEST

D_MODEL = 1024
M_HEADS = 4
M_HD = 128
M_WIDTH = M_HEADS * M_HD
M_CHUNK = 128
A_HEADS = 8
A_HD = 64
A_KV = 2
A_GROUP = A_HEADS // A_KV
A_WIDTH = A_HEADS * A_HD
A_KVW = A_KV * A_HD
ROW_W = 2 * A_KVW
CMP_LEN = 32
CMP_STRIDE = 16
SLC_BLOCK = 64
N_SEL = 16
WINDOW = 512
N_BUCKETS = 32
MAX_EXACT = N_BUCKETS // 2
MAX_DIST = 128
FAR_BUCKET = N_BUCKETS - 1
LN_EPS = 1e-5
ATT_SCALE = A_HD ** -0.5
DEPTH = 1
DEEPNORM_ALPHA = (2.0 * DEPTH) ** 0.25
IN_SPLITS = (M_WIDTH,) * 5 + (M_HEADS, M_HEADS) + (A_WIDTH,) + (A_KVW,) * 6 + (3 * A_HEADS, A_WIDTH)

LANE = 128
SUBLANE = 8
TQ = 128
NEG = -1e30
LOG2E = math.log2(math.e)
MASKED_ID = N_BUCKETS
VMEM_LIMIT = 56 * 1024 * 1024

PM_W = 5 * M_WIDTH + LANE
PA_W = 2 * A_WIDTH + LANE
PW_TOTAL = PM_W + PA_W + 3 * ROW_W


def _cparams(*sem):
    return pltpu.CompilerParams(dimension_semantics=sem, vmem_limit_bytes=VMEM_LIMIT)


def _nt(a, b):
    return lax.dot_general(a, b, (((1,), (1,)), ((), ())), preferred_element_type=F32)


def _dot(a, b, precision=None):
    return jnp.dot(a, b, preferred_element_type=F32, precision=precision)


def _log_sigmoid(x):
    return jnp.minimum(x, 0.0) - jnp.log(1.0 + jnp.exp(-jnp.abs(x)))


def _silu(x):
    return x * jax.nn.sigmoid(x)


def _gelu_tanh(x):
    return 0.5 * x * (1.0 + jnp.tanh(math.sqrt(2.0 / math.pi) * (x + 0.044715 * (x * x * x))))


def _ln_rows(x):
    mu = jnp.mean(x, axis=-1, keepdims=True)
    xc = x - mu
    var = jnp.mean(xc * xc, axis=-1, keepdims=True)
    return xc * lax.rsqrt(var + LN_EPS)


def _bucket_np(dist):
    dist = np.asarray(dist, np.int64)
    n = np.maximum(dist, 0)
    nf = np.maximum(n, 1).astype(np.float32)
    large = MAX_EXACT + (np.log(nf / np.float32(MAX_EXACT)) / np.float32(math.log(MAX_DIST / MAX_EXACT))
                         * np.float32(N_BUCKETS - MAX_EXACT)).astype(np.int32)
    large = np.minimum(large, N_BUCKETS - 1)
    b = np.where(n < MAX_EXACT, n, large)
    return np.where(dist < 0, MASKED_ID, b).astype(np.int32)


def _mod_kernel(c_ref, w_ref, b_ref, o_ref):
    a = _silu(c_ref[...])
    o_ref[...] = _dot(a, w_ref[...]) + b_ref[...]


def _adaln_mod(c, w_ada, b_ada):
    rows = c.shape[0]
    n3 = w_ada.shape[1]
    tn = D_MODEL
    return pl.pallas_call(
        _mod_kernel,
        grid=(n3 // tn,),
        in_specs=[pl.BlockSpec((rows, D_MODEL), lambda j: (0, 0)),
                  pl.BlockSpec((D_MODEL, tn), lambda j: (0, j)),
                  pl.BlockSpec((1, tn), lambda j: (0, j))],
        out_specs=pl.BlockSpec((rows, tn), lambda j: (0, j)),
        out_shape=jax.ShapeDtypeStruct((rows, n3), F32),
        compiler_params=_cparams("arbitrary"),
        name="adaln_mod",
    )(c, w_ada, b_ada.reshape(1, n3))


def _bias_kernel(rb_ref, ids_ref, o_ref, *, n_groups):
    def body(i, carry):
        r0 = pl.multiple_of(i * SUBLANE, SUBLANE)
        ids = ids_ref[pl.ds(r0, SUBLANE), :]
        for h in range(A_HEADS):
            acc = jnp.full((SUBLANE, LANE), NEG, F32)
            for b in range(N_BUCKETS):
                acc = jnp.where(ids == b, rb_ref[b, h], acc)
            o_ref[h, pl.ds(r0, SUBLANE), :] = acc
        return carry

    lax.fori_loop(0, n_groups, body, 0)


def _bias_tables(rel_bias, ids):
    rows = ids.shape[0]
    return pl.pallas_call(
        functools.partial(_bias_kernel, n_groups=rows // SUBLANE),
        in_specs=[pl.BlockSpec(memory_space=pltpu.SMEM),
                  pl.BlockSpec((rows, LANE), lambda: (0, 0))],
        out_specs=pl.BlockSpec((A_HEADS, rows, LANE), lambda: (0, 0, 0)),
        out_shape=jax.ShapeDtypeStruct((A_HEADS, rows, LANE), F32),
        name="bias_tables",
    )(rel_bias, jnp.asarray(ids))


def _pack_in_proj(w_in, b_in, dtype):
    offs = np.cumsum((0,) + IN_SPLITS)
    names = ("mq", "mk", "mv", "mo", "mz", "mi", "mf", "aq", "ck", "cv", "sk", "sv", "wk", "wv", "ga", "za")
    sl = {n: (int(offs[i]), int(offs[i + 1])) for i, n in enumerate(names)}

    def cols(a, name, lo=None, hi=None):
        s, e = sl[name]
        if lo is not None:
            s, e = s + lo, s + hi
        return a[..., s:e]

    def rows_of(a, kn, vn):
        return [cols(a, kn, 0, A_HD), cols(a, vn, 0, A_HD), cols(a, kn, A_HD, 2 * A_HD), cols(a, vn, A_HD, 2 * A_HD)]

    def pack(a):
        def zeros(n):
            return jnp.zeros(a.shape[:-1] + (n,), a.dtype)
        parts = [cols(a, n) for n in ("mq", "mk", "mv", "mo", "mz")]
        parts += [cols(a, "mi"), cols(a, "mf"), zeros(LANE - 2 * M_HEADS)]
        parts += [cols(a, "aq"), cols(a, "za"), cols(a, "ga"), zeros(LANE - 3 * A_HEADS)]
        parts += rows_of(a, "ck", "cv") + rows_of(a, "sk", "sv") + rows_of(a, "wk", "wv")
        return jnp.concatenate(parts, axis=-1)

    w = pack(w_in)
    b = pack(b_in.reshape(1, -1))
    wt = w[:, PM_W + PA_W:].T
    bt = b[:, PM_W + PA_W:].reshape(-1, 1)
    return w.astype(dtype), b, wt.astype(dtype), bt


def _proj_kernel(x_ref, sh_ref, sc_ref, w_ref, b_ref, wt_ref, bt_ref,
                 om_ref, oa_ref, oc_ref, os_ref, ow_ref, oct_ref, ost_ref, owt_ref):
    h = _ln_rows(x_ref[...]) * (1.0 + sc_ref[...]) + sh_ref[...]
    hb = h.astype(w_ref.dtype)
    precision = HIGHEST if w_ref.dtype == F32 else None
    lo = 0
    for o_ref in (om_ref, oa_ref, oc_ref, os_ref, ow_ref):
        n = o_ref.shape[-1]
        o_ref[...] = _dot(hb, w_ref[:, lo:lo + n], precision) + b_ref[:, lo:lo + n]
        lo += n
    t = lax.dot_general(wt_ref[...], hb, (((1,), (1,)), ((), ())), preferred_element_type=F32,
                        precision=precision) + bt_ref[...]
    for i, o_ref in enumerate((oct_ref, ost_ref, owt_ref)):
        o_ref[0] = t[i * ROW_W:(i + 1) * ROW_W]


def _project(x, shift, scale, packed, groups, tm):
    w, b, wt, bt = packed
    rows = x.shape[0]
    per = rows // groups // tm
    if shift.ndim == 3:
        mod_spec = pl.BlockSpec((None, 1, D_MODEL), lambda i: (i // per, 0, 0))
    else:
        mod_spec = pl.BlockSpec((tm, D_MODEL), lambda i: (i, 0))
    widths = (PM_W, PA_W, ROW_W, ROW_W, ROW_W)
    return pl.pallas_call(
        _proj_kernel,
        grid=(rows // tm,),
        in_specs=[pl.BlockSpec((tm, D_MODEL), lambda i: (i, 0)), mod_spec, mod_spec,
                  pl.BlockSpec((D_MODEL, PW_TOTAL), lambda i: (0, 0)),
                  pl.BlockSpec((1, PW_TOTAL), lambda i: (0, 0)),
                  pl.BlockSpec((3 * ROW_W, D_MODEL), lambda i: (0, 0)),
                  pl.BlockSpec((3 * ROW_W, 1), lambda i: (0, 0))],
        out_specs=[pl.BlockSpec((tm, n), lambda i: (i, 0)) for n in widths]
                  + [pl.BlockSpec((1, ROW_W, tm), lambda i: (i // per, 0, i % per))] * 3,
        out_shape=[jax.ShapeDtypeStruct((rows, n), F32) for n in widths]
                  + [jax.ShapeDtypeStruct((groups, ROW_W, rows // groups), F32)] * 3,
        compiler_params=_cparams("arbitrary"),
        name="in_proj",
    )(x, shift, scale, w, b, wt, bt)


def _mlstm_head_out(h, o_pre, z_pre, g_row):
    return jax.nn.sigmoid(o_pre) * (_ln_rows(h) * g_row) * _silu(z_pre)


def _mlstm_prompt_kernel(q_ref, k_ref, v_ref, o_ref, z_ref, g_ref, ng_ref, mix_ref, c_ref, n_ref, m_ref):
    L = M_CHUNK

    @pl.when(pl.program_id(0) == 0)
    def _():
        c_ref[...] = jnp.zeros_like(c_ref)
        n_ref[...] = jnp.zeros_like(n_ref)
        m_ref[...] = jnp.zeros_like(m_ref)

    row = lax.broadcasted_iota(jnp.int32, (L, L), 0)
    col = lax.broadcasted_iota(jnp.int32, (L, L), 1)
    tril = col <= row
    lower = tril.astype(F32)
    upper = (row <= col).astype(F32)
    nb = q_ref.shape[0]
    units = [(b, h) for b in range(nb) for h in range(M_HEADS)]
    gates = [g_ref[b] for b in range(nb)]
    gates_t = [g.T for g in gates]
    cum = [_dot(lower, _log_sigmoid(g), HIGHEST) for g in gates]
    cum_t = [_dot(_log_sigmoid(g), upper, HIGHEST) for g in gates_t]
    st = {}
    for b, h in units:
        hs = slice(h * M_HD, (h + 1) * M_HD)
        b_col = cum[b][:, M_HEADS + h:M_HEADS + h + 1]
        b_row = cum_t[b][M_HEADS + h:M_HEADS + h + 1, :]
        m_prev = m_ref[b, h:h + 1, 0:1]
        d = jnp.where(tril, b_col - b_row + gates_t[b][h:h + 1, :], NEG)
        inter = b_col + m_prev
        m_t = jnp.maximum(inter, jnp.max(d, axis=1, keepdims=True))
        q = q_ref[b, :, hs]
        ks = k_ref[b, :, hs] * (M_HD ** -0.5)
        st[b, h] = dict(hs=hs, b_col=b_col, m_prev=m_prev, d=d, m_t=m_t, w_inter=jnp.exp(inter - m_t), q=q, ks=ks,
                        qb=q.astype(BF16), kb=ks.astype(BF16), vb=v_ref[b, :, hs].astype(BF16),
                        c_prev=c_ref[b, h], n_prev=n_ref[b, h:h + 1, :])
    for u in units:
        s = st[u]
        s["qk"] = _nt(s["qb"], s["kb"]) * jnp.exp(s["d"] - s["m_t"])
        s["qc"] = _dot(s["qb"], s["c_prev"].astype(BF16))
    for u in units:
        s = st[u]
        num = s["w_inter"] * s["qc"] + _dot(s["qk"].astype(BF16), s["vb"])
        den = (s["w_inter"] * jnp.sum(s["q"] * s["n_prev"], axis=1, keepdims=True)
               + jnp.sum(s["qk"], axis=1, keepdims=True))
        s["hh"] = num / jnp.maximum(jnp.abs(den), jnp.exp(-s["m_t"]))
    for (b, h) in units:
        s = st[b, h]
        m_new = s["m_t"][L - 1:L, :]
        b_last = s["b_col"][L - 1:L, :]
        w_c = jnp.exp(b_last + s["m_prev"] - m_new)
        w_s = jnp.exp(b_last - s["b_col"] + gates[b][:, h:h + 1] - m_new)
        kw = s["ks"] * w_s
        c_ref[b, h] = w_c * s["c_prev"] + _dot(kw.T.astype(BF16), s["vb"])
        n_ref[b, h:h + 1, :] = w_c * s["n_prev"] + jnp.sum(kw, axis=0, keepdims=True)
        m_ref[b, h:h + 1, :] = jnp.broadcast_to(m_new, (1, M_HD))
    for (b, h) in units:
        s = st[b, h]
        hs = s["hs"]
        mix_ref[b, :, hs] = _mlstm_head_out(s["hh"], o_ref[b, :, hs], z_ref[b, :, hs], ng_ref[:, hs])


def _mlstm_prompt(pm, norm_g, batch, seq):
    nc = seq // M_CHUNK
    pm3 = pm.reshape(batch, seq, PM_W)

    def col_spec(j, width=M_WIDTH):
        return pl.BlockSpec((batch, M_CHUNK, width), lambda c: (0, c, j))

    state = lambda c: (0, 0, 0)
    mix, c_p, n_p, m_p = pl.pallas_call(
        _mlstm_prompt_kernel,
        grid=(nc,),
        in_specs=[col_spec(0), col_spec(1), col_spec(2), col_spec(3), col_spec(4),
                  pl.BlockSpec((batch, M_CHUNK, LANE), lambda c: (0, c, 5 * M_WIDTH // LANE)),
                  pl.BlockSpec((1, M_WIDTH), lambda c: (0, 0))],
        out_specs=[pl.BlockSpec((batch, M_CHUNK, M_WIDTH), lambda c: (0, c, 0)),
                   pl.BlockSpec((batch, M_HEADS, M_HD, M_HD), lambda c: (0, 0, 0, 0)),
                   pl.BlockSpec((batch, M_HEADS, M_HD), state),
                   pl.BlockSpec((batch, M_HEADS, M_HD), state)],
        out_shape=[jax.ShapeDtypeStruct((batch, seq, M_WIDTH), F32),
                   jax.ShapeDtypeStruct((batch, M_HEADS, M_HD, M_HD), F32),
                   jax.ShapeDtypeStruct((batch, M_HEADS, M_HD), F32),
                   jax.ShapeDtypeStruct((batch, M_HEADS, M_HD), F32)],
        compiler_params=_cparams("arbitrary"),
        name="mlstm_prompt",
    )(pm3, pm3, pm3, pm3, pm3, pm3, norm_g.reshape(1, M_WIDTH))
    return mix.reshape(batch * seq, M_WIDTH), c_p, n_p, m_p


MS_G = 128


def _mlstm_sample_kernel(q_ref, k_ref, v_ref, o_ref, z_ref, ig_ref, fg_ref, m_ref, n_ref, c_ref, ng_ref,
                         mix_ref, co_ref, no_ref, mo_ref):
    q = q_ref[...]
    ks = k_ref[...] * (M_HD ** -0.5)
    v = v_ref[...]
    n_prev = n_ref[...]
    ig = ig_ref[...]
    inter = _log_sigmoid(fg_ref[...]) + m_ref[...]
    m_t = jnp.maximum(inter, ig)
    w_inter = jnp.exp(inter - m_t)
    w_s = jnp.exp(ig - m_t)
    qk = jnp.sum(q * ks, axis=1, keepdims=True) * w_s
    q_t = q.T
    kw_t = (ks * w_s).T
    rows = []
    for r in range(MS_G):
        c_prev = c_ref[r]
        rows.append(jnp.sum(q_t[:, r:r + 1] * c_prev, axis=0, keepdims=True))
        co_ref[r] = w_inter[r:r + 1, :] * c_prev + kw_t[:, r:r + 1] * v[r:r + 1, :]
    q_c = jnp.concatenate(rows, axis=0)
    num = w_inter * q_c + qk * v
    den = w_inter * jnp.sum(q * n_prev, axis=1, keepdims=True) + qk
    hh = num / jnp.maximum(jnp.abs(den), jnp.exp(-m_t))
    no_ref[...] = w_inter * n_prev + w_s * ks
    mo_ref[...] = m_t
    mix_ref[...] = _mlstm_head_out(hh, o_ref[...], z_ref[...], ng_ref[...])


def _mlstm_sample(pm, norm_g, c0, n0, m0):
    nb = pm.shape[0]
    rows = nb * M_HEADS

    def head_rows(j):
        return pm[:, j * M_WIDTH:(j + 1) * M_WIDTH].reshape(rows, M_HD)

    gates = pm[:, 5 * M_WIDTH:5 * M_WIDTH + 2 * M_HEADS]
    ig = gates[:, :M_HEADS].reshape(rows, 1)
    fg = gates[:, M_HEADS:].reshape(rows, 1)
    ng_rows = jnp.tile(norm_g.reshape(M_HEADS, M_HD), (MS_G // M_HEADS, 1))
    vec = pl.BlockSpec((MS_G, M_HD), lambda i: (i, 0))
    one = pl.BlockSpec((MS_G, 1), lambda i: (i, 0))
    mat = pl.BlockSpec((MS_G, M_HD, M_HD), lambda i: (i, 0, 0))
    mix, c1, n1, m1 = pl.pallas_call(
        _mlstm_sample_kernel,
        grid=(rows // MS_G,),
        in_specs=[vec] * 5 + [one] * 3 + [vec, mat, pl.BlockSpec((MS_G, M_HD), lambda i: (0, 0))],
        out_specs=[vec, mat, vec, one],
        out_shape=[jax.ShapeDtypeStruct((rows, M_HD), F32),
                   jax.ShapeDtypeStruct((rows, M_HD, M_HD), F32),
                   jax.ShapeDtypeStruct((rows, M_HD), F32),
                   jax.ShapeDtypeStruct((rows, 1), F32)],
        compiler_params=_cparams("arbitrary"),
        name="mlstm_sample",
    )(*[head_rows(j) for j in range(5)], ig, fg, m0.reshape(rows, 1), n0.reshape(rows, M_HD),
      c0.reshape(rows, M_HD, M_HD), ng_rows)
    return (mix.reshape(nb, M_WIDTH), c1.reshape(nb, M_HEADS, M_HD, M_HD), n1.reshape(nb, M_HEADS, M_HD),
            m1.reshape(nb, M_HEADS))


KVROW_W = 2 * A_HD
HALF_W = 2 * KVROW_W


def _pack_compress(w1, w2):
    def block_diag(k, v):
        z = jnp.zeros_like(k)
        return jnp.concatenate([jnp.concatenate([k, z], axis=-1), jnp.concatenate([z, v], axis=-1)], axis=-2)

    wbd = block_diag(w1[0], w1[1])
    wcat = jnp.concatenate([wbd[:CMP_STRIDE], wbd[CMP_STRIDE:]], axis=-1)
    return wcat.reshape(CMP_STRIDE * KVROW_W, HALF_W).astype(BF16), block_diag(w2[0], w2[1]).astype(BF16)


def _cmp_const_kernel(pe_ref, w_ref, b_ref, o_ref):
    for c in range(2):
        o_ref[c] = _dot(pe_ref[c], w_ref[c], HIGHEST) + b_ref[c]


def _compress_const(pe, w1, b1):
    k = CMP_LEN * A_HD
    pe8 = jnp.broadcast_to(pe.reshape(2, 1, k), (2, SUBLANE, k))
    out = pl.pallas_call(
        _cmp_const_kernel,
        out_shape=jax.ShapeDtypeStruct((2, SUBLANE, A_HD), F32),
        name="compress_const",
    )(pe8, w1.reshape(2, k, A_HD), b1.reshape(2, 1, A_HD))
    return jnp.concatenate([out[0, 0:1], out[1, 0:1]], axis=-1)


def _compress_halves(halves, wcat_ref, const_ref, w2_ref, n_half):
    acc = _dot(halves.astype(BF16), wcat_ref[...])
    pre = acc[:, :KVROW_W] + pltpu.roll(acc[:, KVROW_W:], A_KV * n_half - 1, 0) + const_ref[...]
    return _dot(_gelu_tanh(pre).astype(BF16), w2_ref[...])


def _compress_prompt_kernel(x0_ref, x1_ref, wcat_ref, const_ref, w2_ref, kk_ref, kvt_ref, *, n_half):
    x_refs = (x0_ref, x1_ref)
    halves = jnp.concatenate(
        [jnp.concatenate([x_refs[kv][pl.ds(p, n_half, stride=CMP_STRIDE), :] for kv in range(A_KV)], axis=0)
         for p in range(CMP_STRIDE)], axis=1)
    kc = _compress_halves(halves, wcat_ref, const_ref, w2_ref, n_half)
    kct = kc.T
    for kv in range(A_KV):
        kk_ref[0, kv] = kc[kv * n_half:(kv + 1) * n_half, 0:A_HD]
        kvt_ref[0, kv] = kct[A_HD:, kv * n_half:(kv + 1) * n_half]


def _compress_prompt(rows, wcat, const_row, w2bd, batch, seq):
    n_half = seq // CMP_STRIDE
    return pl.pallas_call(
        functools.partial(_compress_prompt_kernel, n_half=n_half),
        grid=(batch,),
        in_specs=[pl.BlockSpec((seq, KVROW_W), lambda b: (b, 0)),
                  pl.BlockSpec((seq, KVROW_W), lambda b: (b, 1)),
                  pl.BlockSpec((CMP_STRIDE * KVROW_W, HALF_W), lambda b: (0, 0)),
                  pl.BlockSpec((1, KVROW_W), lambda b: (0, 0)),
                  pl.BlockSpec((KVROW_W, KVROW_W), lambda b: (0, 0))],
        out_specs=[pl.BlockSpec((1, A_KV, n_half, A_HD), lambda b: (b, 0, 0, 0)),
                   pl.BlockSpec((1, A_KV, A_HD, n_half), lambda b: (b, 0, 0, 0))],
        out_shape=[jax.ShapeDtypeStruct((batch, A_KV, n_half, A_HD), F32),
                   jax.ShapeDtypeStruct((batch, A_KV, A_HD, n_half), F32)],
        compiler_params=_cparams("arbitrary"),
        name="compress_prompt",
    )(rows, rows, wcat, const_row, w2bd)


CMP_PAT = 16


def _static_ids(p_len):
    i = np.arange(TQ)[None, :]
    c = np.arange(CMP_PAT)[:, None]
    cmp_a = _bucket_np(i + (TQ - (CMP_LEN - 1)) - CMP_STRIDE * c)
    cmp_b = _bucket_np(i - CMP_STRIDE * c - (CMP_LEN - 1))
    r = np.arange(TQ)[:, None]
    slc_diag = _bucket_np(i - r)
    slc_sub = _bucket_np(TQ + i - r)
    slc_far = np.full((TQ, TQ), FAR_BUCKET, np.int32)
    rw = np.arange(2 * WINDOW + TQ)[:, None]
    dw = WINDOW + i - rw
    win = np.where(dw > WINDOW, MASKED_ID, _bucket_np(dw))
    n_half = p_len // CMP_STRIDE
    n = np.arange(n_half)
    cs = _bucket_np(p_len - (CMP_STRIDE * n + CMP_LEN - 1))
    cs[n_half - 1] = MASKED_ID
    cs_rows = -(-n_half // LANE)
    cs_pad = np.full((cs_rows * LANE,), MASKED_ID, np.int32)
    cs_pad[:n_half] = cs
    ws = _bucket_np(WINDOW - np.arange(WINDOW))
    last_page = _bucket_np(PAGE_ROWS - np.arange(PAGE_ROWS))
    parts = [cmp_a, cmp_b, slc_diag, slc_sub, slc_far, win, cs_pad.reshape(cs_rows, LANE),
             ws.reshape(WINDOW // LANE, LANE), last_page.reshape(1, LANE)]
    offs = np.cumsum([0] + [p.shape[0] for p in parts])
    total = -(-int(offs[-1]) // SUBLANE) * SUBLANE
    ids = np.full((total, LANE), MASKED_ID, np.int32)
    ids[:offs[-1]] = np.concatenate(parts, axis=0)
    return ids, [int(o) for o in offs]


def _cover_np(n_cmp_rows, n_cmp, n_slc_rows, n_slc):
    cs = np.arange(n_cmp_rows)[:, None] * CMP_STRIDE
    ss = np.arange(n_slc_rows)[None, :] * SLC_BLOCK
    cov = (cs <= ss + SLC_BLOCK - 1) & (cs + CMP_LEN - 1 >= ss)
    cov &= (np.arange(n_cmp_rows)[:, None] < n_cmp) & (np.arange(n_slc_rows)[None, :] < n_slc)
    return cov.astype(np.float32)


def _softmax_keys_on_rows(s):
    m = jnp.max(s, axis=0, keepdims=True)
    m = jnp.where(m > 0.5 * NEG, m, 0.0)
    e = jnp.exp(s - m)
    tot = jnp.sum(e, axis=0, keepdims=True)
    return e / jnp.where(tot > 0.0, tot, 1.0)


def _softmax_keys_on_lanes(s, s_new=None):
    m = jnp.max(s, axis=1, keepdims=True)
    if s_new is not None:
        m = jnp.maximum(m, s_new)
    m = jnp.where(m > 0.5 * NEG, m, 0.0)
    e = jnp.exp(s - m)
    tot = jnp.sum(e, axis=1, keepdims=True)
    e_new = None
    if s_new is not None:
        e_new = jnp.exp(s_new - m)
        tot = tot + e_new
    return e, e_new, 1.0 / jnp.where(tot > 0.0, tot, 1.0)


def _cmp_attend_kernel(far_ref, q_ref, kk_ref, kvt_ref, pt_ref, cov_ref, o_ref, sel_ref, bscr, *, nc, ns):
    k = pl.program_id(1)
    start = pl.multiple_of(jnp.maximum(SUBLANE * k - SUBLANE, 0), SUBLANE)
    variant = jnp.where(k == 0, 1, 0)
    row = lax.broadcasted_iota(jnp.int32, (nc, GROUP_LANES), 0)
    t = k * TQ + lax.broadcasted_iota(jnp.int32, (ns, TQ), 1)
    blk = lax.broadcasted_iota(jnp.int32, (ns, TQ), 0)
    cur = t // SLC_BLOCK
    valid = blk * SLC_BLOCK <= t
    forced = (blk == 0) | (blk == cur) | (blk == cur - 1)
    kvs = range(A_KV)
    for kv in kvs:
        bscr[kv] = jnp.where(row < start, far_ref[kv], NEG)
        bscr[kv, pl.ds(start, CMP_PAT), :] = pt_ref[variant, kv]
    s = [_nt(kk_ref[0, kv].astype(BF16), _stacked_queries(q_ref, kv)) + bscr[kv] for kv in kvs]
    p = [_softmax_keys_on_rows(s[kv]) for kv in kvs]
    o = [_dot(kvt_ref[0, kv].astype(BF16), p[kv].astype(BF16)) for kv in kvs]
    for kv in kvs:
        for g in range(A_GROUP):
            h = kv * A_GROUP + g
            o_ref[0, h * A_HD:(h + 1) * A_HD, :] = o[kv][:, g * TQ:(g + 1) * TQ]
    imp = [sum(p[kv][:, g * TQ:(g + 1) * TQ] for g in range(A_GROUP)) for kv in kvs]
    sc = [jnp.where(forced, jnp.inf, jnp.where(valid, _dot(cov_ref[...], imp[kv], HIGHEST), -jnp.inf)) for kv in kvs]
    cnt = [jnp.zeros((ns, TQ), jnp.int32) for _ in kvs]
    for j in range(ns):
        for kv in kvs:
            r = sc[kv][j:j + 1, :]
            before = (r > sc[kv]) | ((r == sc[kv]) & (blk > j))
            cnt[kv] = cnt[kv] + before.astype(jnp.int32)
    for kv in kvs:
        sel_ref[0, kv] = jnp.where(cnt[kv] < N_SEL, 0.0, NEG)


def _cmp_attend(far, pa, kk, kvt, pat, cov_t, batch, seq):
    nq = seq // TQ
    nc = seq // CMP_STRIDE
    ns = seq // SLC_BLOCK
    return pl.pallas_call(
        functools.partial(_cmp_attend_kernel, nc=nc, ns=ns),
        grid=(batch, nq),
        in_specs=[pl.BlockSpec((A_KV, 1, GROUP_LANES), lambda b, k: (0, 0, 0)),
                  pl.BlockSpec((TQ, A_WIDTH), lambda b, k: (b * nq + k, 0)),
                  pl.BlockSpec((1, A_KV, nc, A_HD), lambda b, k: (b, 0, 0, 0)),
                  pl.BlockSpec((1, A_KV, A_HD, nc), lambda b, k: (b, 0, 0, 0)),
                  pl.BlockSpec((2, A_KV, CMP_PAT, GROUP_LANES), lambda b, k: (0, 0, 0, 0)),
                  pl.BlockSpec((ns, nc), lambda b, k: (0, 0))],
        out_specs=[pl.BlockSpec((1, A_WIDTH, TQ), lambda b, k: (b, 0, k)),
                   pl.BlockSpec((1, A_KV, ns, TQ), lambda b, k: (b, 0, 0, k))],
        out_shape=[jax.ShapeDtypeStruct((batch, A_WIDTH, seq), F32),
                   jax.ShapeDtypeStruct((batch, A_KV, ns, seq), F32)],
        scratch_shapes=[pltpu.VMEM((A_KV, nc, GROUP_LANES), F32)],
        compiler_params=_cparams("arbitrary", "arbitrary"),
        name="cmp_attend",
    )(far, pa, kk, kvt, pat, cov_t)


SLC_CK = 4 * TQ
GROUP_LANES = A_GROUP * TQ
SLC_CLASSES = 4
SLC_HALVES = 2


def _stacked_queries(q_ref, kv, scale=ATT_SCALE):
    heads = [q_ref[:, (kv * A_GROUP + g) * A_HD:(kv * A_GROUP + g + 1) * A_HD] for g in range(A_GROUP)]
    return (jnp.concatenate(heads, axis=0) * scale).astype(BF16)


def _slc_attend_kernel(q_ref, k_ref, vt_ref, sel_ref, tab_ref, o_ref):
    k = pl.program_id(1)
    sub = SLC_CK // TQ
    hw = GROUP_LANES // SLC_HALVES
    heads_per_half = A_GROUP // SLC_HALVES
    q4 = [_stacked_queries(q_ref, kv, ATT_SCALE * LOG2E) for kv in range(A_KV)]
    upper = lax.broadcasted_iota(jnp.int32, (TQ, hw), 0) < SLC_BLOCK
    kvs = range(A_KV)

    def body(j, carry):
        j0 = pl.multiple_of(j * SLC_CK, SLC_CK)
        kj = [k_ref[pl.ds(j0, SLC_CK), kv * LANE:kv * LANE + A_HD].astype(BF16) for kv in kvs]
        vt = [vt_ref[0, kv * LANE + A_HD:(kv + 1) * LANE, pl.ds(j0, SLC_CK)].astype(BF16) for kv in kvs]
        out = [None] * (A_KV * SLC_HALVES)
        for hf in range(SLC_HALVES):
            lanes = slice(hf * hw, (hf + 1) * hw)
            s_all = [_nt(kj[kv], q4[kv][hf * hw:(hf + 1) * hw]) for kv in kvs]
            s = []
            for kv in kvs:
                parts = []
                for u in range(sub):
                    jj = j * sub + u
                    sel0 = jnp.concatenate([sel_ref[0, kv, pl.ds(2 * jj, 1), :]] * heads_per_half, axis=1)
                    sel1 = jnp.concatenate([sel_ref[0, kv, pl.ds(2 * jj + 1, 1), :]] * heads_per_half, axis=1)
                    cls = jnp.where(jj > k, SLC_CLASSES - 1, jnp.minimum(k - jj, 2))
                    parts.append(s_all[kv][u * TQ:(u + 1) * TQ] + tab_ref[cls, kv, :, lanes]
                                 + jnp.where(upper, sel0, sel1))
                s.append(jnp.concatenate(parts, axis=0))
            units = [kv * SLC_HALVES + hf for kv in kvs]
            m_new = [jnp.maximum(carry[units[kv]][0], jnp.max(s[kv], axis=0, keepdims=True)) for kv in kvs]
            p = [jnp.exp2(s[kv] - m_new[kv]) for kv in kvs]
            pv = [_dot(vt[kv], p[kv].astype(BF16)) for kv in kvs]
            for kv in kvs:
                m_run, l_run, acc = carry[units[kv]]
                alpha = jnp.exp2(m_run - m_new[kv])
                l_new = alpha * l_run + jnp.sum(p[kv], axis=0, keepdims=True)
                out[units[kv]] = (m_new[kv], l_new, alpha * acc + pv[kv])
        return tuple(out)

    init = tuple((jnp.full((1, hw), NEG, F32), jnp.zeros((1, hw), F32), jnp.zeros((A_HD, hw), F32))
                 for _ in range(A_KV * SLC_HALVES))
    res = lax.fori_loop(0, (k + sub) // sub, body, init)
    for kv in kvs:
        for hf in range(SLC_HALVES):
            _, l_run, acc = res[kv * SLC_HALVES + hf]
            o = acc / l_run
            for g in range(heads_per_half):
                h = kv * A_GROUP + hf * heads_per_half + g
                o_ref[0, h * A_HD:(h + 1) * A_HD, :] = o[:, g * TQ:(g + 1) * TQ]


def _slc_attend(pa, rows, rows_t, sel, tab, batch, seq):
    nq = seq // TQ
    ns = seq // SLC_BLOCK
    return pl.pallas_call(
        _slc_attend_kernel,
        grid=(batch, nq),
        in_specs=[pl.BlockSpec((TQ, A_WIDTH), lambda b, k: (b * nq + k, 0)),
                  pl.BlockSpec((seq, ROW_W), lambda b, k: (b, 0)),
                  pl.BlockSpec((1, ROW_W, seq), lambda b, k: (b, 0, 0)),
                  pl.BlockSpec((1, A_KV, ns, TQ), lambda b, k: (b, 0, 0, k)),
                  pl.BlockSpec((SLC_CLASSES, A_KV, TQ, GROUP_LANES), lambda b, k: (0, 0, 0, 0))],
        out_specs=pl.BlockSpec((1, A_WIDTH, TQ), lambda b, k: (b, 0, k)),
        out_shape=jax.ShapeDtypeStruct((batch, A_WIDTH, seq), F32),
        compiler_params=_cparams("arbitrary", "arbitrary"),
        name="slc_attend",
    )(pa, rows, rows_t, sel, tab)


WIN_SPAN = WINDOW + TQ


def _win_attend_kernel(q_ref, k_ref, vt_ref, bias_ref, o_ref):
    k = pl.program_id(1)
    key0 = pl.multiple_of(jnp.maximum(k * TQ - WINDOW, 0), TQ)
    row0 = pl.multiple_of(jnp.maximum(WINDOW - k * TQ, 0), TQ)
    kvs = range(A_KV)
    s = [_nt(k_ref[0, pl.ds(key0, WIN_SPAN), kv * LANE:kv * LANE + A_HD].astype(BF16), _stacked_queries(q_ref, kv))
         + bias_ref[kv, pl.ds(row0, WIN_SPAN), :] for kv in kvs]
    p = [_softmax_keys_on_rows(s[kv]) for kv in kvs]
    o = [_dot(vt_ref[0, kv * LANE + A_HD:(kv + 1) * LANE, pl.ds(key0, WIN_SPAN)].astype(BF16), p[kv].astype(BF16))
         for kv in kvs]
    for kv in kvs:
        for g in range(A_GROUP):
            h = kv * A_GROUP + g
            o_ref[0, h * A_HD:(h + 1) * A_HD, :] = o[kv][:, g * TQ:(g + 1) * TQ]


def _win_attend(pa, rows, rows_t, bias_w, batch, seq):
    nq = seq // TQ
    assert seq >= WIN_SPAN
    return pl.pallas_call(
        _win_attend_kernel,
        grid=(batch, nq),
        in_specs=[pl.BlockSpec((TQ, A_WIDTH), lambda b, k: (b * nq + k, 0)),
                  pl.BlockSpec((1, seq, ROW_W), lambda b, k: (b, 0, 0)),
                  pl.BlockSpec((1, ROW_W, seq), lambda b, k: (b, 0, 0)),
                  pl.BlockSpec((A_KV, WIN_SPAN + WINDOW, GROUP_LANES), lambda b, k: (0, 0, 0))],
        out_specs=pl.BlockSpec((1, A_WIDTH, TQ), lambda b, k: (b, 0, k)),
        out_shape=jax.ShapeDtypeStruct((batch, A_WIDTH, seq), F32),
        compiler_params=_cparams("arbitrary", "arbitrary"),
        name="win_attend",
    )(pa, rows, rows_t, bias_w)


def _out_tail(x, mix_m, mix_a, gate, w_ref, b_ref, g_ref, beta_ref):
    y = (_dot(mix_m.astype(BF16), w_ref[:M_WIDTH]) + _dot(mix_a.astype(BF16), w_ref[M_WIDTH:]) + b_ref[...])
    return _ln_rows(DEEPNORM_ALPHA * x + gate * y) * g_ref[...] + beta_ref[...]


OUT_TM = 512


def _out_prompt_kernel(x_ref, mm_ref, oc_ref, os_ref, ow_ref, ga_ref, za_ref, gate_ref,
                       w_ref, b_ref, g_ref, beta_ref, y_ref):
    sig = jax.nn.sigmoid(ga_ref[...].T)
    parts = []
    for h in range(A_HEADS):
        hs = slice(h * A_HD, (h + 1) * A_HD)
        parts.append(sig[h:h + 1] * oc_ref[0, hs, :] + sig[A_HEADS + h:A_HEADS + h + 1] * os_ref[0, hs, :]
                     + sig[2 * A_HEADS + h:2 * A_HEADS + h + 1] * ow_ref[0, hs, :])
    ha = jnp.concatenate(parts, axis=0).T
    mix_a = ha * _silu(za_ref[...])
    y_ref[...] = _out_tail(x_ref[...], mm_ref[...], mix_a, gate_ref[...], w_ref, b_ref, g_ref, beta_ref)


def _out_prompt(x, mix_m, o_c, o_s, o_w, pa, gate, w_out, b_out, ln_g, ln_b, batch, seq):
    tm = OUT_TM
    nq = seq // tm
    rows = batch * seq
    branch = pl.BlockSpec((1, A_WIDTH, tm), lambda i: (i // nq, 0, i % nq))
    vec = pl.BlockSpec((1, D_MODEL), lambda i: (0, 0))
    return pl.pallas_call(
        _out_prompt_kernel,
        grid=(rows // tm,),
        in_specs=[pl.BlockSpec((tm, D_MODEL), lambda i: (i, 0)),
                  pl.BlockSpec((tm, M_WIDTH), lambda i: (i, 0)),
                  branch, branch, branch,
                  pl.BlockSpec((tm, LANE), lambda i: (i, 2 * A_WIDTH // LANE)),
                  pl.BlockSpec((tm, A_WIDTH), lambda i: (i, 1)),
                  pl.BlockSpec((None, 1, D_MODEL), lambda i: (i // nq, 0, 0)),
                  pl.BlockSpec((D_MODEL, D_MODEL), lambda i: (0, 0)),
                  vec, vec, vec],
        out_specs=pl.BlockSpec((tm, D_MODEL), lambda i: (i, 0)),
        out_shape=jax.ShapeDtypeStruct((rows, D_MODEL), F32),
        compiler_params=_cparams("arbitrary"),
        name="out_prompt",
    )(x, mix_m, o_c, o_s, o_w, pa, pa, gate, w_out, b_out, ln_g, ln_b)


def _out_sample_kernel(x_ref, mm_ref, ha_ref, za_ref, gate_ref, w_ref, b_ref, g_ref, beta_ref, y_ref):
    mix_a = ha_ref[...] * _silu(za_ref[...])
    y_ref[...] = _out_tail(x_ref[...], mm_ref[...], mix_a, gate_ref[...], w_ref, b_ref, g_ref, beta_ref)


def _out_sample(x, mix_m, ha, pa, gate, w_out, b_out, ln_g, ln_b):
    rows = x.shape[0]
    vec = pl.BlockSpec((1, D_MODEL), lambda i: (0, 0))
    return pl.pallas_call(
        _out_sample_kernel,
        grid=(1,),
        in_specs=[pl.BlockSpec((rows, D_MODEL), lambda i: (0, 0)),
                  pl.BlockSpec((rows, M_WIDTH), lambda i: (0, 0)),
                  pl.BlockSpec((rows, A_WIDTH), lambda i: (0, 0)),
                  pl.BlockSpec((rows, A_WIDTH), lambda i: (0, 1)),
                  pl.BlockSpec((rows, D_MODEL), lambda i: (0, 0)),
                  pl.BlockSpec((D_MODEL, D_MODEL), lambda i: (0, 0)),
                  vec, vec, vec],
        out_specs=pl.BlockSpec((rows, D_MODEL), lambda i: (0, 0)),
        out_shape=jax.ShapeDtypeStruct((rows, D_MODEL), F32),
        compiler_params=_cparams("arbitrary"),
        name="out_sample",
    )(x, mix_m, ha, pa, gate, w_out, b_out, ln_g, ln_b)


def _nsa_prompt(rel_bias, pa, rc, rs, rw, st, wt, bias, offs, cmp_w, const_row, batch, seq):
    wcat, w2bd = cmp_w
    nc = seq // CMP_STRIDE
    ns = seq // SLC_BLOCK
    kk, kvt = _compress_prompt(rc, wcat, const_row, w2bd, batch, seq)

    def group_lanes(tiles):
        rows = tiles.shape[1]
        return tiles.reshape(A_KV, A_GROUP, rows, TQ).transpose(0, 2, 1, 3).reshape(A_KV, rows, GROUP_LANES)

    pat = group_lanes(bias[:, offs[0]:offs[2]]).reshape(A_KV, 2, CMP_PAT, GROUP_LANES).transpose(1, 0, 2, 3)
    far = jnp.repeat(rel_bias[FAR_BUCKET].reshape(A_KV, A_GROUP), TQ, axis=1).reshape(A_KV, 1, GROUP_LANES)
    cov_t = jnp.asarray(_cover_np(nc, nc - 1, ns, ns).T)
    o_c, sel = _cmp_attend(far, pa, kk, kvt, pat, cov_t, batch, seq)

    tab = group_lanes(bias[:, offs[2]:offs[5]]).reshape(A_KV, 3, TQ, GROUP_LANES).transpose(1, 0, 2, 3)
    tab = jnp.concatenate([tab * LOG2E, jnp.full((1,) + tab.shape[1:], NEG, F32)], axis=0)
    o_s = _slc_attend(pa, rs, st, sel, tab, batch, seq)
    o_w = _win_attend(pa, rw.reshape(batch, seq, ROW_W), wt, group_lanes(bias[:, offs[5]:offs[6]]), batch, seq)
    return o_c, o_s, o_w, sel


HALVES_PER_PAGE = 8
SEL_LANES = 256


def _gather_pages(copy_of, n_copies):
    b = pl.program_id(0)
    slot = b % 2

    @pl.when(b == 0)
    def _():
        for i in range(n_copies):
            copy_of(0, i, 0).start()

    @pl.when(b + 1 < pl.num_programs(0))
    def _():
        for i in range(n_copies):
            copy_of(b + 1, i, 1 - slot).start()

    for i in range(n_copies):
        copy_of(b, i, slot).wait()
    return slot


def _cmp_sample_kernel(pt_ref, cache_ref, q_ref, perm_ref, wcat_ref, const_ref, w2_ref, bias_ref, cov_ref,
                       o_ref, idx_ref, buf, sem, *, n_pages, p_len):
    def page_copy(seq, j, slot):
        return pltpu.make_async_copy(cache_ref.at[pt_ref[seq * n_pages + j]], buf.at[slot, j], sem.at[slot])

    slot = _gather_pages(page_copy, n_pages)
    pages = [buf.at[slot, j] for j in range(n_pages)]
    n_half = n_pages * HALVES_PER_PAGE
    perm = perm_ref[...]
    group = 8

    def move_lanes(g):
        tiles = [pages[j][kv].reshape(KVROW_W, PAGE_ROWS).astype(BF16)
                 for j in range(g * group, (g + 1) * group) for kv in range(A_KV)]
        return _dot(jnp.concatenate(tiles, axis=0), perm)

    blocks = [[], []]
    n_groups = n_pages // group
    moved = move_lanes(0)
    for g in range(n_groups):
        moved_next = move_lanes(g + 1) if g + 1 < n_groups else None
        for i in range(group * A_KV):
            kv = i % A_KV
            rows = moved[i * KVROW_W:(i + 1) * KVROW_W].T
            blocks[kv].append(jnp.concatenate(
                [rows[p * HALVES_PER_PAGE:(p + 1) * HALVES_PER_PAGE] for p in range(CMP_STRIDE)], axis=1))
        moved = moved_next
    halves = jnp.concatenate(blocks[0] + blocks[1], axis=0)
    kc = _compress_halves(halves, wcat_ref, const_ref, w2_ref, n_half)
    kc0, kc1 = kc[:n_half], kc[n_half:]
    q8 = q_ref[0]
    first = lax.broadcasted_iota(jnp.int32, (A_HEADS, n_half), 0) < A_GROUP

    def logits(keys):
        return lax.dot_general(q8, keys, (((1,), (1,)), ((), ())), preferred_element_type=F32, precision=HIGHEST)

    s = jnp.where(first, logits(kc0[:, :A_HD]), logits(kc1[:, :A_HD])) * ATT_SCALE + bias_ref[...]
    e, _, inv = _softmax_keys_on_lanes(s)
    p = e * inv
    pb = p.astype(BF16)
    first_o = lax.broadcasted_iota(jnp.int32, (A_HEADS, A_HD), 0) < A_GROUP
    o_ref[0] = jnp.where(first_o, _dot(pb, kc0[:, A_HD:].astype(BF16)), _dot(pb, kc1[:, A_HD:].astype(BF16)))
    hrow = lax.broadcasted_iota(jnp.int32, (A_HEADS, n_half), 0)
    imp0 = jnp.sum(jnp.where(first, p, 0.0), axis=0, keepdims=True)
    imp1 = jnp.sum(jnp.where(first, 0.0, p), axis=0, keepdims=True)
    imp = jnp.where(hrow == 0, imp0, jnp.where(hrow == 1, imp1, 0.0))
    score = _dot(imp, cov_ref[...], HIGHEST)
    n_slc = p_len // SLC_BLOCK + 1
    cur = p_len // SLC_BLOCK
    lane = lax.broadcasted_iota(jnp.int32, (A_HEADS, SEL_LANES), 1)
    forced = (lane == 0) | (lane == cur) | (lane == cur - 1)
    valid = lane * SLC_BLOCK <= p_len
    sc = jnp.where(forced, jnp.inf, jnp.where(valid, score, -jnp.inf))
    k_sel = float(min(N_SEL, n_slc))
    sub = lax.broadcasted_iota(jnp.int32, (SEL_LANES, SEL_LANES), 0)
    lan = lax.broadcasted_iota(jnp.int32, (SEL_LANES, SEL_LANES), 1)
    slot_l = lax.broadcasted_iota(jnp.int32, (SEL_LANES, LANE), 1).astype(F32)
    blk_s = lax.broadcasted_iota(jnp.int32, (SEL_LANES, LANE), 0).astype(F32)
    out_row = lax.broadcasted_iota(jnp.int32, (A_HEADS, LANE), 0)
    picks = jnp.zeros((A_HEADS, LANE), F32)
    for kv in range(A_KV):
        row = sc[kv:kv + 1, :]
        col = jnp.sum(jnp.where(sub == lan, row, 0.0), axis=1, keepdims=True)
        before_c = (lan < n_slc) & ((row > col) | ((row == col) & (lan < sub)))
        sel_c = (jnp.sum(before_c.astype(F32), axis=1, keepdims=True) < k_sel) & (sub[:, :1] < n_slc)
        before_r = (sub < n_slc) & ((col > row) | ((col == row) & (sub < lan)))
        sel_r = (jnp.sum(before_r.astype(F32), axis=0, keepdims=True) < k_sel) & (lan[:1] < n_slc)
        slot_c = jnp.sum(jnp.where((lan < sub) & sel_r, 1.0, 0.0), axis=1, keepdims=True)
        hit = sel_c & (slot_c == slot_l)
        picks_kv = jnp.sum(jnp.where(hit, blk_s, 0.0), axis=0, keepdims=True)
        picks = jnp.where(out_row == kv, picks_kv, picks)
    idx_ref[0] = picks.astype(jnp.int32)


def _cmp_sample(page_table, cache_pages, q3, wcat, const_row, w2bd, bias_cs, cov, p_len):
    nb, n_pages = page_table.shape
    n_half = n_pages * HALVES_PER_PAGE

    r = np.arange(PAGE_ROWS)
    perm = np.zeros((PAGE_ROWS, PAGE_ROWS), np.float32)
    perm[r, (r % CMP_STRIDE) * HALVES_PER_PAGE + r // CMP_STRIDE] = 1.0
    const2 = lambda b, pt: (0, 0)
    grid_spec = pltpu.PrefetchScalarGridSpec(
        num_scalar_prefetch=1,
        grid=(nb,),
        in_specs=[pl.BlockSpec(memory_space=pl.ANY),
                  pl.BlockSpec((1, A_HEADS, A_HD), lambda b, pt: (b, 0, 0)),
                  pl.BlockSpec((PAGE_ROWS, PAGE_ROWS), const2),
                  pl.BlockSpec((CMP_STRIDE * KVROW_W, HALF_W), const2),
                  pl.BlockSpec((1, KVROW_W), const2),
                  pl.BlockSpec((KVROW_W, KVROW_W), const2),
                  pl.BlockSpec((A_HEADS, n_half), const2),
                  pl.BlockSpec((n_half, SEL_LANES), const2)],
        out_specs=[pl.BlockSpec((1, A_HEADS, A_HD), lambda b, pt: (b, 0, 0)),
                   pl.BlockSpec((1, A_HEADS, LANE), lambda b, pt: (b, 0, 0))],
        scratch_shapes=[pltpu.VMEM((2, n_pages) + cache_pages.shape[1:], F32),
                        pltpu.SemaphoreType.DMA((2,))],
    )
    return pl.pallas_call(
        functools.partial(_cmp_sample_kernel, n_pages=n_pages, p_len=p_len),
        grid_spec=grid_spec,
        out_shape=[jax.ShapeDtypeStruct((nb, A_HEADS, A_HD), F32),
                   jax.ShapeDtypeStruct((nb, A_HEADS, LANE), jnp.int32)],
        compiler_params=_cparams("arbitrary"),
        name="cmp_sample",
    )(page_table.reshape(-1), cache_pages, q3, jnp.asarray(perm, BF16), wcat, const_row, w2bd, bias_cs, cov)


PAGE_ROWS = 128
BLOCKS_PER_PAGE = PAGE_ROWS // SLC_BLOCK


def _slc_sample_kernel(idx_ref, pt_ref, cache_ref, q_ref, snew_ref, win_ref, wnew_ref, wcol_ref, oc_ref, g_ref,
                       rbt_ref, bw_ref, near_ref, ha_ref, wbuf_ref, buf, sem, *, n_pages, p_len):
    n_blk = A_KV * N_SEL
    past_blocks = p_len // SLC_BLOCK

    def block_copy(seq, i, slot):
        kv = i // N_SEL
        blk = jnp.minimum(idx_ref[seq * n_blk + i], past_blocks - 1)
        page = pt_ref[seq * n_pages + blk // BLOCKS_PER_PAGE]
        return pltpu.make_async_copy(cache_ref.at[page, kv], buf.at[slot, i], sem.at[slot])

    slot = _gather_pages(block_copy, n_blk)
    blocks = [buf.at[slot, i] for i in range(n_blk)]
    b = pl.program_id(0)
    n_keys = N_SEL * PAGE_ROWS
    n_buf = win_ref.shape[-1]
    qf = q_ref[0]
    q8 = qf.astype(BF16)
    first_o = lax.broadcasted_iota(jnp.int32, (A_HEADS, A_HD), 0) < A_GROUP
    lane = lax.broadcasted_iota(jnp.int32, (1, n_keys), 1)
    slot = lane // PAGE_ROWS
    in_page = lane % PAGE_ROWS
    near = jnp.concatenate([near_ref[...]] * N_SEL, axis=1)
    far = rbt_ref[:, FAR_BUCKET:FAR_BUCKET + 1]
    bias_new = rbt_ref[:, 0:1]

    def new_key_logit(row_ref, kv):
        k_new = row_ref[0, :, kv * LANE:kv * LANE + A_HD]
        v_new = row_ref[0, :, kv * LANE + A_HD:(kv + 1) * LANE]
        return jnp.sum(qf * k_new, axis=1, keepdims=True) * ATT_SCALE + bias_new, v_new

    o_s, o_w = [], []
    for kv in range(A_KV):
        blk_of = jnp.zeros((1, n_keys), jnp.int32)
        has_new = False
        for j in range(N_SEL):
            blk = idx_ref[(b * A_KV + kv) * N_SEL + j]
            blk_of = jnp.where(slot == j, blk, blk_of)
            has_new = jnp.logical_or(has_new, blk == past_blocks)
        page_of = jnp.minimum(blk_of, past_blocks - 1) // BLOCKS_PER_PAGE
        pos = page_of * PAGE_ROWS + in_page
        valid = (pos // SLC_BLOCK == blk_of) & (pos < p_len)
        bias = jnp.where(page_of == n_pages - 1, near, far)
        kt = jnp.concatenate([blocks[kv * N_SEL + j][0] for j in range(N_SEL)], axis=1).astype(BF16)
        vt = jnp.concatenate([blocks[kv * N_SEL + j][1] for j in range(N_SEL)], axis=1).astype(BF16)
        s = jnp.where(valid, _dot(q8, kt) * ATT_SCALE + bias, NEG)
        s_new, v_new = new_key_logit(snew_ref, kv)
        s_new = jnp.where(has_new, s_new, NEG)
        e, e_new, inv = _softmax_keys_on_lanes(s, s_new)
        o_s.append((_nt(e.astype(BF16), vt) + e_new * v_new) * inv)
        sw = _dot(q8, win_ref[0, kv, 0].astype(BF16)) * ATT_SCALE + bw_ref[...]
        sw_new, vw_new = new_key_logit(wnew_ref, kv)
        e, e_new, inv = _softmax_keys_on_lanes(sw, sw_new)
        o_w.append((_nt(e.astype(BF16), win_ref[0, kv, 1].astype(BF16)) + e_new * vw_new) * inv)
    g = jax.nn.sigmoid(g_ref[0])
    ha_ref[0] = (g[0] * oc_ref[0] + g[1] * jnp.where(first_o, o_s[0], o_s[1])
                 + g[2] * jnp.where(first_o, o_w[0], o_w[1]))
    last = lax.broadcasted_iota(jnp.int32, (A_HD, n_buf), 1) == n_buf - 1
    for kv in range(A_KV):
        for c in range(2):
            r0 = (kv * 2 + c) * A_HD
            wbuf_ref[0, kv, c] = jnp.where(last, wcol_ref[0, r0:r0 + A_HD, :],
                                           pltpu.roll(win_ref[0, kv, c], n_buf - 1, 1))


def _slc_sample(idx, page_table, cache_t, q3, slc_new, win_t, win_new, win_new_col, o_c, gates, rb_t, bias_ws,
                bias_near, p_len):
    nb, n_pages = page_table.shape
    n_buf = win_t.shape[-1]
    per_seq3 = lambda b, i, p: (b, 0, 0)
    win_spec = pl.BlockSpec((1, A_KV, 2, A_HD, n_buf), lambda b, i, p: (b, 0, 0, 0, 0))
    grid_spec = pltpu.PrefetchScalarGridSpec(
        num_scalar_prefetch=2,
        grid=(nb,),
        in_specs=[pl.BlockSpec(memory_space=pl.ANY),
                  pl.BlockSpec((1, A_HEADS, A_HD), per_seq3),
                  pl.BlockSpec((1, 1, ROW_W), per_seq3),
                  win_spec,
                  pl.BlockSpec((1, 1, ROW_W), per_seq3),
                  pl.BlockSpec((1, ROW_W, 1), per_seq3),
                  pl.BlockSpec((1, A_HEADS, A_HD), per_seq3),
                  pl.BlockSpec((1, 3, A_HEADS, 1), lambda b, i, p: (b, 0, 0, 0)),
                  pl.BlockSpec((A_HEADS, N_BUCKETS), lambda b, i, p: (0, 0)),
                  pl.BlockSpec((A_HEADS, n_buf), lambda b, i, p: (0, 0)),
                  pl.BlockSpec((A_HEADS, PAGE_ROWS), lambda b, i, p: (0, 0))],
        out_specs=[pl.BlockSpec((1, A_HEADS, A_HD), per_seq3), win_spec],
        scratch_shapes=[pltpu.VMEM((2, A_KV * N_SEL) + cache_t.shape[2:], F32),
                        pltpu.SemaphoreType.DMA((2,))],
    )
    return pl.pallas_call(
        functools.partial(_slc_sample_kernel, n_pages=n_pages, p_len=p_len),
        grid_spec=grid_spec,
        out_shape=[jax.ShapeDtypeStruct((nb, A_HEADS, A_HD), F32),
                   jax.ShapeDtypeStruct(win_t.shape, F32)],
        compiler_params=_cparams("arbitrary"),
        name="slc_win_sample",
    )(idx.reshape(-1), page_table.reshape(-1), cache_t, q3, slc_new, win_t, win_new, win_new_col, o_c, gates,
      rb_t, bias_ws, bias_near)


def _rows_last(a):
    n = a.ndim
    return a.transpose(*range(n - 4), n - 3, n - 2, n - 1, n - 4)


def _rows_first(a):
    n = a.ndim
    return a.transpose(*range(n - 4), n - 1, n - 4, n - 3, n - 2)


def _nsa_sample(rel_bias, pa, rs_new, rw_new, cache_cmp, cache_slc, win_cache, page_table,
                bias, offs, cmp_w, const_row, p_len):
    wcat, w2bd = cmp_w
    nb, n_pages = page_table.shape
    n_half = p_len // CMP_STRIDE
    n_slc = p_len // SLC_BLOCK + 1
    n_buf = win_cache.shape[1]
    q3 = pa[:, :A_WIDTH].reshape(nb, A_HEADS, A_HD)
    gates = pa[:, 2 * A_WIDTH:2 * A_WIDTH + 3 * A_HEADS].reshape(nb, 3, A_HEADS, 1)
    bias_cs = bias[:, offs[6]:offs[7]].reshape(A_HEADS, -1)[:, :n_half]
    bias_ws = bias[:, offs[7]:offs[8]].reshape(A_HEADS, -1)[:, :n_buf]
    cov = jnp.asarray(_cover_np(n_half, n_half - 1, SEL_LANES, n_slc))
    o_c, picks = _cmp_sample(page_table, _rows_last(cache_cmp), q3, wcat, const_row, w2bd, bias_cs, cov, p_len)
    idx = picks[:, :A_KV, :N_SEL]
    ha, wbuf = _slc_sample(idx, page_table, _rows_last(cache_slc), q3, rs_new.reshape(nb, 1, ROW_W),
                           _rows_last(win_cache), rw_new.reshape(nb, 1, ROW_W), rw_new.reshape(nb, ROW_W, 1),
                           o_c, gates, rel_bias.T, bias_ws, bias[:, offs[8]], p_len)
    return ha.reshape(nb, A_WIDTH), idx, wbuf


def kernel(x_prompt, x_sample, cache_cmp_kv, cache_slc_kv, cache_win_kv, state_mlstm_C, state_mlstm_n, state_mlstm_m, page_table, c_prompt, c_sample, rel_bias, w_ada, b_ada, w_in, b_in, m_norm_g, cmp_pe, cmp_w1, cmp_b1, cmp_w2, w_out, b_out, ln_g, ln_b):
    B, T, _ = x_prompt.shape
    NB = x_sample.shape[0]
    n_pages = page_table.shape[1]
    p_len = n_pages * PAGE_ROWS
    depth = w_in.shape[0]
    assert depth == 1 and x_sample.shape[1] == 1 and cache_win_kv.shape[2] == WINDOW
    ids, offs = _static_ids(p_len)
    bias = _bias_tables(rel_bias, ids)
    x_p = x_prompt.reshape(B * T, D_MODEL)
    x_s = x_sample.reshape(NB, D_MODEL)
    l = 0
    n_mod = -(-(B + NB) // SUBLANE) * SUBLANE
    c_all = jnp.concatenate([c_prompt, c_sample, jnp.zeros((n_mod - B - NB, D_MODEL), F32)])
    shift, scale, gate = jnp.split(_adaln_mod(c_all, w_ada[l], b_ada[l]), 3, axis=-1)
    packed = _pack_in_proj(w_in[l], b_in[l], BF16)
    packed_f32 = _pack_in_proj(w_in[l], b_in[l], F32)
    cmp_w = _pack_compress(cmp_w1[l], cmp_w2[l])
    const_row = _compress_const(cmp_pe[l], cmp_w1[l], cmp_b1[l])
    w_out_b = w_out[l].astype(BF16)
    vecs = (b_out[l].reshape(1, -1), ln_g[l].reshape(1, -1), ln_b[l].reshape(1, -1))
    pm, pa, rc, rs, rw, ct, st, wt = _project(x_p, shift[:B, None], scale[:B, None], packed, B, 256)
    mix_m, c_p, n_p, m_p = _mlstm_prompt(pm, m_norm_g[l], B, T)
    o_c, o_s, o_w, _ = _nsa_prompt(rel_bias, pa, rc, rs, rw, st, wt, bias, offs, cmp_w, const_row, B, T)
    y_p = _out_prompt(x_p, mix_m, o_c, o_s, o_w, pa, gate[:B, None], w_out_b, *vecs, B, T)
    pm_s, pa_s, _, rs_s, rw_s, ct_s, st_s, wt_s = _project(x_s, shift[B:B + NB], scale[B:B + NB], packed_f32, 1, NB)
    mix_s, c_s, n_s, m_s = _mlstm_sample(pm_s, m_norm_g[l], state_mlstm_C[l], state_mlstm_n[l], state_mlstm_m[l])
    ha_s, _, wbuf_s = _nsa_sample(rel_bias, pa_s, rs_s, rw_s, cache_cmp_kv[l], cache_slc_kv[l],
                                  cache_win_kv[l], page_table, bias, offs, cmp_w, const_row, p_len)
    y_s = _out_sample(x_s, mix_s.reshape(NB, M_WIDTH), ha_s, pa_s, gate[B:B + NB], w_out_b, *vecs)

    def kv_prompt(a):
        return _rows_first(a.reshape(1, B, A_KV, 2, A_HD, a.shape[-1]))

    def kv_sample(a):
        return a.reshape(1, 1, A_KV, 2, A_HD, NB).transpose(0, 5, 1, 2, 3, 4)

    return (y_p.reshape(B, T, D_MODEL), y_s.reshape(NB, 1, D_MODEL),
            kv_prompt(ct), kv_sample(ct_s), kv_prompt(st), kv_sample(st_s),
            kv_prompt(wt[:, :, T - WINDOW:]), _rows_first(wbuf_s)[None],
            c_p[None], c_s[None], n_p[None], n_s[None], m_p[None, :, :, 0], m_s[None])
```
